```python
import jax, jax.numpy as jnp
from jax import lax
import numpy as np

D_MODEL = 1024
BATCH = 16
SEQ = 2048
DEPTH = 1

GRID_W = 64
D_FF = 2816
EPS = 1e-6
GLA_HEADS = 4
GLA_DK = 128
GLA_DV = 256
GLA_RANK = 16
GLA_TAU = 16.0
GLA_CHUNK = 64
ATT_Q_HEADS = 8
ATT_KV_HEADS = 2
ATT_HEAD_DIM = 128
ATT_BLOCK = 128
ROPE_THETA = 10000.0
GLA_KEY_W = GLA_HEADS * GLA_DK
GLA_VAL_W = GLA_HEADS * GLA_DV
ATT_Q_W = ATT_Q_HEADS * ATT_HEAD_DIM
ATT_KV_W = ATT_KV_HEADS * ATT_HEAD_DIM
IN_SPLITS = (GLA_KEY_W, GLA_KEY_W, GLA_VAL_W, GLA_VAL_W, GLA_RANK, GLA_RANK,
             ATT_Q_W, ATT_KV_W, ATT_KV_W, D_MODEL, D_MODEL)
N_IN = sum(IN_SPLITS)
IN_OFFSETS = tuple(int(s) for s in np.cumsum(IN_SPLITS)[:-1])

kernel_name = "hybrid_gla_axial_gqa_macaron_encoder"


def _rmsnorm(x, g):
    x32 = x.astype(jnp.float32)
    y = x32 * lax.rsqrt(jnp.mean(x32 * x32, axis=-1, keepdims=True) + EPS)
    return (y * g.astype(jnp.float32)).astype(x.dtype)


def _swiglu(h, w_in, w_out):
    gate, up = jnp.split(h @ w_in, 2, axis=-1)
    return (jax.nn.silu(gate) * up) @ w_out


def _heads(t, n):
    b, l, _ = t.shape
    return t.reshape(b, l, n, -1).transpose(0, 2, 1, 3)


def _gla_direction(q, k, v, g, strict):
    B, H, L, dk = q.shape
    dv = v.shape[-1]
    C = GLA_CHUNK
    n = L // C

    def to_chunks(t):
        return t.reshape(B, H, n, C, t.shape[-1]).transpose(2, 0, 1, 3, 4)

    idx = jnp.arange(C)
    mask = (idx[:, None] > idx[None, :]) if strict else (idx[:, None] >= idx[None, :])

    def step(S, inp):
        qi, ki, vi, gi = inp
        b = jnp.cumsum(gi, axis=-2)
        b_last = b[..., -1:, :]
        o_inter = jnp.einsum('bhcd,bhde->bhce', qi * jnp.exp(b), S)
        rel = jnp.where(mask[:, :, None], b[..., :, None, :] - b[..., None, :, :], -jnp.inf)
        scores = jnp.einsum('bhid,bhjd,bhijd->bhij', qi, ki, jnp.exp(rel))
        o_intra = jnp.einsum('bhij,bhje->bhie', scores, vi)
        S_new = (jnp.exp(b_last[..., 0, :])[..., None] * S
                 + jnp.einsum('bhjd,bhje->bhde', ki * jnp.exp(b_last - b), vi))
        return S_new, o_inter + o_intra

    S0 = jnp.zeros((B, H, dk, dv), jnp.float32)
    _, o = lax.scan(step, S0, (to_chunks(q), to_chunks(k), to_chunks(v), to_chunks(g)))
    return o.transpose(1, 2, 0, 3, 4).reshape(B, H, L, dv)


def _axial_rope_tables(L):
    rows = L // GRID_W
    row_pos = jnp.repeat(jnp.arange(rows), GRID_W).astype(jnp.float32)
    col_pos = jnp.tile(jnp.arange(GRID_W), rows).astype(jnp.float32)
    half = ATT_HEAD_DIM // 2
    inv_freq = ROPE_THETA ** (-jnp.arange(0, half, 2, dtype=jnp.float32) / half)
    ang_r = row_pos[:, None] * inv_freq
    ang_c = col_pos[:, None] * inv_freq
    return jnp.cos(ang_r), jnp.sin(ang_r), jnp.cos(ang_c), jnp.sin(ang_c)


def _rot_half(x, cos, sin):
    x1, x2 = jnp.split(x, 2, axis=-1)
    c, s = cos[:, None, :], sin[:, None, :]
    return jnp.concatenate([x1 * c - x2 * s, x2 * c + x1 * s], axis=-1)


def _axial_rope(x, tables):
    cos_r, sin_r, cos_c, sin_c = tables
    x32 = x.astype(jnp.float32)
    xr, xc = jnp.split(x32, 2, axis=-1)
    out = jnp.concatenate([_rot_half(xr, cos_r, sin_r), _rot_half(xc, cos_c, sin_c)], axis=-1)
    return out.astype(x.dtype)


def _block_attention(q, k, v):
    B, Hk, G, L, hd = q.shape
    nb = L // ATT_BLOCK
    qb = q.reshape(B, Hk, G, nb, ATT_BLOCK, hd).transpose(3, 0, 1, 2, 4, 5)
    scale = hd ** -0.5

    def one(qi):
        s = jnp.einsum('bkgqd,bksd->bkgqs', qi, k).astype(jnp.float32) * scale
        p = jax.nn.softmax(s, axis=-1).astype(v.dtype)
        return jnp.einsum('bkgqs,bksd->bkgqd', p, v)

    o = lax.map(one, qb)
    return o.transpose(1, 2, 3, 0, 4, 5).reshape(B, Hk, G, L, hd)


def _token_mixers(u, w_in, up_f, bias_f, up_b, bias_b, gla_out_g, w_branch_a,
                  q_norm_g, k_norm_g, w_branch_b, w_out):
    B, L, _ = u.shape
    (gq, gk, gv, gr, za_f, za_b, aq, ak, av, ga, gb) = jnp.split(u @ w_in, IN_OFFSETS, axis=-1)

    log_a_f = jax.nn.log_sigmoid((za_f @ up_f + bias_f).astype(jnp.float32)) / GLA_TAU
    log_a_b = jax.nn.log_sigmoid((za_b @ up_b + bias_b).astype(jnp.float32)) / GLA_TAU
    q_h = _heads(gq, GLA_HEADS) * (GLA_DK ** -0.5)
    k_h = _heads(gk, GLA_HEADS)
    v_h = _heads(gv, GLA_HEADS)
    gf_h = _heads(log_a_f, GLA_HEADS)
    gb_h = _heads(log_a_b, GLA_HEADS)
    o_fwd = _gla_direction(q_h, k_h, v_h, gf_h, strict=False)
    flip = lambda t: t[..., ::-1, :]
    o_bwd = flip(_gla_direction(flip(q_h), flip(k_h), flip(v_h), flip(gb_h), strict=True))
    o = o_fwd + o_bwd
    o = _rmsnorm(o, gla_out_g[:, None, :]).astype(u.dtype)
    o = o.transpose(0, 2, 1, 3).reshape(B, L, GLA_VAL_W)
    y_a = (o * jax.nn.silu(gr)) @ w_branch_a

    tables = _axial_rope_tables(L)
    qa = aq.reshape(B, L, ATT_Q_HEADS, ATT_HEAD_DIM)
    ka = ak.reshape(B, L, ATT_KV_HEADS, ATT_HEAD_DIM)
    va = av.reshape(B, L, ATT_KV_HEADS, ATT_HEAD_DIM)
    qa = _axial_rope(_rmsnorm(qa, q_norm_g), tables)
    ka = _axial_rope(_rmsnorm(ka, k_norm_g), tables)
    group = ATT_Q_HEADS // ATT_KV_HEADS
    qa = qa.transpose(0, 2, 1, 3).reshape(B, ATT_KV_HEADS, group, L, ATT_HEAD_DIM)
    ka = ka.transpose(0, 2, 1, 3)
    va = va.transpose(0, 2, 1, 3)
    ob = _block_attention(qa, ka, va)
    ob = ob.reshape(B, ATT_Q_HEADS, L, ATT_HEAD_DIM).transpose(0, 2, 1, 3).reshape(B, L, ATT_Q_W)
    y_b = ob @ w_branch_b

    merged = jax.nn.sigmoid(ga) * y_a + jax.nn.sigmoid(gb) * y_b
    return merged @ w_out


def setup_inputs(seed: int = 0) -> dict:
    key = jax.random.key(seed)
    ks = iter(jax.random.split(key, 32))
    f32 = jnp.float32

    def w(shape, fan_in):
        return jax.random.normal(next(ks), shape, f32) * (fan_in ** -0.5)

    def gain(shape):
        return 1.0 + 0.05 * jax.random.normal(next(ks), shape, f32)

    def bias(shape):
        return 0.1 * jax.random.normal(next(ks), shape, f32)

    Dp = DEPTH
    return {
        "x": jax.random.normal(next(ks), (BATCH, SEQ, D_MODEL), f32),
        "ffn1_pre_g": gain((Dp, D_MODEL)),
        "ffn1_w_in": w((Dp, D_MODEL, 2 * D_FF), D_MODEL),
        "ffn1_w_out": w((Dp, D_FF, D_MODEL), D_FF),
        "ffn1_post_g": gain((Dp, D_MODEL)),
        "mix_pre_g": gain((Dp, D_MODEL)),
        "w_in": w((Dp, D_MODEL, N_IN), D_MODEL),
        "gla_decay_up_f": w((Dp, GLA_RANK, GLA_KEY_W), GLA_RANK),
        "gla_decay_bias_f": bias((Dp, GLA_KEY_W)),
        "gla_decay_up_b": w((Dp, GLA_RANK, GLA_KEY_W), GLA_RANK),
        "gla_decay_bias_b": bias((Dp, GLA_KEY_W)),
        "gla_out_g": gain((Dp, GLA_HEADS, GLA_DV)),
        "w_branch_a": w((Dp, GLA_VAL_W, D_MODEL), GLA_VAL_W),
        "att_q_norm_g": gain((Dp, ATT_HEAD_DIM)),
        "att_k_norm_g": gain((Dp, ATT_HEAD_DIM)),
        "w_branch_b": w((Dp, ATT_Q_W, D_MODEL), ATT_Q_W),
        "w_out": w((Dp, D_MODEL, D_MODEL), D_MODEL),
        "mix_post_g": gain((Dp, D_MODEL)),
        "ffn2_pre_g": gain((Dp, D_MODEL)),
        "ffn2_w_in": w((Dp, D_MODEL, 2 * D_FF), D_MODEL),
        "ffn2_w_out": w((Dp, D_FF, D_MODEL), D_FF),
        "ffn2_post_g": gain((Dp, D_MODEL)),
    }


def reference(x, ffn1_pre_g, ffn1_w_in, ffn1_w_out, ffn1_post_g, mix_pre_g, w_in,
              gla_decay_up_f, gla_decay_bias_f, gla_decay_up_b, gla_decay_bias_b,
              gla_out_g, w_branch_a, att_q_norm_g, att_k_norm_g, w_branch_b, w_out,
              mix_post_g, ffn2_pre_g, ffn2_w_in, ffn2_w_out, ffn2_post_g):
    h = x
    for l in range(DEPTH):
        f1 = _swiglu(_rmsnorm(h, ffn1_pre_g[l]), ffn1_w_in[l], ffn1_w_out[l])
        h = h + 0.5 * _rmsnorm(f1, ffn1_post_g[l])
        m = _token_mixers(_rmsnorm(h, mix_pre_g[l]), w_in[l],
                          gla_decay_up_f[l], gla_decay_bias_f[l],
                          gla_decay_up_b[l], gla_decay_bias_b[l],
                          gla_out_g[l], w_branch_a[l],
                          att_q_norm_g[l], att_k_norm_g[l], w_branch_b[l], w_out[l])
        h = h + _rmsnorm(m, mix_post_g[l])
        f2 = _swiglu(_rmsnorm(h, ffn2_pre_g[l]), ffn2_w_in[l], ffn2_w_out[l])
        h = h + 0.5 * _rmsnorm(f2, ffn2_post_g[l])
    return h
```

```python
import functools

import numpy as np
import jax
import jax.numpy as jnp
from jax import lax
from jax.experimental import pallas as pl
from jax.experimental.pallas import tpu as pltpu

F32 = jnp.float32
BF16 = jnp.bfloat16

D_MODEL = 1024
GRID_W = 64
D_FF = 2816
EPS = 1e-6
GLA_HEADS = 4
GLA_DK = 128
GLA_DV = 256
GLA_RANK = 16
GLA_TAU = 16.0
ATT_Q_HEADS = 8
ATT_KV_HEADS = 2
ATT_HEAD_DIM = 128
ATT_GROUP = ATT_Q_HEADS // ATT_KV_HEADS
ROPE_THETA = 10000.0
GLA_KEY_W = GLA_HEADS * GLA_DK
GLA_VAL_W = GLA_HEADS * GLA_DV
ATT_Q_W = ATT_Q_HEADS * ATT_HEAD_DIM
ATT_KV_W = ATT_KV_HEADS * ATT_HEAD_DIM
IN_SPLITS = (GLA_KEY_W, GLA_KEY_W, GLA_VAL_W, GLA_VAL_W, GLA_RANK, GLA_RANK,
             ATT_Q_W, ATT_KV_W, ATT_KV_W, D_MODEL, D_MODEL)
IN_OFFSETS = tuple(int(s) for s in np.cumsum(IN_SPLITS)[:-1])

V7X_LANES = 128
V7X_VMEM_BYTES = 64 * 1024 * 1024
VMEM_LIMIT = V7X_VMEM_BYTES - 8 * 1024 * 1024

ROW_TILE = 512
ATT_Q_TILE = 256
GLA_C = 64
GLA_LEVELS = GLA_C.bit_length() - 1
FF_CHUNKS = ((0, 1024), (1024, 1024), (2048, 768))


def _const_spec(shape):
    zeros = (0,) * len(shape)
    return pl.BlockSpec(shape, lambda *_: zeros, pipeline_mode=pl.Buffered(1))


def _params(n_axes):
    return pltpu.CompilerParams(dimension_semantics=("arbitrary",) * n_axes,
                                vmem_limit_bytes=VMEM_LIMIT)


def _rms(x, g):
    ms = jnp.mean(x * x, axis=-1, keepdims=True)
    return x * lax.rsqrt(ms + EPS) * g


def _dot(a, b):
    return jnp.dot(a, b, preferred_element_type=F32)


def _dot_nt(a, b):
    return lax.dot_general(a, b, (((1,), (1,)), ((), ())), preferred_element_type=F32)


def _dot_tn(a, b):
    return lax.dot_general(a, b, (((0,), (0,)), ((), ())), preferred_element_type=F32)


def _ffn_kernel(x_ref, gpre_ref, win_ref, wout_ref, gpost_ref, o_ref):
    x = x_ref[...]
    xn = _rms(x, gpre_ref[...]).astype(BF16)
    acc = None
    for c0, cw in FF_CHUNKS:
        gate = _dot(xn, win_ref[:, c0:c0 + cw])
        up = _dot(xn, win_ref[:, D_FF + c0:D_FF + c0 + cw])
        act = (gate * jax.nn.sigmoid(gate) * up).astype(BF16)
        part = _dot(act, wout_ref[c0:c0 + cw, :])
        acc = part if acc is None else acc + part
    o_ref[...] = x + 0.5 * _rms(acc, gpost_ref[...])


def _ffn(x, g_pre, w_in, w_out, g_post):
    t = x.shape[0]
    tm = min(ROW_TILE, t)
    row = pl.BlockSpec((tm, D_MODEL), lambda i: (i, 0))
    return pl.pallas_call(
        _ffn_kernel,
        grid=(t // tm,),
        in_specs=[row, _const_spec((1, D_MODEL)), _const_spec((D_MODEL, 2 * D_FF)),
                  _const_spec((D_FF, D_MODEL)), _const_spec((1, D_MODEL))],
        out_specs=row,
        out_shape=jax.ShapeDtypeStruct((t, D_MODEL), F32),
        compiler_params=_params(1),
        name="ffn",
    )(x, g_pre, w_in, w_out, g_post)


def _log_sigmoid(z):
    return jnp.minimum(z, 0.0) - jnp.log(1.0 + jnp.exp(-jnp.abs(z)))


def _rope(x, cos, sin_up, sin_dn):
    return (x * cos + pltpu.roll(x, V7X_LANES - 32, axis=1) * sin_up
            + pltpu.roll(x, 32, axis=1) * sin_dn)


def _proj_kernel(h_ref, gmix_ref, wq_ref, wk_ref, wv_ref, wr_ref, wz_ref, upf_ref, upb_ref,
                 bf_ref, bb_ref, waq_ref, wak_ref, wav_ref, wga_ref, wgb_ref,
                 qg_ref, kg_ref, cos_ref, sup_ref, sdn_ref,
                 gq_o, gk_o, gv_o, sr_o, lf_o, lb_o, aq_o, ak_o, av_o, sga_o, sgb_o):
    u = _rms(h_ref[...], gmix_ref[...]).astype(BF16)

    gq_o[...] = (_dot(u, wq_ref[...]) * (GLA_DK ** -0.5)).astype(BF16)
    gk_o[...] = _dot(u, wk_ref[...]).astype(BF16)
    gv_o[...] = _dot(u, wv_ref[...]).astype(BF16)
    gr = _dot(u, wr_ref[...])
    sr_o[...] = (gr * jax.nn.sigmoid(gr)).astype(BF16)

    za = _dot(u, wz_ref[...]).astype(BF16)
    lf_o[...] = _log_sigmoid(_dot(za, upf_ref[...]) + bf_ref[...]) * (1.0 / GLA_TAU)
    lb_o[...] = _log_sigmoid(_dot(za, upb_ref[...]) + bb_ref[...]) * (1.0 / GLA_TAU)

    cos, sup, sdn = cos_ref[...], sup_ref[...], sdn_ref[...]
    aq = _dot(u, waq_ref[...])
    qg = qg_ref[...] * (ATT_HEAD_DIM ** -0.5)
    for hd in range(ATT_Q_HEADS):
        sl = slice(hd * ATT_HEAD_DIM, (hd + 1) * ATT_HEAD_DIM)
        aq_o[:, sl] = _rope(_rms(aq[:, sl], qg), cos, sup, sdn).astype(BF16)
    ak = _dot(u, wak_ref[...])
    for hd in range(ATT_KV_HEADS):
        sl = slice(hd * ATT_HEAD_DIM, (hd + 1) * ATT_HEAD_DIM)
        ak_o[:, sl] = _rope(_rms(ak[:, sl], kg_ref[...]), cos, sup, sdn).astype(BF16)
    av_o[...] = _dot(u, wav_ref[...]).astype(BF16)

    sga_o[...] = jax.nn.sigmoid(_dot(u, wga_ref[...])).astype(BF16)
    sgb_o[...] = jax.nn.sigmoid(_dot(u, wgb_ref[...])).astype(BF16)


def _rope_tables(seq):
    pos = jnp.arange(seq)
    row_pos = (pos // GRID_W).astype(F32)
    col_pos = (pos % GRID_W).astype(F32)
    half = ATT_HEAD_DIM // 2
    inv_freq = ROPE_THETA ** (-jnp.arange(0, half, 2, dtype=F32) / half)
    ang_r = row_pos[:, None] * inv_freq
    ang_c = col_pos[:, None] * inv_freq
    zero = jnp.zeros_like(ang_r)
    cos = jnp.concatenate([jnp.cos(ang_r)] * 2 + [jnp.cos(ang_c)] * 2, axis=-1)
    sin_up = jnp.concatenate([-jnp.sin(ang_r), zero, -jnp.sin(ang_c), zero], axis=-1)
    sin_dn = jnp.concatenate([zero, jnp.sin(ang_r), zero, jnp.sin(ang_c)], axis=-1)
    return cos, sin_up, sin_dn


def _proj(h, seq, g_mix, w_parts, up_f, up_b, bias_f, bias_b, q_gain, k_gain):
    t = h.shape[0]
    tm = min(ROW_TILE, seq)
    tiles_per_seq = seq // tm
    cos, sin_up, sin_dn = _rope_tables(seq)

    def row(width):
        return pl.BlockSpec((tm, width), lambda i: (i, 0))

    table = pl.BlockSpec((tm, ATT_HEAD_DIM), lambda i: (i % tiles_per_seq, 0))
    wq, wk, wv, wr, wz, waq, wak, wav, wga, wgb = w_parts
    consts_a = (g_mix, wq, wk, wv, wr, wz, up_f, up_b, bias_f, bias_b, waq, wak, wav, wga, wgb,
                q_gain, k_gain)
    out_widths = (GLA_KEY_W, GLA_KEY_W, GLA_VAL_W, GLA_VAL_W, GLA_KEY_W, GLA_KEY_W,
                  ATT_Q_W, ATT_KV_W, ATT_KV_W, D_MODEL, D_MODEL)
    out_dtypes = (BF16, BF16, BF16, BF16, F32, F32, BF16, BF16, BF16, BF16, BF16)
    return pl.pallas_call(
        _proj_kernel,
        grid=(t // tm,),
        in_specs=[row(D_MODEL)] + [_const_spec(c.shape) for c in consts_a] + [table] * 3,
        out_specs=[row(w) for w in out_widths],
        out_shape=[jax.ShapeDtypeStruct((t, w), d) for w, d in zip(out_widths, out_dtypes)],
        compiler_params=_params(1),
        name="proj",
    )(h, *consts_a, cos, sin_up, sin_dn)


def _gla_constants():
    c, nl = GLA_C, GLA_LEVELS
    i = np.arange(c)[:, None]
    m = np.arange(c)[None, :]
    sum_f = [m <= i, m > i]
    sum_b = [m >= i, m < i]
    role_f, role_b, mask_f, mask_b = [], [], [], []
    for lvl in range(nl):
        s = c >> (lvl + 1)
        start = (i // (2 * s)) * 2 * s
        mid = start + s
        upper = (i % (2 * s)) >= s
        sum_f.append(np.where(upper, (m >= mid) & (m <= i), (m > i) & (m < mid)))
        sum_b.append(np.where(upper, (m >= mid) & (m < i), (m >= i) & (m < mid)))
        same = (i // (2 * s)) == (m // (2 * s))
        m_upper = (m % (2 * s)) >= s
        role_f.append(np.broadcast_to(upper, (c, V7X_LANES)))
        role_b.append(np.broadcast_to(~upper, (c, V7X_LANES)))
        mask_f.append(same & upper & ~m_upper)
        mask_b.append(same & ~upper & m_upper)
    sum_f = np.concatenate(sum_f, axis=0).astype(np.float32)
    sum_b = np.concatenate(sum_b, axis=0).astype(np.float32)
    sum_f = np.concatenate([sum_f, sum_f], axis=1)
    sum_b = np.concatenate([sum_b, sum_b], axis=1)
    roles = np.stack(role_f + role_b).astype(np.float32)
    masks = np.stack(mask_f + mask_b + [i == m]).astype(np.float32)
    return (jnp.asarray(sum_f, BF16), jnp.asarray(sum_b, BF16), jnp.asarray(roles), jnp.asarray(masks))


def _gla_kernel(q_ref, k_ref, v_ref, sr_ref, gf_ref, gb_ref, gain_ref,
                sumf_ref, sumb_ref, role_ref, mask_ref,
                o_ref,
                acc_ref, qeb_ref, kdb_ref, ab_ref, sf_ref, sb_ref):
    c, nl = GLA_C, GLA_LEVELS
    n_chunks = q_ref.shape[0] // c

    def exponentials(g, sum_ref):
        g_hi = g.astype(BF16)
        g_lo = (g - g_hi.astype(F32)).astype(BF16)
        return jnp.exp(_dot(sum_ref[...], jnp.concatenate([g_hi, g_lo], axis=0)))

    sf_ref[...] = jnp.zeros_like(sf_ref)
    sb_ref[...] = jnp.zeros_like(sb_ref)

    def forward(i, carry):
        rows = pl.ds(pl.multiple_of(i * c, c), c)
        q, k, v = q_ref[rows, :], k_ref[rows, :], v_ref[rows, :]
        qf, kf = q.astype(F32), k.astype(F32)
        ef = exponentials(gf_ref[rows, :], sumf_ref)
        eb = exponentials(gb_ref[rows, :], sumb_ref)

        a = _dot_nt(q, k) * mask_ref[2 * nl]
        for lvl in range(nl):
            blk = slice((2 + lvl) * c, (3 + lvl) * c)
            xf = (jnp.where(role_ref[lvl] > 0.5, qf, kf) * ef[blk]).astype(BF16)
            a = a + _dot_nt(xf, xf) * mask_ref[lvl]
            xb = (jnp.where(role_ref[nl + lvl] > 0.5, qf, kf) * eb[blk]).astype(BF16)
            a = a + _dot_nt(xb, xb) * mask_ref[nl + lvl]

        state = sf_ref[...]
        qe = (qf * ef[0:c]).astype(BF16)
        acc_ref[rows, :] = _dot(a.astype(BF16), v) + _dot_nt(qe, state.astype(BF16))
        kd = (kf * ef[c:2 * c]).astype(BF16)
        sf_ref[...] = state * ef[c - 1:c] + _dot_tn(v, kd)

        qeb_ref[rows, :] = (qf * eb[0:c]).astype(BF16)
        kdb_ref[rows, :] = (kf * eb[c:2 * c]).astype(BF16)
        ab_ref[pl.ds(pl.multiple_of(i * 8, 8), 8), :] = jnp.broadcast_to(eb[0:1], (8, GLA_DK))
        return carry

    lax.fori_loop(0, n_chunks, forward, 0)

    def backward(step, carry):
        i = n_chunks - 1 - step
        rows = pl.ds(pl.multiple_of(i * c, c), c)
        state = sb_ref[...]
        o = acc_ref[rows, :] + _dot_nt(qeb_ref[rows, :], state.astype(BF16))
        decay = ab_ref[pl.ds(pl.multiple_of(i * 8, 8), 8), :][0:1]
        sb_ref[...] = state * decay + _dot_tn(v_ref[rows, :], kdb_ref[rows, :])
        o_ref[rows, :] = (_rms(o, gain_ref[...]) * sr_ref[rows, :].astype(F32)).astype(BF16)
        return carry

    lax.fori_loop(0, n_chunks, backward, 0)


def _gla(gq, gk, gv, sr, lgf, lgb, gain, batch, seq):
    consts = _gla_constants()
    n_chunks = seq // GLA_C

    def head(width):
        return pl.BlockSpec((None, seq, width), lambda b, h: (b, 0, h))

    return pl.pallas_call(
        _gla_kernel,
        grid=(batch, GLA_HEADS),
        in_specs=[head(GLA_DK), head(GLA_DK), head(GLA_DV), head(GLA_DV), head(GLA_DK), head(GLA_DK),
                  pl.BlockSpec((None, 1, GLA_DV), lambda b, h: (h, 0, 0))]
                 + [_const_spec(cst.shape) for cst in consts],
        out_specs=head(GLA_DV),
        out_shape=jax.ShapeDtypeStruct((batch, seq, GLA_VAL_W), BF16),
        scratch_shapes=[pltpu.VMEM((seq, GLA_DV), F32),
                        pltpu.VMEM((seq, GLA_DK), BF16),
                        pltpu.VMEM((seq, GLA_DK), BF16),
                        pltpu.VMEM((n_chunks * 8, GLA_DK), F32),
                        pltpu.VMEM((GLA_DV, GLA_DK), F32),
                        pltpu.VMEM((GLA_DV, GLA_DK), F32)],
        compiler_params=_params(2),
        name="gla",
    )(gq, gk, gv, sr, lgf, lgb, gain, *consts)


def _attn_kernel(q_ref, k_ref, v_ref, o_ref):
    k = k_ref[...]
    v = v_ref[...]
    for g in range(ATT_GROUP):
        sl = slice(g * ATT_HEAD_DIM, (g + 1) * ATT_HEAD_DIM)
        s = _dot_nt(q_ref[:, sl], k)
        p = jnp.exp(s - jnp.max(s, axis=-1, keepdims=True))
        denom = jnp.sum(p, axis=-1, keepdims=True)
        o_ref[:, sl] = (_dot(p.astype(BF16), v) / denom).astype(BF16)


def _attn(aq, ak, av, batch, seq):
    tq = min(ATT_Q_TILE, seq)
    gw = ATT_GROUP * ATT_HEAD_DIM
    qspec = pl.BlockSpec((None, tq, gw), lambda b, h, i: (b, i, h))
    kvspec = pl.BlockSpec((None, seq, ATT_HEAD_DIM), lambda b, h, i: (b, 0, h))
    return pl.pallas_call(
        _attn_kernel,
        grid=(batch, ATT_KV_HEADS, seq // tq),
        in_specs=[qspec, kvspec, kvspec],
        out_specs=qspec,
        out_shape=jax.ShapeDtypeStruct((batch, seq, ATT_Q_W), BF16),
        compiler_params=_params(3),
        name="attn",
    )(aq, ak, av)


def _merge_kernel(h_ref, oa_ref, ob_ref, sga_ref, sgb_ref, wa_ref, wb_ref, wo_ref, gpost_ref, o_ref):
    ya = _dot(oa_ref[...], wa_ref[...])
    yb = _dot(ob_ref[...], wb_ref[...])
    merged = sga_ref[...].astype(F32) * ya + sgb_ref[...].astype(F32) * yb
    m = _dot(merged.astype(BF16), wo_ref[...])
    o_ref[...] = h_ref[...] + _rms(m, gpost_ref[...])


def _merge(h, oa, ob, sga, sgb, w_a, w_b, w_o, g_post):
    t = h.shape[0]
    tm = min(ROW_TILE, t)
    row = pl.BlockSpec((tm, D_MODEL), lambda i: (i, 0))
    sq = _const_spec((D_MODEL, D_MODEL))
    return pl.pallas_call(
        _merge_kernel,
        grid=(t // tm,),
        in_specs=[row] * 5 + [sq, sq, sq, _const_spec((1, D_MODEL))],
        out_specs=row,
        out_shape=jax.ShapeDtypeStruct((t, D_MODEL), F32),
        compiler_params=_params(1),
        name="merge",
    )(h, oa, ob, sga, sgb, w_a, w_b, w_o, g_post)


def _split_w_in(w_in):
    gq, gk, gv, gr, za_f, za_b, aq, ak, av, ga, gb = jnp.split(w_in.astype(BF16), IN_OFFSETS, axis=-1)
    pad = jnp.zeros((D_MODEL, V7X_LANES - 2 * GLA_RANK), BF16)
    return gq, gk, gv, gr, jnp.concatenate([za_f, za_b, pad], axis=-1), aq, ak, av, ga, gb


def _pad_decay_up(up, first_row):
    out = jnp.zeros((V7X_LANES, GLA_KEY_W), BF16)
    return lax.dynamic_update_slice(out, up.astype(BF16), (first_row, 0))


def kernel(x, ffn1_pre_g, ffn1_w_in, ffn1_w_out, ffn1_post_g, mix_pre_g, w_in, gla_decay_up_f,
           gla_decay_bias_f, gla_decay_up_b, gla_decay_bias_b, gla_out_g, w_branch_a, att_q_norm_g,
           att_k_norm_g, w_branch_b, w_out, mix_post_g, ffn2_pre_g, ffn2_w_in, ffn2_w_out, ffn2_post_g):
    batch, seq, d = x.shape
    assert d == D_MODEL and seq % max(GLA_C, ATT_Q_TILE, GRID_W) == 0
    assert seq % ROW_TILE == 0 or seq < ROW_TILE
    depth = w_in.shape[0]
    t = batch * seq
    h = x.reshape(t, d)
    vec = lambda g: g.reshape(1, -1).astype(F32)
    for l in range(depth):
        h = _ffn(h, vec(ffn1_pre_g[l]), ffn1_w_in[l].astype(BF16), ffn1_w_out[l].astype(BF16),
                 vec(ffn1_post_g[l]))
        gq, gk, gv, sr, lgf, lgb, aq, ak, av, sga, sgb = _proj(
            h, seq, vec(mix_pre_g[l]), _split_w_in(w_in[l]),
            _pad_decay_up(gla_decay_up_f[l], 0), _pad_decay_up(gla_decay_up_b[l], GLA_RANK),
            vec(gla_decay_bias_f[l]), vec(gla_decay_bias_b[l]),
            vec(att_q_norm_g[l]), vec(att_k_norm_g[l]))
        per_seq = lambda a: a.reshape(batch, seq, a.shape[-1])
        oa = _gla(per_seq(gq), per_seq(gk), per_seq(gv), per_seq(sr), per_seq(lgf), per_seq(lgb),
                  gla_out_g[l].reshape(GLA_HEADS, 1, GLA_DV).astype(F32), batch, seq)
        ob = _attn(per_seq(aq), per_seq(ak), per_seq(av), batch, seq)
        h = _merge(h, oa.reshape(t, GLA_VAL_W), ob.reshape(t, ATT_Q_W), sga, sgb,
                   w_branch_a[l].astype(BF16), w_branch_b[l].astype(BF16), w_out[l].astype(BF16),
                   vec(mix_post_g[l]))
        h = _ffn(h, vec(ffn2_pre_g[l]), ffn2_w_in[l].astype(BF16), ffn2_w_out[l].astype(BF16),
                 vec(ffn2_post_g[l]))
    return h.reshape(batch, seq, d)
```

```python
import functools

import numpy as np
import jax
import jax.numpy as jnp
from jax import lax
from jax.experimental import pallas as pl
from jax.experimental.pallas import tpu as pltpu

F32 = jnp.float32
BF16 = jnp.bfloat16

D_MODEL = 1024
GRID_W = 64
D_FF = 2816
EPS = 1e-6
GLA_HEADS = 4
GLA_DK = 128
GLA_DV = 256
GLA_RANK = 16
GLA_TAU = 16.0
ATT_Q_HEADS = 8
ATT_KV_HEADS = 2
ATT_HEAD_DIM = 128
ATT_GROUP = ATT_Q_HEADS // ATT_KV_HEADS
ROPE_THETA = 10000.0
GLA_KEY_W = GLA_HEADS * GLA_DK
GLA_VAL_W = GLA_HEADS * GLA_DV
ATT_Q_W = ATT_Q_HEADS * ATT_HEAD_DIM
ATT_KV_W = ATT_KV_HEADS * ATT_HEAD_DIM
IN_SPLITS = (GLA_KEY_W, GLA_KEY_W, GLA_VAL_W, GLA_VAL_W, GLA_RANK, GLA_RANK,
             ATT_Q_W, ATT_KV_W, ATT_KV_W, D_MODEL, D_MODEL)
IN_OFFSETS = tuple(int(s) for s in np.cumsum(IN_SPLITS)[:-1])
LOG2_E = 1.4426950408889634

V7X_LANES = 128
V7X_VMEM_BYTES = 64 * 1024 * 1024
VMEM_LIMIT = V7X_VMEM_BYTES - 8 * 1024 * 1024

ROW_TILE = 512
ATT_Q_TILE = 256
ATT_SHIFT_LIMIT_LOG2 = 60.0
GLA_C = 64
GLA_LEVELS = GLA_C.bit_length() - 1
GLA_UNROLL = 2
GLA_SLAB = 64
GLA_WIDE_C = 256
GLA_MILD_LOG2 = 48.0
FF_CHUNKS = ((0, 1024), (1024, 1024), (2048, 768))


def _const_spec(shape):
    zeros = (0,) * len(shape)
    return pl.BlockSpec(shape, lambda *_: zeros, pipeline_mode=pl.Buffered(1))


def _params(n_axes):
    return pltpu.CompilerParams(dimension_semantics=("arbitrary",) * n_axes,
                                vmem_limit_bytes=VMEM_LIMIT)


def _rms(x, g):
    ms = jnp.mean(x * x, axis=-1, keepdims=True)
    return x * lax.rsqrt(ms + EPS) * g


def _dot(a, b):
    return jnp.dot(a, b, preferred_element_type=F32)


def _dot_nt(a, b):
    return lax.dot_general(a, b, (((1,), (1,)), ((), ())), preferred_element_type=F32)


def _dot_tn(a, b):
    return lax.dot_general(a, b, (((0,), (0,)), ((), ())), preferred_element_type=F32)


def _ffn_kernel(x_ref, gpre_ref, win_ref, wout_ref, gpost_ref, o_ref):
    x = x_ref[...]
    xn = _rms(x, gpre_ref[...]).astype(BF16)
    acc = None
    for c0, cw in FF_CHUNKS:
        gate = _dot(xn, win_ref[:, c0:c0 + cw])
        up = _dot(xn, win_ref[:, D_FF + c0:D_FF + c0 + cw])
        act = (gate * jax.nn.sigmoid(gate) * up).astype(BF16)
        part = _dot(act, wout_ref[c0:c0 + cw, :])
        acc = part if acc is None else acc + part
    o_ref[...] = x + 0.5 * _rms(acc, gpost_ref[...])


def _ffn(x, g_pre, w_in, w_out, g_post):
    t = x.shape[0]
    tm = min(ROW_TILE, t)
    row = pl.BlockSpec((tm, D_MODEL), lambda i: (i, 0))
    return pl.pallas_call(
        _ffn_kernel,
        grid=(t // tm,),
        in_specs=[row, _const_spec((1, D_MODEL)), _const_spec((D_MODEL, 2 * D_FF)),
                  _const_spec((D_FF, D_MODEL)), _const_spec((1, D_MODEL))],
        out_specs=row,
        out_shape=jax.ShapeDtypeStruct((t, D_MODEL), F32),
        compiler_params=_params(1),
        name="ffn",
    )(x, g_pre, w_in, w_out, g_post)


def _log_sigmoid(z):
    return jnp.minimum(z, 0.0) - jnp.log(1.0 + jnp.exp(-jnp.abs(z)))


def _rope(x, cos, sin_up, sin_dn):
    return (x * cos + pltpu.roll(x, V7X_LANES - 32, axis=1) * sin_up
            + pltpu.roll(x, 32, axis=1) * sin_dn)


def _proj_kernel(h_ref, gmix_ref, wq_ref, wk_ref, wv_ref, wr_ref, wz_ref, upf_ref, upb_ref,
                 bf_ref, bb_ref, waq_ref, wak_ref, wav_ref, wga_ref, wgb_ref,
                 qg_ref, kg_ref, cos_ref, sup_ref, sdn_ref,
                 gq_o, gk_o, gv_o, sr_o, lf_o, lb_o, aq_o, ak_o, av_o, sga_o, sgb_o):
    u = _rms(h_ref[...], gmix_ref[...]).astype(BF16)

    gq_o[...] = (_dot(u, wq_ref[...]) * (GLA_DK ** -0.5)).astype(BF16)
    gk_o[...] = _dot(u, wk_ref[...]).astype(BF16)
    gv_o[...] = _dot(u, wv_ref[...]).astype(BF16)
    gr = _dot(u, wr_ref[...])
    sr_o[...] = (gr * jax.nn.sigmoid(gr)).astype(BF16)

    za = _dot(u, wz_ref[...]).astype(BF16)
    lf_o[...] = _log_sigmoid(_dot(za, upf_ref[...]) + bf_ref[...]) * (LOG2_E / GLA_TAU)
    lb_o[...] = _log_sigmoid(_dot(za, upb_ref[...]) + bb_ref[...]) * (LOG2_E / GLA_TAU)

    cos, sup, sdn = cos_ref[...], sup_ref[...], sdn_ref[...]
    aq = _dot(u, waq_ref[...])
    qg = qg_ref[...] * (ATT_HEAD_DIM ** -0.5 * LOG2_E)
    for hd in range(ATT_Q_HEADS):
        sl = slice(hd * ATT_HEAD_DIM, (hd + 1) * ATT_HEAD_DIM)
        aq_o[hd] = _rope(_rms(aq[:, sl], qg), cos, sup, sdn).astype(BF16)
    ak = _dot(u, wak_ref[...])
    for hd in range(ATT_KV_HEADS):
        sl = slice(hd * ATT_HEAD_DIM, (hd + 1) * ATT_HEAD_DIM)
        ak_o[:, sl] = _rope(_rms(ak[:, sl], kg_ref[...]), cos, sup, sdn).astype(BF16)
    av_o[...] = _dot(u, wav_ref[...]).astype(BF16)

    sga_o[...] = jax.nn.sigmoid(_dot(u, wga_ref[...])).astype(BF16)
    sgb_o[...] = jax.nn.sigmoid(_dot(u, wgb_ref[...])).astype(BF16)


def _rope_tables(seq):
    pos = jnp.arange(seq)
    row_pos = (pos // GRID_W).astype(F32)
    col_pos = (pos % GRID_W).astype(F32)
    half = ATT_HEAD_DIM // 2
    inv_freq = ROPE_THETA ** (-jnp.arange(0, half, 2, dtype=F32) / half)
    ang_r = row_pos[:, None] * inv_freq
    ang_c = col_pos[:, None] * inv_freq
    zero = jnp.zeros_like(ang_r)
    cos = jnp.concatenate([jnp.cos(ang_r)] * 2 + [jnp.cos(ang_c)] * 2, axis=-1)
    sin_up = jnp.concatenate([-jnp.sin(ang_r), zero, -jnp.sin(ang_c), zero], axis=-1)
    sin_dn = jnp.concatenate([zero, jnp.sin(ang_r), zero, jnp.sin(ang_c)], axis=-1)
    return cos, sin_up, sin_dn


def _proj(h, seq, g_mix, w_parts, up_f, up_b, bias_f, bias_b, q_gain, k_gain):
    t = h.shape[0]
    tm = min(ROW_TILE, seq)
    tiles_per_seq = seq // tm
    cos, sin_up, sin_dn = _rope_tables(seq)

    def row(width):
        return pl.BlockSpec((tm, width), lambda i: (i, 0))

    table = pl.BlockSpec((tm, ATT_HEAD_DIM), lambda i: (i % tiles_per_seq, 0))
    wq, wk, wv, wr, wz, waq, wak, wav, wga, wgb = w_parts
    consts_a = (g_mix, wq, wk, wv, wr, wz, up_f, up_b, bias_f, bias_b, waq, wak, wav, wga, wgb,
                q_gain, k_gain)
    out_widths = (GLA_KEY_W, GLA_KEY_W, GLA_VAL_W, GLA_VAL_W, GLA_KEY_W, GLA_KEY_W,
                  None, ATT_KV_W, ATT_KV_W, D_MODEL, D_MODEL)
    out_dtypes = (BF16, BF16, BF16, BF16, F32, F32, BF16, BF16, BF16, BF16, BF16)
    q_heads = pl.BlockSpec((ATT_Q_HEADS, tm, ATT_HEAD_DIM), lambda i: (0, i, 0))
    return pl.pallas_call(
        _proj_kernel,
        grid=(t // tm,),
        in_specs=[row(D_MODEL)] + [_const_spec(c.shape) for c in consts_a] + [table] * 3,
        out_specs=[q_heads if w is None else row(w) for w in out_widths],
        out_shape=[jax.ShapeDtypeStruct((ATT_Q_HEADS, t, ATT_HEAD_DIM) if w is None else (t, w), d)
                   for w, d in zip(out_widths, out_dtypes)],
        compiler_params=_params(1),
        name="proj",
    )(h, *consts_a, cos, sin_up, sin_dn)


def _gla_constants():
    c, nl = GLA_C, GLA_LEVELS
    i = np.arange(c)[:, None]
    m = np.arange(c)[None, :]
    sum_f = [m <= i, m > i]
    sum_b = [m >= i, m < i]
    role_f, role_b, mask_f, mask_b = [], [], [], []
    for lvl in range(nl):
        s = c >> (lvl + 1)
        start = (i // (2 * s)) * 2 * s
        mid = start + s
        upper = (i % (2 * s)) >= s
        sum_f.append(np.where(upper, (m >= mid) & (m <= i), (m > i) & (m < mid)))
        sum_b.append(np.where(upper, (m >= mid) & (m < i), (m >= i) & (m < mid)))
        same = (i // (2 * s)) == (m // (2 * s))
        m_upper = (m % (2 * s)) >= s
        role_f.append(np.broadcast_to(upper, (c, V7X_LANES)))
        role_b.append(np.broadcast_to(~upper, (c, V7X_LANES)))
        mask_f.append(same & upper & ~m_upper)
        mask_b.append(same & ~upper & m_upper)
    sum_f = np.concatenate(sum_f, axis=0).astype(np.float32)
    sum_b = np.concatenate(sum_b, axis=0).astype(np.float32)
    sum_f = np.concatenate([sum_f, sum_f], axis=1)
    sum_b = np.concatenate([sum_b, sum_b], axis=1)
    roles = np.stack(role_f + role_b).astype(np.float32)
    masks = np.stack(mask_f + mask_b + [i == m]).astype(np.float32)
    wide = np.arange(GLA_WIDE_C)
    tri = (wide[None, :] <= wide[:, None]).astype(np.float32)
    tri = np.concatenate([tri, tri], axis=1)
    return (jnp.asarray(sum_f, BF16), jnp.asarray(sum_b, BF16), jnp.asarray(roles), jnp.asarray(masks),
            jnp.asarray(tri, BF16))


def _gla_kernel(q_ref, k_ref, v_ref, sr_ref, gf_ref, gb_ref, gain_ref,
                sumf_ref, sumb_ref, role_ref, mask_ref, tri_ref,
                o_ref,
                acc_ref, qe_ref, upd_ref, snap_ref, dec_ref):
    c, nl, dk, dv = GLA_C, GLA_LEVELS, GLA_DK, GLA_DV
    n_chunks = q_ref.shape[0] // c

    def decay_sums(g, sum_ref, n_rows):
        g_hi = g.astype(BF16)
        g_lo = (g - g_hi.astype(F32)).astype(BF16)
        return _dot(sum_ref[0:n_rows, :], jnp.concatenate([g_hi, g_lo], axis=0))

    def store_chunk(i, rows, a, v, qe_f, qe_b, kd_f, kd_b, total_f, total_b):
        acc_ref[rows, :] = _dot(a.astype(BF16), v)
        qe_ref[rows, :] = jnp.concatenate([qe_f, qe_b], axis=1).astype(BF16)
        kd = jnp.concatenate([kd_f, kd_b], axis=1).astype(BF16)
        upd_ref[i] = _dot_tn(v, kd)
        total = jnp.concatenate([total_f, total_b], axis=1)
        dec_ref[pl.ds(pl.multiple_of(i * 8, 8), 8), :] = jnp.broadcast_to(total, (8, 2 * dk))

    def chunk_any_decay(i, carry):
        rows = pl.ds(pl.multiple_of(i * c, c), c)
        q, k, v = q_ref[rows, :], k_ref[rows, :], v_ref[rows, :]
        qf, kf = q.astype(F32), k.astype(F32)
        ef = jnp.exp2(decay_sums(gf_ref[rows, :], sumf_ref, (2 + nl) * c))
        eb = jnp.exp2(decay_sums(gb_ref[rows, :], sumb_ref, (2 + nl) * c))
        a = _dot_nt(q, k) * mask_ref[2 * nl]
        for lvl in range(nl):
            blk = slice((2 + lvl) * c, (3 + lvl) * c)
            xf = (jnp.where(role_ref[lvl] > 0.5, qf, kf) * ef[blk]).astype(BF16)
            a = a + _dot_nt(xf, xf) * mask_ref[lvl]
            xb = (jnp.where(role_ref[nl + lvl] > 0.5, qf, kf) * eb[blk]).astype(BF16)
            a = a + _dot_nt(xb, xb) * mask_ref[nl + lvl]
        store_chunk(i, rows, a, v, qf * ef[0:c], qf * eb[0:c], kf * ef[c:2 * c], kf * eb[c:2 * c],
                    ef[c - 1:c], eb[0:1])
        return carry

    w = GLA_WIDE_C
    n_wide = q_ref.shape[0] // w

    def chunk_mild_decay(i, carry):
        rows = pl.ds(pl.multiple_of(i * w, w), w)
        q, k, v = q_ref[rows, :], k_ref[rows, :], v_ref[rows, :]
        qf, kf = q.astype(F32), k.astype(F32)
        g_b = gb_ref[rows, :]
        prefix = decay_sums(jnp.concatenate([gf_ref[rows, :], g_b], axis=1), tri_ref, w)
        bf, pb = prefix[:, 0:dk], prefix[:, dk:2 * dk]
        bb = (pb[w - 1:w] - pb) + g_b
        mid_f, mid_b = bf[w // 2 - 1:w // 2], bb[w // 2:w // 2 + 1]
        rf, rb = bf - mid_f, bb - mid_b
        q_f, k_f = qf * jnp.exp2(rf), kf * jnp.exp2(-rf)
        q_b, k_b = qf * jnp.exp2(rb), kf * jnp.exp2(-rb)
        row = lax.broadcasted_iota(jnp.int32, (w, w), 0)
        col = lax.broadcasted_iota(jnp.int32, (w, w), 1)
        a = jnp.where(row >= col, _dot_nt(q_f.astype(BF16), k_f.astype(BF16)),
                      _dot_nt(q_b.astype(BF16), k_b.astype(BF16)))
        store_chunk(i, rows, a, v, q_f * jnp.exp2(mid_f), q_b * jnp.exp2(mid_b),
                    k_f * jnp.exp2(bf[w - 1:w] - mid_f), k_b * jnp.exp2(bb[0:1] - mid_b),
                    jnp.exp2(bf[w - 1:w]), jnp.exp2(bb[0:1]))
        return carry

    def scan_states(n):
        for slab in range(dv // GLA_SLAB):
            srows = slice(slab * GLA_SLAB, (slab + 1) * GLA_SLAB)

            def scan(step, carry):
                sf, sb = carry
                i, j = step, n - 1 - step
                snap_ref[i, srows, 0:dk] = sf.astype(BF16)
                snap_ref[j, srows, dk:2 * dk] = sb.astype(BF16)
                dec_f = dec_ref[pl.ds(pl.multiple_of(i * 8, 8), 8), 0:dk]
                dec_b = dec_ref[pl.ds(pl.multiple_of(j * 8, 8), 8), dk:2 * dk]
                sf = (sf.reshape(-1, 8, dk) * dec_f[None]).reshape(GLA_SLAB, dk) + upd_ref[i, srows, 0:dk]
                sb = (sb.reshape(-1, 8, dk) * dec_b[None]).reshape(GLA_SLAB, dk) + upd_ref[j, srows, dk:2 * dk]
                return sf, sb

            zero = jnp.zeros((GLA_SLAB, dk), F32)
            lax.fori_loop(0, n, scan, (zero, zero), unroll=2)

    def finish(size):
        def body(i, carry):
            rows = pl.ds(pl.multiple_of(i * size, size), size)
            o = acc_ref[rows, :] + _dot_nt(qe_ref[rows, :], snap_ref[i])
            o_ref[rows, :] = (_rms(o, gain_ref[...]) * sr_ref[rows, :].astype(F32)).astype(BF16)
            return carry
        return body

    def chunk_totals(g_ref):
        return jnp.sum(g_ref[...].reshape(n_wide, w, dk), axis=1)
    mild = jnp.min(jnp.minimum(chunk_totals(gf_ref), chunk_totals(gb_ref))) >= -GLA_MILD_LOG2

    @pl.when(mild)
    def _():
        lax.fori_loop(0, n_wide, chunk_mild_decay, 0)
        scan_states(n_wide)
        lax.fori_loop(0, n_wide, finish(w), 0)

    @pl.when(jnp.logical_not(mild))
    def _():
        lax.fori_loop(0, n_chunks, chunk_any_decay, 0)
        scan_states(n_chunks)
        lax.fori_loop(0, n_chunks, finish(c), 0)


def _gla(gq, gk, gv, sr, lgf, lgb, gain, batch, seq):
    consts = _gla_constants()
    n_chunks = seq // GLA_C

    def head(width):
        return pl.BlockSpec((None, seq, width), lambda b, h: (b, 0, h))

    return pl.pallas_call(
        _gla_kernel,
        grid=(batch, GLA_HEADS),
        in_specs=[head(GLA_DK), head(GLA_DK), head(GLA_DV), head(GLA_DV), head(GLA_DK), head(GLA_DK),
                  pl.BlockSpec((None, 1, GLA_DV), lambda b, h: (h, 0, 0))]
                 + [_const_spec(cst.shape) for cst in consts],
        out_specs=head(GLA_DV),
        out_shape=jax.ShapeDtypeStruct((batch, seq, GLA_VAL_W), BF16),
        scratch_shapes=[pltpu.VMEM((seq, GLA_DV), F32),
                        pltpu.VMEM((seq, 2 * GLA_DK), BF16),
                        pltpu.VMEM((n_chunks, GLA_DV, 2 * GLA_DK), F32),
                        pltpu.VMEM((n_chunks, GLA_DV, 2 * GLA_DK), BF16),
                        pltpu.VMEM((n_chunks * 8, 2 * GLA_DK), F32)],
        compiler_params=_params(2),
        name="gla",
    )(gq, gk, gv, sr, lgf, lgb, gain, *consts)


def _attn_kernel(bound_ref, q_ref, k_ref, v_ref, o_ref, vext_ref):
    hd = ATT_HEAD_DIM
    tq = q_ref.shape[1]

    @pl.when(pl.program_id(2) == 0)
    def _():
        vext_ref[:, :hd] = v_ref[...]
        vext_ref[:, hd:] = jnp.ones((v_ref.shape[0], hd), BF16)

    def finish(p):
        o_ext = _dot(p, vext_ref[...])
        o = (o_ext[:, :hd] / o_ext[:, hd:]).astype(BF16)
        for g in range(ATT_GROUP):
            o_ref[:, g * hd:(g + 1) * hd] = o[g * tq:(g + 1) * tq]

    def scores():
        q = q_ref[...].reshape(ATT_GROUP * tq, hd)
        return _dot_nt(q, k_ref[...])

    bound = bound_ref[0]
    small = bound <= ATT_SHIFT_LIMIT_LOG2

    @pl.when(small)
    def _():
        finish(jnp.exp2(scores() - bound).astype(BF16))

    @pl.when(jnp.logical_not(small))
    def _():
        s = scores()
        finish(jnp.exp2(s - jnp.max(s, axis=-1, keepdims=True)).astype(BF16))


def _attn(bound, aq, ak, av, batch, seq):
    tq = min(ATT_Q_TILE, seq)
    nq = seq // tq
    qspec = pl.BlockSpec((ATT_GROUP, tq, ATT_HEAD_DIM), lambda b, h, i: (h, b * nq + i, 0))
    kvspec = pl.BlockSpec((None, seq, ATT_HEAD_DIM), lambda b, h, i: (b, 0, h))
    ospec = pl.BlockSpec((None, tq, ATT_GROUP * ATT_HEAD_DIM), lambda b, h, i: (b, i, h))
    return pl.pallas_call(
        _attn_kernel,
        grid=(batch, ATT_KV_HEADS, nq),
        in_specs=[pl.BlockSpec(memory_space=pltpu.SMEM), qspec, kvspec, kvspec],
        out_specs=ospec,
        out_shape=jax.ShapeDtypeStruct((batch, seq, ATT_Q_W), BF16),
        scratch_shapes=[pltpu.VMEM((seq, 2 * ATT_HEAD_DIM), BF16)],
        compiler_params=_params(3),
        name="attn",
    )(bound, aq, ak, av)


def _score_bound(q_gain, k_gain):
    return (jnp.max(jnp.abs(q_gain)) * jnp.max(jnp.abs(k_gain))
            * (1.02 * ATT_HEAD_DIM ** 0.5 * LOG2_E)).reshape(1).astype(F32)


def _merge_kernel(h_ref, oa_ref, ob_ref, sga_ref, sgb_ref, wa_ref, wb_ref, wo_ref, gpost_ref, o_ref):
    ya = _dot(oa_ref[...], wa_ref[...])
    yb = _dot(ob_ref[...], wb_ref[...])
    merged = sga_ref[...].astype(F32) * ya + sgb_ref[...].astype(F32) * yb
    m = _dot(merged.astype(BF16), wo_ref[...])
    o_ref[...] = h_ref[...] + _rms(m, gpost_ref[...])


def _merge(h, oa, ob, sga, sgb, w_a, w_b, w_o, g_post):
    t = h.shape[0]
    tm = min(ROW_TILE, t)
    row = pl.BlockSpec((tm, D_MODEL), lambda i: (i, 0))
    sq = _const_spec((D_MODEL, D_MODEL))
    return pl.pallas_call(
        _merge_kernel,
        grid=(t // tm,),
        in_specs=[row] * 5 + [sq, sq, sq, _const_spec((1, D_MODEL))],
        out_specs=row,
        out_shape=jax.ShapeDtypeStruct((t, D_MODEL), F32),
        compiler_params=_params(1),
        name="merge",
    )(h, oa, ob, sga, sgb, w_a, w_b, w_o, g_post)


def _split_w_in(w_in):
    gq, gk, gv, gr, za_f, za_b, aq, ak, av, ga, gb = jnp.split(w_in.astype(BF16), IN_OFFSETS, axis=-1)
    pad = jnp.zeros((D_MODEL, V7X_LANES - 2 * GLA_RANK), BF16)
    return gq, gk, gv, gr, jnp.concatenate([za_f, za_b, pad], axis=-1), aq, ak, av, ga, gb


def _pad_decay_up(up, first_row):
    out = jnp.zeros((V7X_LANES, GLA_KEY_W), BF16)
    return lax.dynamic_update_slice(out, up.astype(BF16), (first_row, 0))


def kernel(x, ffn1_pre_g, ffn1_w_in, ffn1_w_out, ffn1_post_g, mix_pre_g, w_in, gla_decay_up_f,
           gla_decay_bias_f, gla_decay_up_b, gla_decay_bias_b, gla_out_g, w_branch_a, att_q_norm_g,
           att_k_norm_g, w_branch_b, w_out, mix_post_g, ffn2_pre_g, ffn2_w_in, ffn2_w_out, ffn2_post_g):
    batch, seq, d = x.shape
    assert d == D_MODEL and seq % max(GLA_WIDE_C, ATT_Q_TILE, GRID_W) == 0
    assert seq % ROW_TILE == 0 or seq < ROW_TILE
    depth = w_in.shape[0]
    t = batch * seq
    h = x.reshape(t, d)
    vec = lambda g: g.reshape(1, -1).astype(F32)
    for l in range(depth):
        h = _ffn(h, vec(ffn1_pre_g[l]), ffn1_w_in[l].astype(BF16), ffn1_w_out[l].astype(BF16),
                 vec(ffn1_post_g[l]))
        gq, gk, gv, sr, lgf, lgb, aq, ak, av, sga, sgb = _proj(
            h, seq, vec(mix_pre_g[l]), _split_w_in(w_in[l]),
            _pad_decay_up(gla_decay_up_f[l], 0), _pad_decay_up(gla_decay_up_b[l], GLA_RANK),
            vec(gla_decay_bias_f[l]), vec(gla_decay_bias_b[l]),
            vec(att_q_norm_g[l]), vec(att_k_norm_g[l]))
        per_seq = lambda a: a.reshape(batch, seq, a.shape[-1])
        oa = _gla(per_seq(gq), per_seq(gk), per_seq(gv), per_seq(sr), per_seq(lgf), per_seq(lgb),
                  gla_out_g[l].reshape(GLA_HEADS, 1, GLA_DV).astype(F32), batch, seq)
        ob = _attn(_score_bound(att_q_norm_g[l], att_k_norm_g[l]), aq, per_seq(ak), per_seq(av), batch, seq)
        h = _merge(h, oa.reshape(t, GLA_VAL_W), ob.reshape(t, ATT_Q_W), sga, sgb,
                   w_branch_a[l].astype(BF16), w_branch_b[l].astype(BF16), w_out[l].astype(BF16),
                   vec(mix_post_g[l]))
        h = _ffn(h, vec(ffn2_pre_g[l]), ffn2_w_in[l].astype(BF16), ffn2_w_out[l].astype(BF16),
                 vec(ffn2_post_g[l]))
    return h.reshape(batch, seq, d)
```

```python
import functools

import numpy as np
import jax
import jax.numpy as jnp
from jax import lax
from jax.experimental import pallas as pl
from jax.experimental.pallas import tpu as pltpu

F32 = jnp.float32
BF16 = jnp.bfloat16

D_MODEL = 1024
GRID_W = 64
D_FF = 2816
EPS = 1e-6
GLA_HEADS = 4
GLA_DK = 128
GLA_DV = 256
GLA_RANK = 16
GLA_TAU = 16.0
ATT_Q_HEADS = 8
ATT_KV_HEADS = 2
ATT_HEAD_DIM = 128
ATT_GROUP = ATT_Q_HEADS // ATT_KV_HEADS
ROPE_THETA = 10000.0
GLA_KEY_W = GLA_HEADS * GLA_DK
GLA_VAL_W = GLA_HEADS * GLA_DV
ATT_Q_W = ATT_Q_HEADS * ATT_HEAD_DIM
ATT_KV_W = ATT_KV_HEADS * ATT_HEAD_DIM
IN_SPLITS = (GLA_KEY_W, GLA_KEY_W, GLA_VAL_W, GLA_VAL_W, GLA_RANK, GLA_RANK,
             ATT_Q_W, ATT_KV_W, ATT_KV_W, D_MODEL, D_MODEL)
IN_OFFSETS = tuple(int(s) for s in np.cumsum(IN_SPLITS)[:-1])
LOG2_E = 1.4426950408889634

V7X_LANES = 128
V7X_VMEM_BYTES = 64 * 1024 * 1024
VMEM_LIMIT = V7X_VMEM_BYTES - 8 * 1024 * 1024

ROW_TILE = 512
ROW_PARTS = 2
ATT_Q_TILE = 256
ATT_SHIFT_LIMIT_LOG2 = 60.0
GLA_C = 64
GLA_LEVELS = GLA_C.bit_length() - 1
GLA_GROUP = 4
GLA_SLAB = 64
GLA_WIDE_C = 256
GLA_MILD_LOG2 = 48.0
FF_CHUNKS = ((0, 1024), (1024, 1024), (2048, 768))


def _const_spec(shape):
    zeros = (0,) * len(shape)
    return pl.BlockSpec(shape, lambda *_: zeros, pipeline_mode=pl.Buffered(1))


def _params(n_axes):
    return pltpu.CompilerParams(dimension_semantics=("arbitrary",) * n_axes,
                                vmem_limit_bytes=VMEM_LIMIT)


def _rms(x, g):
    ms = jnp.mean(x * x, axis=-1, keepdims=True)
    return x * lax.rsqrt(ms + EPS) * g


def _dot(a, b):
    return jnp.dot(a, b, preferred_element_type=F32)


def _dot_nt(a, b):
    return lax.dot_general(a, b, (((1,), (1,)), ((), ())), preferred_element_type=F32)


def _dot_tn(a, b):
    return lax.dot_general(a, b, (((0,), (0,)), ((), ())), preferred_element_type=F32)


def _row_parts(ref):
    rows = ref.shape[0] // ROW_PARTS
    return [slice(p * rows, (p + 1) * rows) for p in range(ROW_PARTS)]


def _ffn_kernel(x_ref, gpre_ref, win_ref, wout_ref, gpost_ref, o_ref):
    for rows in _row_parts(x_ref):
        x = x_ref[rows, :]
        xn = _rms(x, gpre_ref[...]).astype(BF16)
        acc = None
        for c0, cw in FF_CHUNKS:
            gate = _dot(xn, win_ref[:, c0:c0 + cw])
            up = _dot(xn, win_ref[:, D_FF + c0:D_FF + c0 + cw])
            act = (gate * jax.nn.sigmoid(gate) * up).astype(BF16)
            part = _dot(act, wout_ref[c0:c0 + cw, :])
            acc = part if acc is None else acc + part
        o_ref[rows, :] = x + 0.5 * _rms(acc, gpost_ref[...])


def _ffn(x, g_pre, w_in, w_out, g_post):
    t = x.shape[0]
    tm = min(ROW_TILE * ROW_PARTS, t)
    row = pl.BlockSpec((tm, D_MODEL), lambda i: (i, 0))
    return pl.pallas_call(
        _ffn_kernel,
        grid=(t // tm,),
        in_specs=[row, _const_spec((1, D_MODEL)), _const_spec((D_MODEL, 2 * D_FF)),
                  _const_spec((D_FF, D_MODEL)), _const_spec((1, D_MODEL))],
        out_specs=row,
        out_shape=jax.ShapeDtypeStruct((t, D_MODEL), F32),
        compiler_params=_params(1),
        name="ffn",
    )(x, g_pre, w_in, w_out, g_post)


def _log_sigmoid(z):
    return jnp.minimum(z, 0.0) - jnp.log(1.0 + jnp.exp(-jnp.abs(z)))


def _rope(x, cos, sin_up, sin_dn):
    return (x * cos + pltpu.roll(x, V7X_LANES - 32, axis=1) * sin_up
            + pltpu.roll(x, 32, axis=1) * sin_dn)


def _proj_kernel(h_ref, gmix_ref, wq_ref, wk_ref, wv_ref, wr_ref, wz_ref, upf_ref, upb_ref,
                 bf_ref, bb_ref, waq_ref, wak_ref, wav_ref, wga_ref, wgb_ref,
                 qg_ref, kg_ref, cos_ref, sup_ref, sdn_ref,
                 gq_o, gk_o, gv_o, sr_o, lf_o, lb_o, aq_o, ak_o, av_o, sga_o, sgb_o):
    u = _rms(h_ref[...], gmix_ref[...]).astype(BF16)
    za = _dot(u, wz_ref[...]).astype(BF16)

    cos, sup, sdn = cos_ref[...], sup_ref[...], sdn_ref[...]
    aq = _dot(u, waq_ref[...])
    qg = qg_ref[...] * (ATT_HEAD_DIM ** -0.5 * LOG2_E)
    for hd in range(ATT_Q_HEADS):
        sl = slice(hd * ATT_HEAD_DIM, (hd + 1) * ATT_HEAD_DIM)
        aq_o[hd] = _rope(_rms(aq[:, sl], qg), cos, sup, sdn).astype(BF16)
    ak = _dot(u, wak_ref[...])
    for hd in range(ATT_KV_HEADS):
        sl = slice(hd * ATT_HEAD_DIM, (hd + 1) * ATT_HEAD_DIM)
        ak_o[:, sl] = _rope(_rms(ak[:, sl], kg_ref[...]), cos, sup, sdn).astype(BF16)

    sga_o[...] = jax.nn.sigmoid(_dot(u, wga_ref[...])).astype(BF16)
    sgb_o[...] = jax.nn.sigmoid(_dot(u, wgb_ref[...])).astype(BF16)
    gr = _dot(u, wr_ref[...])
    sr_o[...] = (gr * jax.nn.sigmoid(gr)).astype(BF16)
    lf_o[...] = _log_sigmoid(_dot(za, upf_ref[...]) + bf_ref[...]) * (LOG2_E / GLA_TAU)
    lb_o[...] = _log_sigmoid(_dot(za, upb_ref[...]) + bb_ref[...]) * (LOG2_E / GLA_TAU)

    gq_o[...] = (_dot(u, wq_ref[...]) * (GLA_DK ** -0.5)).astype(BF16)
    gk_o[...] = _dot(u, wk_ref[...]).astype(BF16)
    gv_o[...] = _dot(u, wv_ref[...]).astype(BF16)
    av_o[...] = _dot(u, wav_ref[...]).astype(BF16)


def _rope_tables(seq):
    pos = np.arange(seq)
    half = ATT_HEAD_DIM // 2
    inv_freq = ROPE_THETA ** (-np.arange(0, half, 2, dtype=np.float64) / half)
    ang_r = (pos // GRID_W)[:, None] * inv_freq
    ang_c = (pos % GRID_W)[:, None] * inv_freq
    zero = np.zeros_like(ang_r)
    cos = np.concatenate([np.cos(ang_r)] * 2 + [np.cos(ang_c)] * 2, axis=-1)
    sin_up = np.concatenate([-np.sin(ang_r), zero, -np.sin(ang_c), zero], axis=-1)
    sin_dn = np.concatenate([zero, np.sin(ang_r), zero, np.sin(ang_c)], axis=-1)
    return tuple(jnp.asarray(t, F32) for t in (cos, sin_up, sin_dn))


def _proj(h, seq, g_mix, w_parts, up_f, up_b, bias_f, bias_b, q_gain, k_gain):
    t = h.shape[0]
    tm = min(ROW_TILE, seq)
    tiles_per_seq = seq // tm
    cos, sin_up, sin_dn = _rope_tables(seq)

    def row(width):
        return pl.BlockSpec((tm, width), lambda i: (i, 0))

    table = pl.BlockSpec((tm, ATT_HEAD_DIM), lambda i: (i % tiles_per_seq, 0))
    wq, wk, wv, wr, wz, waq, wak, wav, wga, wgb = w_parts
    consts_a = (g_mix, wq, wk, wv, wr, wz, up_f, up_b, bias_f, bias_b, waq, wak, wav, wga, wgb,
                q_gain, k_gain)
    out_widths = (GLA_KEY_W, GLA_KEY_W, GLA_VAL_W, GLA_VAL_W, GLA_KEY_W, GLA_KEY_W,
                  None, ATT_KV_W, ATT_KV_W, D_MODEL, D_MODEL)
    out_dtypes = (BF16, BF16, BF16, BF16, F32, F32, BF16, BF16, BF16, BF16, BF16)
    q_heads = pl.BlockSpec((ATT_Q_HEADS, tm, ATT_HEAD_DIM), lambda i: (0, i, 0))
    return pl.pallas_call(
        _proj_kernel,
        grid=(t // tm,),
        in_specs=[row(D_MODEL)] + [_const_spec(c.shape) for c in consts_a] + [table] * 3,
        out_specs=[q_heads if w is None else row(w) for w in out_widths],
        out_shape=[jax.ShapeDtypeStruct((ATT_Q_HEADS, t, ATT_HEAD_DIM) if w is None else (t, w), d)
                   for w, d in zip(out_widths, out_dtypes)],
        compiler_params=_params(1),
        name="proj",
    )(h, *consts_a, cos, sin_up, sin_dn)


def _gla_constants():
    c, nl = GLA_C, GLA_LEVELS
    i = np.arange(c)[:, None]
    m = np.arange(c)[None, :]
    sum_f = [m <= i, m > i]
    sum_b = [m >= i, m < i]
    role_f, role_b, mask_f, mask_b = [], [], [], []
    for lvl in range(nl):
        s = c >> (lvl + 1)
        start = (i // (2 * s)) * 2 * s
        mid = start + s
        upper = (i % (2 * s)) >= s
        sum_f.append(np.where(upper, (m >= mid) & (m <= i), (m > i) & (m < mid)))
        sum_b.append(np.where(upper, (m >= mid) & (m < i), (m >= i) & (m < mid)))
        same = (i // (2 * s)) == (m // (2 * s))
        m_upper = (m % (2 * s)) >= s
        role_f.append(np.broadcast_to(upper, (c, V7X_LANES)))
        role_b.append(np.broadcast_to(~upper, (c, V7X_LANES)))
        mask_f.append(same & upper & ~m_upper)
        mask_b.append(same & ~upper & m_upper)
    sum_f = np.concatenate(sum_f, axis=0).astype(np.float32)
    sum_b = np.concatenate(sum_b, axis=0).astype(np.float32)
    sum_f = np.concatenate([sum_f, sum_f], axis=1)
    sum_b = np.concatenate([sum_b, sum_b], axis=1)
    roles = np.stack(role_f + role_b).astype(np.float32)
    masks = np.stack(mask_f + mask_b + [i == m]).astype(np.float32)
    wide = np.arange(GLA_WIDE_C)
    tri = (wide[None, :] <= wide[:, None]).astype(np.float32)
    tri = np.concatenate([tri, tri], axis=1)
    return (jnp.asarray(sum_f, BF16), jnp.asarray(sum_b, BF16), jnp.asarray(roles), jnp.asarray(masks),
            jnp.asarray(tri, BF16))


def _gla_kernel(q_ref, k_ref, v_ref, sr_ref, gf_ref, gb_ref, gain_ref,
                sumf_ref, sumb_ref, role_ref, mask_ref, tri_ref,
                o_ref,
                acc_ref, qe_ref, upd_ref, snap_ref, dec_ref):
    c, nl, dk, dv = GLA_C, GLA_LEVELS, GLA_DK, GLA_DV
    n_chunks = q_ref.shape[0] // c

    def decay_sums(g, sum_ref, n_rows):
        g_hi = g.astype(BF16)
        g_lo = (g - g_hi.astype(F32)).astype(BF16)
        return _dot(sum_ref[0:n_rows, :], jnp.concatenate([g_hi, g_lo], axis=0))

    def store_intra(rows, a, v):
        acc_ref[rows, :] = _dot(a.astype(BF16), v)

    def store_edges(i, rows, v, qe_f, qe_b, kd_f, kd_b, total_f, total_b):
        qe_ref[rows, :] = jnp.concatenate([qe_f, qe_b], axis=1).astype(BF16)
        kd = jnp.concatenate([kd_f, kd_b], axis=1).astype(BF16)
        upd_ref[i] = _dot_tn(v, kd)
        total = jnp.concatenate([total_f, total_b], axis=1)
        dec_ref[pl.ds(pl.multiple_of(i * 8, 8), 8), :] = jnp.broadcast_to(total, (8, 2 * dk))

    def chunk_any_decay(i, carry):
        rows = pl.ds(pl.multiple_of(i * c, c), c)
        q, k, v = q_ref[rows, :], k_ref[rows, :], v_ref[rows, :]
        qf, kf = q.astype(F32), k.astype(F32)
        ef = jnp.exp2(decay_sums(gf_ref[rows, :], sumf_ref, (2 + nl) * c))
        eb = jnp.exp2(decay_sums(gb_ref[rows, :], sumb_ref, (2 + nl) * c))
        a = _dot_nt(q, k) * mask_ref[2 * nl]
        for lvl in range(nl):
            blk = slice((2 + lvl) * c, (3 + lvl) * c)
            xf = (jnp.where(role_ref[lvl] > 0.5, qf, kf) * ef[blk]).astype(BF16)
            a = a + _dot_nt(xf, xf) * mask_ref[lvl]
            xb = (jnp.where(role_ref[nl + lvl] > 0.5, qf, kf) * eb[blk]).astype(BF16)
            a = a + _dot_nt(xb, xb) * mask_ref[nl + lvl]
        store_intra(rows, a, v)
        store_edges(i, rows, v, qf * ef[0:c], qf * eb[0:c], kf * ef[c:2 * c], kf * eb[c:2 * c],
                    ef[c - 1:c], eb[0:1])
        return carry

    w = GLA_WIDE_C
    n_wide = q_ref.shape[0] // w

    def chunks_mild_decay(step, carry):
        ids = [step * GLA_GROUP + u for u in range(GLA_GROUP)]
        rows = [pl.ds(pl.multiple_of(i * w, w), w) for i in ids]
        g_b = [gb_ref[r, :] for r in rows]
        prefix = [decay_sums(jnp.concatenate([gf_ref[r, :], g], axis=1), tri_ref, w)
                  for r, g in zip(rows, g_b)]
        lower = (lax.broadcasted_iota(jnp.int32, (w, w), 0) >= lax.broadcasted_iota(jnp.int32, (w, w), 1))
        scores, rest = [], []
        for r, g, p in zip(rows, g_b, prefix):
            qf, kf = q_ref[r, :].astype(F32), k_ref[r, :].astype(F32)
            bf, pb = p[:, 0:dk], p[:, dk:2 * dk]
            bb = (pb[w - 1:w] - pb) + g
            mid_f, mid_b = bf[w // 2 - 1:w // 2], bb[w // 2:w // 2 + 1]
            rf, rb = bf - mid_f, bb - mid_b
            q_f, k_f = qf * jnp.exp2(rf), kf * jnp.exp2(-rf)
            q_b, k_b = qf * jnp.exp2(rb), kf * jnp.exp2(-rb)
            scores.append((_dot_nt(q_f.astype(BF16), k_f.astype(BF16)),
                           _dot_nt(q_b.astype(BF16), k_b.astype(BF16))))
            rest.append((q_f * jnp.exp2(mid_f), q_b * jnp.exp2(mid_b),
                         k_f * jnp.exp2(bf[w - 1:w] - mid_f), k_b * jnp.exp2(bb[0:1] - mid_b),
                         jnp.exp2(bf[w - 1:w]), jnp.exp2(bb[0:1])))
        for r, (s_f, s_b) in zip(rows, scores):
            store_intra(r, jnp.where(lower, s_f, s_b), v_ref[r, :])
        for i, r, parts in zip(ids, rows, rest):
            store_edges(i, r, v_ref[r, :], *parts)
        return carry

    def scan_states(n):
        for slab in range(dv // GLA_SLAB):
            srows = slice(slab * GLA_SLAB, (slab + 1) * GLA_SLAB)

            def scan(step, carry):
                sf, sb = carry
                i, j = step, n - 1 - step
                snap_ref[i, srows, 0:dk] = sf.astype(BF16)
                snap_ref[j, srows, dk:2 * dk] = sb.astype(BF16)
                dec_f = dec_ref[pl.ds(pl.multiple_of(i * 8, 8), 8), 0:dk]
                dec_b = dec_ref[pl.ds(pl.multiple_of(j * 8, 8), 8), dk:2 * dk]
                sf = (sf.reshape(-1, 8, dk) * dec_f[None]).reshape(GLA_SLAB, dk) + upd_ref[i, srows, 0:dk]
                sb = (sb.reshape(-1, 8, dk) * dec_b[None]).reshape(GLA_SLAB, dk) + upd_ref[j, srows, dk:2 * dk]
                return sf, sb

            zero = jnp.zeros((GLA_SLAB, dk), F32)
            lax.fori_loop(0, n, scan, (zero, zero), unroll=2)

    def finish(size, group):
        def body(step, carry):
            ids = [step * group + u for u in range(group)]
            rows = [pl.ds(pl.multiple_of(i * size, size), size) for i in ids]
            inter = [_dot_nt(qe_ref[r, :], snap_ref[i]) for i, r in zip(ids, rows)]
            for r, o_inter in zip(rows, inter):
                o = acc_ref[r, :] + o_inter
                o_ref[r, :] = (_rms(o, gain_ref[...]) * sr_ref[r, :].astype(F32)).astype(BF16)
            return carry
        return body

    def chunk_totals(g_ref):
        return jnp.sum(g_ref[...].reshape(n_wide, w, dk), axis=1)
    mild = jnp.min(jnp.minimum(chunk_totals(gf_ref), chunk_totals(gb_ref))) >= -GLA_MILD_LOG2

    @pl.when(mild)
    def _():
        lax.fori_loop(0, n_wide // GLA_GROUP, chunks_mild_decay, 0)
        scan_states(n_wide)
        lax.fori_loop(0, n_wide // GLA_GROUP, finish(w, GLA_GROUP), 0)

    @pl.when(jnp.logical_not(mild))
    def _():
        lax.fori_loop(0, n_chunks, chunk_any_decay, 0)
        scan_states(n_chunks)
        lax.fori_loop(0, n_chunks, finish(c, 1), 0)


def _gla(gq, gk, gv, sr, lgf, lgb, gain, batch, seq):
    consts = _gla_constants()
    n_chunks = seq // GLA_C

    def head(width):
        return pl.BlockSpec((None, seq, width), lambda b, h: (b, 0, h))

    return pl.pallas_call(
        _gla_kernel,
        grid=(batch, GLA_HEADS),
        in_specs=[head(GLA_DK), head(GLA_DK), head(GLA_DV), head(GLA_DV), head(GLA_DK), head(GLA_DK),
                  pl.BlockSpec((None, 1, GLA_DV), lambda b, h: (h, 0, 0))]
                 + [_const_spec(cst.shape) for cst in consts],
        out_specs=head(GLA_DV),
        out_shape=jax.ShapeDtypeStruct((batch, seq, GLA_VAL_W), BF16),
        scratch_shapes=[pltpu.VMEM((seq, GLA_DV), F32),
                        pltpu.VMEM((seq, 2 * GLA_DK), BF16),
                        pltpu.VMEM((n_chunks, GLA_DV, 2 * GLA_DK), F32),
                        pltpu.VMEM((n_chunks, GLA_DV, 2 * GLA_DK), BF16),
                        pltpu.VMEM((n_chunks * 8, 2 * GLA_DK), F32)],
        compiler_params=_params(2),
        name="gla",
    )(gq, gk, gv, sr, lgf, lgb, gain, *consts)


def _attn_kernel(bound_ref, q_ref, k_ref, v_ref, o_ref, vext_ref):
    hd = ATT_HEAD_DIM
    tq = q_ref.shape[1]

    @pl.when(pl.program_id(2) == 0)
    def _():
        vext_ref[:, :hd] = v_ref[...]
        vext_ref[:, hd:] = jnp.ones((v_ref.shape[0], hd), BF16)

    def finish(p):
        o_ext = _dot(p, vext_ref[...])
        o = (o_ext[:, :hd] / o_ext[:, hd:]).astype(BF16)
        for g in range(ATT_GROUP):
            o_ref[:, g * hd:(g + 1) * hd] = o[g * tq:(g + 1) * tq]

    def scores():
        q = q_ref[...].reshape(ATT_GROUP * tq, hd)
        return _dot_nt(q, k_ref[...])

    bound = bound_ref[0]
    small = bound <= ATT_SHIFT_LIMIT_LOG2

    @pl.when(small)
    def _():
        finish(jnp.exp2(scores() - bound).astype(BF16))

    @pl.when(jnp.logical_not(small))
    def _():
        s = scores()
        finish(jnp.exp2(s - jnp.max(s, axis=-1, keepdims=True)).astype(BF16))


def _attn(bound, aq, ak, av, batch, seq):
    tq = min(ATT_Q_TILE, seq)
    nq = seq // tq
    qspec = pl.BlockSpec((ATT_GROUP, tq, ATT_HEAD_DIM), lambda b, h, i: (h, b * nq + i, 0))
    kvspec = pl.BlockSpec((None, seq, ATT_HEAD_DIM), lambda b, h, i: (b, 0, h))
    ospec = pl.BlockSpec((None, tq, ATT_GROUP * ATT_HEAD_DIM), lambda b, h, i: (b, i, h))
    return pl.pallas_call(
        _attn_kernel,
        grid=(batch, ATT_KV_HEADS, nq),
        in_specs=[pl.BlockSpec(memory_space=pltpu.SMEM), qspec, kvspec, kvspec],
        out_specs=ospec,
        out_shape=jax.ShapeDtypeStruct((batch, seq, ATT_Q_W), BF16),
        scratch_shapes=[pltpu.VMEM((seq, 2 * ATT_HEAD_DIM), BF16)],
        compiler_params=_params(3),
        name="attn",
    )(bound, aq, ak, av)


def _score_bound(q_gain, k_gain):
    return (jnp.max(jnp.abs(q_gain)) * jnp.max(jnp.abs(k_gain))
            * (1.02 * ATT_HEAD_DIM ** 0.5 * LOG2_E)).reshape(1).astype(F32)


def _merge_kernel(h_ref, oa_ref, ob_ref, sga_ref, sgb_ref, wa_ref, wb_ref, wo_ref, gpost_ref, o_ref):
    for rows in _row_parts(h_ref):
        ya = _dot(oa_ref[rows, :], wa_ref[...])
        yb = _dot(ob_ref[rows, :], wb_ref[...])
        merged = sga_ref[rows, :].astype(F32) * ya + sgb_ref[rows, :].astype(F32) * yb
        m = _dot(merged.astype(BF16), wo_ref[...])
        o_ref[rows, :] = h_ref[rows, :] + _rms(m, gpost_ref[...])


def _merge(h, oa, ob, sga, sgb, w_a, w_b, w_o, g_post):
    t = h.shape[0]
    tm = min(ROW_TILE * ROW_PARTS, t)
    row = pl.BlockSpec((tm, D_MODEL), lambda i: (i, 0))
    sq = _const_spec((D_MODEL, D_MODEL))
    return pl.pallas_call(
        _merge_kernel,
        grid=(t // tm,),
        in_specs=[row] * 5 + [sq, sq, sq, _const_spec((1, D_MODEL))],
        out_specs=row,
        out_shape=jax.ShapeDtypeStruct((t, D_MODEL), F32),
        compiler_params=_params(1),
        name="merge",
    )(h, oa, ob, sga, sgb, w_a, w_b, w_o, g_post)


def _split_w_in(w_in):
    gq, gk, gv, gr, za_f, za_b, aq, ak, av, ga, gb = jnp.split(w_in.astype(BF16), IN_OFFSETS, axis=-1)
    pad = jnp.zeros((D_MODEL, V7X_LANES - 2 * GLA_RANK), BF16)
    return gq, gk, gv, gr, jnp.concatenate([za_f, za_b, pad], axis=-1), aq, ak, av, ga, gb


def _pad_decay_up(up, first_row):
    out = jnp.zeros((V7X_LANES, GLA_KEY_W), BF16)
    return lax.dynamic_update_slice(out, up.astype(BF16), (first_row, 0))


def kernel(x, ffn1_pre_g, ffn1_w_in, ffn1_w_out, ffn1_post_g, mix_pre_g, w_in, gla_decay_up_f,
           gla_decay_bias_f, gla_decay_up_b, gla_decay_bias_b, gla_out_g, w_branch_a, att_q_norm_g,
           att_k_norm_g, w_branch_b, w_out, mix_post_g, ffn2_pre_g, ffn2_w_in, ffn2_w_out, ffn2_post_g):
    batch, seq, d = x.shape
    assert d == D_MODEL and seq % max(GLA_WIDE_C * GLA_GROUP, ATT_Q_TILE, GRID_W) == 0
    assert seq % ROW_TILE == 0 or seq < ROW_TILE
    depth = w_in.shape[0]
    t = batch * seq
    h = x.reshape(t, d)
    vec = lambda g: g.reshape(1, -1).astype(F32)
    for l in range(depth):
        h = _ffn(h, vec(ffn1_pre_g[l]), ffn1_w_in[l].astype(BF16), ffn1_w_out[l].astype(BF16),
                 vec(ffn1_post_g[l]))
        gq, gk, gv, sr, lgf, lgb, aq, ak, av, sga, sgb = _proj(
            h, seq, vec(mix_pre_g[l]), _split_w_in(w_in[l]),
            _pad_decay_up(gla_decay_up_f[l], 0), _pad_decay_up(gla_decay_up_b[l], GLA_RANK),
            vec(gla_decay_bias_f[l]), vec(gla_decay_bias_b[l]),
            vec(att_q_norm_g[l]), vec(att_k_norm_g[l]))
        per_seq = lambda a: a.reshape(batch, seq, a.shape[-1])
        oa = _gla(per_seq(gq), per_seq(gk), per_seq(gv), per_seq(sr), per_seq(lgf), per_seq(lgb),
                  gla_out_g[l].reshape(GLA_HEADS, 1, GLA_DV).astype(F32), batch, seq)
        ob = _attn(_score_bound(att_q_norm_g[l], att_k_norm_g[l]), aq, per_seq(ak), per_seq(av), batch, seq)
        h = _merge(h, oa.reshape(t, GLA_VAL_W), ob.reshape(t, ATT_Q_W), sga, sgb,
                   w_branch_a[l].astype(BF16), w_branch_b[l].astype(BF16), w_out[l].astype(BF16),
                   vec(mix_post_g[l]))
        h = _ffn(h, vec(ffn2_pre_g[l]), ffn2_w_in[l].astype(BF16), ffn2_w_out[l].astype(BF16),
                 vec(ffn2_post_g[l]))
    return h.reshape(batch, seq, d)
```

```python
import functools

import numpy as np
import jax
import jax.numpy as jnp
from jax import lax
from jax.experimental import pallas as pl
from jax.experimental.pallas import tpu as pltpu

F32 = jnp.float32
BF16 = jnp.bfloat16

D_MODEL = 1024
GRID_W = 64
D_FF = 2816
EPS = 1e-6
GLA_HEADS = 4
GLA_DK = 128
GLA_DV = 256
GLA_RANK = 16
GLA_TAU = 16.0
ATT_Q_HEADS = 8
ATT_KV_HEADS = 2
ATT_HEAD_DIM = 128
ATT_GROUP = ATT_Q_HEADS // ATT_KV_HEADS
ROPE_THETA = 10000.0
GLA_KEY_W = GLA_HEADS * GLA_DK
GLA_VAL_W = GLA_HEADS * GLA_DV
ATT_Q_W = ATT_Q_HEADS * ATT_HEAD_DIM
ATT_KV_W = ATT_KV_HEADS * ATT_HEAD_DIM
IN_SPLITS = (GLA_KEY_W, GLA_KEY_W, GLA_VAL_W, GLA_VAL_W, GLA_RANK, GLA_RANK,
             ATT_Q_W, ATT_KV_W, ATT_KV_W, D_MODEL, D_MODEL)
IN_OFFSETS = tuple(int(s) for s in np.cumsum(IN_SPLITS)[:-1])
LOG2_E = 1.4426950408889634

V7X_LANES = 128
V7X_VMEM_BYTES = 64 * 1024 * 1024
VMEM_LIMIT = V7X_VMEM_BYTES - 8 * 1024 * 1024

ROW_TILE = 512
ROW_PARTS = 2
ATT_Q_TILE = 512
ATT_SHIFT_LIMIT_LOG2 = 60.0
GLA_C = 64
GLA_LEVELS = GLA_C.bit_length() - 1
GLA_GROUP = 8
GLA_SLAB = 64
GLA_WIDE_C = 256
GLA_MILD_LOG2 = 48.0
FF_CHUNKS = ((0, 1024), (1024, 1024), (2048, 768))


def _const_spec(shape):
    zeros = (0,) * len(shape)
    return pl.BlockSpec(shape, lambda *_: zeros, pipeline_mode=pl.Buffered(1))


def _params(n_axes):
    return pltpu.CompilerParams(dimension_semantics=("arbitrary",) * n_axes,
                                vmem_limit_bytes=VMEM_LIMIT)


def _rms(x, g):
    ms = jnp.mean(x * x, axis=-1, keepdims=True)
    return x * lax.rsqrt(ms + EPS) * g


def _dot(a, b):
    return jnp.dot(a, b, preferred_element_type=F32)


def _dot_nt(a, b):
    return lax.dot_general(a, b, (((1,), (1,)), ((), ())), preferred_element_type=F32)


def _dot_tn(a, b):
    return lax.dot_general(a, b, (((0,), (0,)), ((), ())), preferred_element_type=F32)


def _row_parts(ref):
    rows = ref.shape[0] // ROW_PARTS
    return [slice(p * rows, (p + 1) * rows) for p in range(ROW_PARTS)]


def _ffn_kernel(x_ref, gpre_ref, win_ref, wout_ref, gpost_ref, o_ref):
    for rows in _row_parts(x_ref):
        x = x_ref[rows, :]
        xn = _rms(x, gpre_ref[...]).astype(BF16)
        acc = None
        for c0, cw in FF_CHUNKS:
            gate = _dot(xn, win_ref[:, c0:c0 + cw])
            up = _dot(xn, win_ref[:, D_FF + c0:D_FF + c0 + cw])
            act = (gate * jax.nn.sigmoid(gate) * up).astype(BF16)
            part = _dot(act, wout_ref[c0:c0 + cw, :])
            acc = part if acc is None else acc + part
        o_ref[rows, :] = x + 0.5 * _rms(acc, gpost_ref[...])


def _ffn(x, g_pre, w_in, w_out, g_post):
    t = x.shape[0]
    tm = min(ROW_TILE * ROW_PARTS, t)
    row = pl.BlockSpec((tm, D_MODEL), lambda i: (i, 0))
    return pl.pallas_call(
        _ffn_kernel,
        grid=(t // tm,),
        in_specs=[row, _const_spec((1, D_MODEL)), _const_spec((D_MODEL, 2 * D_FF)),
                  _const_spec((D_FF, D_MODEL)), _const_spec((1, D_MODEL))],
        out_specs=row,
        out_shape=jax.ShapeDtypeStruct((t, D_MODEL), F32),
        compiler_params=_params(1),
        name="ffn",
    )(x, g_pre, w_in, w_out, g_post)


def _log_sigmoid(z):
    return jnp.minimum(z, 0.0) - jnp.log(1.0 + jnp.exp(-jnp.abs(z)))


def _rope(x, cos, sin_up, sin_dn):
    return (x * cos + pltpu.roll(x, V7X_LANES - 32, axis=1) * sin_up
            + pltpu.roll(x, 32, axis=1) * sin_dn)


def _proj_kernel(h_ref, gmix_ref, wz_ref, watt_ref, wgate_ref, wplain_ref, upf_ref, upb_ref,
                 bf_ref, bb_ref, qg_ref, kg_ref, cos_ref, sup_ref, sdn_ref,
                 gq_o, gk_o, gv_o, sr_o, lf_o, lb_o, aq_o, ak_o, av_o, sga_o, sgb_o):
    u = _rms(h_ref[...], gmix_ref[...]).astype(BF16)
    za = _dot(u, wz_ref[...]).astype(BF16)

    cos, sup, sdn = cos_ref[...], sup_ref[...], sdn_ref[...]
    att = _dot(u, watt_ref[...])
    qg = qg_ref[...] * (ATT_HEAD_DIM ** -0.5 * LOG2_E)
    for hd in range(ATT_Q_HEADS):
        sl = slice(hd * ATT_HEAD_DIM, (hd + 1) * ATT_HEAD_DIM)
        aq_o[hd] = _rope(_rms(att[:, sl], qg), cos, sup, sdn).astype(BF16)
    for hd in range(ATT_KV_HEADS):
        sl = slice(hd * ATT_HEAD_DIM, (hd + 1) * ATT_HEAD_DIM)
        ak_o[:, sl] = _rope(_rms(att[:, ATT_Q_W + sl.start:ATT_Q_W + sl.stop], kg_ref[...]),
                            cos, sup, sdn).astype(BF16)

    gate = _dot(u, wgate_ref[...])
    sga_o[...] = jax.nn.sigmoid(gate[:, 0:D_MODEL]).astype(BF16)
    sgb_o[...] = jax.nn.sigmoid(gate[:, D_MODEL:2 * D_MODEL]).astype(BF16)
    gr = gate[:, 2 * D_MODEL:]
    sr_o[...] = (gr * jax.nn.sigmoid(gr)).astype(BF16)
    lf_o[...] = _log_sigmoid(_dot(za, upf_ref[...]) + bf_ref[...]) * (LOG2_E / GLA_TAU)
    lb_o[...] = _log_sigmoid(_dot(za, upb_ref[...]) + bb_ref[...]) * (LOG2_E / GLA_TAU)

    plain = _dot(u, wplain_ref[...])
    gq_o[...] = (plain[:, 0:GLA_KEY_W] * (GLA_DK ** -0.5)).astype(BF16)
    gk_o[...] = plain[:, GLA_KEY_W:2 * GLA_KEY_W].astype(BF16)
    gv_o[...] = plain[:, 2 * GLA_KEY_W:2 * GLA_KEY_W + GLA_VAL_W].astype(BF16)
    av_o[...] = plain[:, 2 * GLA_KEY_W + GLA_VAL_W:].astype(BF16)


def _rope_tables(seq):
    pos = np.arange(seq)
    half = ATT_HEAD_DIM // 2
    inv_freq = ROPE_THETA ** (-np.arange(0, half, 2, dtype=np.float64) / half)
    ang_r = (pos // GRID_W)[:, None] * inv_freq
    ang_c = (pos % GRID_W)[:, None] * inv_freq
    zero = np.zeros_like(ang_r)
    cos = np.concatenate([np.cos(ang_r)] * 2 + [np.cos(ang_c)] * 2, axis=-1)
    sin_up = np.concatenate([-np.sin(ang_r), zero, -np.sin(ang_c), zero], axis=-1)
    sin_dn = np.concatenate([zero, np.sin(ang_r), zero, np.sin(ang_c)], axis=-1)
    return tuple(jnp.asarray(t, F32) for t in (cos, sin_up, sin_dn))


def _proj(h, seq, g_mix, w_parts, up_f, up_b, bias_f, bias_b, q_gain, k_gain):
    t = h.shape[0]
    tm = min(ROW_TILE, seq)
    tiles_per_seq = seq // tm
    cos, sin_up, sin_dn = _rope_tables(seq)

    def row(width):
        return pl.BlockSpec((tm, width), lambda i: (i, 0))

    table = pl.BlockSpec((tm, ATT_HEAD_DIM), lambda i: (i % tiles_per_seq, 0))
    consts_a = (g_mix, *w_parts, up_f, up_b, bias_f, bias_b, q_gain, k_gain)
    out_widths = (GLA_KEY_W, GLA_KEY_W, GLA_VAL_W, GLA_VAL_W, GLA_KEY_W, GLA_KEY_W,
                  None, ATT_KV_W, ATT_KV_W, D_MODEL, D_MODEL)
    out_dtypes = (BF16, BF16, BF16, BF16, F32, F32, BF16, BF16, BF16, BF16, BF16)
    q_heads = pl.BlockSpec((ATT_Q_HEADS, tm, ATT_HEAD_DIM), lambda i: (0, i, 0))
    return pl.pallas_call(
        _proj_kernel,
        grid=(t // tm,),
        in_specs=[row(D_MODEL)] + [_const_spec(c.shape) for c in consts_a] + [table] * 3,
        out_specs=[q_heads if w is None else row(w) for w in out_widths],
        out_shape=[jax.ShapeDtypeStruct((ATT_Q_HEADS, t, ATT_HEAD_DIM) if w is None else (t, w), d)
                   for w, d in zip(out_widths, out_dtypes)],
        compiler_params=_params(1),
        name="proj",
    )(h, *consts_a, cos, sin_up, sin_dn)


def _gla_constants():
    c, nl = GLA_C, GLA_LEVELS
    i = np.arange(c)[:, None]
    m = np.arange(c)[None, :]
    sum_f = [m <= i, m > i]
    sum_b = [m >= i, m < i]
    role_f, role_b, mask_f, mask_b = [], [], [], []
    for lvl in range(nl):
        s = c >> (lvl + 1)
        start = (i // (2 * s)) * 2 * s
        mid = start + s
        upper = (i % (2 * s)) >= s
        sum_f.append(np.where(upper, (m >= mid) & (m <= i), (m > i) & (m < mid)))
        sum_b.append(np.where(upper, (m >= mid) & (m < i), (m >= i) & (m < mid)))
        same = (i // (2 * s)) == (m // (2 * s))
        m_upper = (m % (2 * s)) >= s
        role_f.append(np.broadcast_to(upper, (c, V7X_LANES)))
        role_b.append(np.broadcast_to(~upper, (c, V7X_LANES)))
        mask_f.append(same & upper & ~m_upper)
        mask_b.append(same & ~upper & m_upper)
    sum_f = np.concatenate(sum_f, axis=0).astype(np.float32)
    sum_b = np.concatenate(sum_b, axis=0).astype(np.float32)
    sum_f = np.concatenate([sum_f, sum_f], axis=1)
    sum_b = np.concatenate([sum_b, sum_b], axis=1)
    roles = np.stack(role_f + role_b).astype(np.float32)
    masks = np.stack(mask_f + mask_b + [i == m]).astype(np.float32)
    wide = np.arange(GLA_WIDE_C)
    tri = (wide[None, :] <= wide[:, None]).astype(np.float32)
    tri = np.concatenate([tri, tri], axis=1)
    return (jnp.asarray(sum_f, BF16), jnp.asarray(sum_b, BF16), jnp.asarray(roles), jnp.asarray(masks),
            jnp.asarray(tri, BF16))


def _gla_kernel(q_ref, k_ref, v_ref, sr_ref, gf_ref, gb_ref, gain_ref,
                sumf_ref, sumb_ref, role_ref, mask_ref, tri_ref,
                o_ref,
                acc_ref, qe_ref, upd_ref, snap_ref, dec_ref):
    c, nl, dk, dv = GLA_C, GLA_LEVELS, GLA_DK, GLA_DV
    n_chunks = q_ref.shape[0] // c

    def decay_sums(g, sum_ref, n_rows):
        g_hi = g.astype(BF16)
        g_lo = (g - g_hi.astype(F32)).astype(BF16)
        return _dot(sum_ref[0:n_rows, :], jnp.concatenate([g_hi, g_lo], axis=0))

    def store_intra(rows, a, v):
        acc_ref[rows, :] = _dot(a.astype(BF16), v)

    def store_edges(i, rows, v, qe_f, qe_b, kd_f, kd_b, total_f, total_b):
        qe_ref[rows, :] = jnp.concatenate([qe_f, qe_b], axis=1).astype(BF16)
        kd = jnp.concatenate([kd_f, kd_b], axis=1).astype(BF16)
        upd_ref[i] = _dot_tn(v, kd)
        total = jnp.concatenate([total_f, total_b], axis=1)
        dec_ref[pl.ds(pl.multiple_of(i * 8, 8), 8), :] = jnp.broadcast_to(total, (8, 2 * dk))

    def chunk_any_decay(i, carry):
        rows = pl.ds(pl.multiple_of(i * c, c), c)
        q, k, v = q_ref[rows, :], k_ref[rows, :], v_ref[rows, :]
        qf, kf = q.astype(F32), k.astype(F32)
        ef = jnp.exp2(decay_sums(gf_ref[rows, :], sumf_ref, (2 + nl) * c))
        eb = jnp.exp2(decay_sums(gb_ref[rows, :], sumb_ref, (2 + nl) * c))
        a = _dot_nt(q, k) * mask_ref[2 * nl]
        for lvl in range(nl):
            blk = slice((2 + lvl) * c, (3 + lvl) * c)
            xf = (jnp.where(role_ref[lvl] > 0.5, qf, kf) * ef[blk]).astype(BF16)
            a = a + _dot_nt(xf, xf) * mask_ref[lvl]
            xb = (jnp.where(role_ref[nl + lvl] > 0.5, qf, kf) * eb[blk]).astype(BF16)
            a = a + _dot_nt(xb, xb) * mask_ref[nl + lvl]
        store_intra(rows, a, v)
        store_edges(i, rows, v, qf * ef[0:c], qf * eb[0:c], kf * ef[c:2 * c], kf * eb[c:2 * c],
                    ef[c - 1:c], eb[0:1])
        return carry

    w = GLA_WIDE_C
    n_wide = q_ref.shape[0] // w

    def chunks_mild_decay(step, carry):
        ids = [step * GLA_GROUP + u for u in range(GLA_GROUP)]
        rows = [pl.ds(pl.multiple_of(i * w, w), w) for i in ids]
        g_b = [gb_ref[r, :] for r in rows]
        prefix = [decay_sums(jnp.concatenate([gf_ref[r, :], g], axis=1), tri_ref, w)
                  for r, g in zip(rows, g_b)]
        lower = (lax.broadcasted_iota(jnp.int32, (w, w), 0) >= lax.broadcasted_iota(jnp.int32, (w, w), 1))
        scores, rest = [], []
        for r, g, p in zip(rows, g_b, prefix):
            qf, kf = q_ref[r, :].astype(F32), k_ref[r, :].astype(F32)
            bf, pb = p[:, 0:dk], p[:, dk:2 * dk]
            bb = (pb[w - 1:w] - pb) + g
            mid_f, mid_b = bf[w // 2 - 1:w // 2], bb[w // 2:w // 2 + 1]
            rf, rb = bf - mid_f, bb - mid_b
            q_f, k_f = qf * jnp.exp2(rf), kf * jnp.exp2(-rf)
            q_b, k_b = qf * jnp.exp2(rb), kf * jnp.exp2(-rb)
            scores.append((_dot_nt(q_f.astype(BF16), k_f.astype(BF16)),
                           _dot_nt(q_b.astype(BF16), k_b.astype(BF16))))
            rest.append((q_f * jnp.exp2(mid_f), q_b * jnp.exp2(mid_b),
                         k_f * jnp.exp2(bf[w - 1:w] - mid_f), k_b * jnp.exp2(bb[0:1] - mid_b),
                         jnp.exp2(bf[w - 1:w]), jnp.exp2(bb[0:1])))
        for r, (s_f, s_b) in zip(rows, scores):
            store_intra(r, jnp.where(lower, s_f, s_b), v_ref[r, :])
        for i, r, parts in zip(ids, rows, rest):
            store_edges(i, r, v_ref[r, :], *parts)
        return carry

    def scan_states(n):
        for slab in range(dv // GLA_SLAB):
            srows = slice(slab * GLA_SLAB, (slab + 1) * GLA_SLAB)

            def scan(step, carry):
                sf, sb = carry
                i, j = step, n - 1 - step
                snap_ref[i, srows, 0:dk] = sf.astype(BF16)
                snap_ref[j, srows, dk:2 * dk] = sb.astype(BF16)
                dec_f = dec_ref[pl.ds(pl.multiple_of(i * 8, 8), 8), 0:dk]
                dec_b = dec_ref[pl.ds(pl.multiple_of(j * 8, 8), 8), dk:2 * dk]
                sf = (sf.reshape(-1, 8, dk) * dec_f[None]).reshape(GLA_SLAB, dk) + upd_ref[i, srows, 0:dk]
                sb = (sb.reshape(-1, 8, dk) * dec_b[None]).reshape(GLA_SLAB, dk) + upd_ref[j, srows, dk:2 * dk]
                return sf, sb

            zero = jnp.zeros((GLA_SLAB, dk), F32)
            lax.fori_loop(0, n, scan, (zero, zero), unroll=2)

    def finish(size, group):
        def body(step, carry):
            ids = [step * group + u for u in range(group)]
            rows = [pl.ds(pl.multiple_of(i * size, size), size) for i in ids]
            inter = [_dot_nt(qe_ref[r, :], snap_ref[i]) for i, r in zip(ids, rows)]
            for r, o_inter in zip(rows, inter):
                o = acc_ref[r, :] + o_inter
                o_ref[r, :] = (_rms(o, gain_ref[...]) * sr_ref[r, :].astype(F32)).astype(BF16)
            return carry
        return body

    def chunk_totals(g_ref):
        return jnp.sum(g_ref[...].reshape(n_wide, w, dk), axis=1)
    mild = jnp.min(jnp.minimum(chunk_totals(gf_ref), chunk_totals(gb_ref))) >= -GLA_MILD_LOG2

    @pl.when(mild)
    def _():
        lax.fori_loop(0, n_wide // GLA_GROUP, chunks_mild_decay, 0)
        scan_states(n_wide)
        lax.fori_loop(0, n_wide // GLA_GROUP, finish(w, GLA_GROUP), 0)

    @pl.when(jnp.logical_not(mild))
    def _():
        lax.fori_loop(0, n_chunks, chunk_any_decay, 0)
        scan_states(n_chunks)
        lax.fori_loop(0, n_chunks, finish(c, 1), 0)


def _gla(gq, gk, gv, sr, lgf, lgb, gain, batch, seq):
    consts = _gla_constants()
    n_chunks = seq // GLA_C

    def head(width):
        return pl.BlockSpec((None, seq, width), lambda b, h: (b, 0, h))

    return pl.pallas_call(
        _gla_kernel,
        grid=(batch, GLA_HEADS),
        in_specs=[head(GLA_DK), head(GLA_DK), head(GLA_DV), head(GLA_DV), head(GLA_DK), head(GLA_DK),
                  pl.BlockSpec((None, 1, GLA_DV), lambda b, h: (h, 0, 0))]
                 + [_const_spec(cst.shape) for cst in consts],
        out_specs=head(GLA_DV),
        out_shape=jax.ShapeDtypeStruct((batch, seq, GLA_VAL_W), BF16),
        scratch_shapes=[pltpu.VMEM((seq, GLA_DV), F32),
                        pltpu.VMEM((seq, 2 * GLA_DK), BF16),
                        pltpu.VMEM((n_chunks, GLA_DV, 2 * GLA_DK), F32),
                        pltpu.VMEM((n_chunks, GLA_DV, 2 * GLA_DK), BF16),
                        pltpu.VMEM((n_chunks * 8, 2 * GLA_DK), F32)],
        compiler_params=_params(2),
        name="gla",
    )(gq, gk, gv, sr, lgf, lgb, gain, *consts)


def _attn_kernel(bound_ref, q_ref, k_ref, v_ref, o_ref, vext_ref):
    hd = ATT_HEAD_DIM
    tq = q_ref.shape[1]

    @pl.when(pl.program_id(2) == 0)
    def _():
        vext_ref[:, :hd] = v_ref[...]
        vext_ref[:, hd:] = jnp.ones((v_ref.shape[0], hd), BF16)

    def finish(p):
        o_ext = _dot(p, vext_ref[...])
        o = (o_ext[:, :hd] / o_ext[:, hd:]).astype(BF16)
        for g in range(ATT_GROUP):
            o_ref[:, g * hd:(g + 1) * hd] = o[g * tq:(g + 1) * tq]

    def scores():
        q = q_ref[...].reshape(ATT_GROUP * tq, hd)
        return _dot_nt(q, k_ref[...])

    bound = bound_ref[0]
    small = bound <= ATT_SHIFT_LIMIT_LOG2

    @pl.when(small)
    def _():
        finish(jnp.exp2(scores() - bound).astype(BF16))

    @pl.when(jnp.logical_not(small))
    def _():
        s = scores()
        finish(jnp.exp2(s - jnp.max(s, axis=-1, keepdims=True)).astype(BF16))


def _attn(bound, aq, ak, av, batch, seq):
    tq = min(ATT_Q_TILE, seq)
    nq = seq // tq
    qspec = pl.BlockSpec((ATT_GROUP, tq, ATT_HEAD_DIM), lambda b, h, i: (h, b * nq + i, 0))
    kvspec = pl.BlockSpec((None, seq, ATT_HEAD_DIM), lambda b, h, i: (b, 0, h))
    ospec = pl.BlockSpec((None, tq, ATT_GROUP * ATT_HEAD_DIM), lambda b, h, i: (b, i, h))
    return pl.pallas_call(
        _attn_kernel,
        grid=(batch, ATT_KV_HEADS, nq),
        in_specs=[pl.BlockSpec(memory_space=pltpu.SMEM), qspec, kvspec, kvspec],
        out_specs=ospec,
        out_shape=jax.ShapeDtypeStruct((batch, seq, ATT_Q_W), BF16),
        scratch_shapes=[pltpu.VMEM((seq, 2 * ATT_HEAD_DIM), BF16)],
        compiler_params=_params(3),
        name="attn",
    )(bound, aq, ak, av)


def _score_bound(q_gain, k_gain):
    return (jnp.max(jnp.abs(q_gain)) * jnp.max(jnp.abs(k_gain))
            * (1.02 * ATT_HEAD_DIM ** 0.5 * LOG2_E)).reshape(1).astype(F32)


def _merge_kernel(h_ref, oa_ref, ob_ref, sga_ref, sgb_ref, wa_ref, wb_ref, wo_ref, gpost_ref, o_ref):
    for rows in _row_parts(h_ref):
        ya = _dot(oa_ref[rows, :], wa_ref[...])
        yb = _dot(ob_ref[rows, :], wb_ref[...])
        merged = sga_ref[rows, :].astype(F32) * ya + sgb_ref[rows, :].astype(F32) * yb
        m = _dot(merged.astype(BF16), wo_ref[...])
        o_ref[rows, :] = h_ref[rows, :] + _rms(m, gpost_ref[...])


def _merge(h, oa, ob, sga, sgb, w_a, w_b, w_o, g_post):
    t = h.shape[0]
    tm = min(ROW_TILE * ROW_PARTS, t)
    row = pl.BlockSpec((tm, D_MODEL), lambda i: (i, 0))
    sq = _const_spec((D_MODEL, D_MODEL))
    return pl.pallas_call(
        _merge_kernel,
        grid=(t // tm,),
        in_specs=[row] * 5 + [sq, sq, sq, _const_spec((1, D_MODEL))],
        out_specs=row,
        out_shape=jax.ShapeDtypeStruct((t, D_MODEL), F32),
        compiler_params=_params(1),
        name="merge",
    )(h, oa, ob, sga, sgb, w_a, w_b, w_o, g_post)


def _split_w_in(w_in):
    gq, gk, gv, gr, za_f, za_b, aq, ak, av, ga, gb = jnp.split(w_in.astype(BF16), IN_OFFSETS, axis=-1)
    pad = jnp.zeros((D_MODEL, V7X_LANES - 2 * GLA_RANK), BF16)
    cat = lambda *parts: jnp.concatenate(parts, axis=-1)
    return cat(za_f, za_b, pad), cat(aq, ak), cat(ga, gb, gr), cat(gq, gk, gv, av)


def _pad_decay_up(up, first_row):
    out = jnp.zeros((V7X_LANES, GLA_KEY_W), BF16)
    return lax.dynamic_update_slice(out, up.astype(BF16), (first_row, 0))


def kernel(x, ffn1_pre_g, ffn1_w_in, ffn1_w_out, ffn1_post_g, mix_pre_g, w_in, gla_decay_up_f,
           gla_decay_bias_f, gla_decay_up_b, gla_decay_bias_b, gla_out_g, w_branch_a, att_q_norm_g,
           att_k_norm_g, w_branch_b, w_out, mix_post_g, ffn2_pre_g, ffn2_w_in, ffn2_w_out, ffn2_post_g):
    batch, seq, d = x.shape
    assert d == D_MODEL and seq % max(GLA_WIDE_C * GLA_GROUP, ATT_Q_TILE, GRID_W) == 0
    assert seq % ROW_TILE == 0 or seq < ROW_TILE
    depth = w_in.shape[0]
    t = batch * seq
    h = x.reshape(t, d)
    vec = lambda g: g.reshape(1, -1).astype(F32)
    for l in range(depth):
        h = _ffn(h, vec(ffn1_pre_g[l]), ffn1_w_in[l].astype(BF16), ffn1_w_out[l].astype(BF16),
                 vec(ffn1_post_g[l]))
        gq, gk, gv, sr, lgf, lgb, aq, ak, av, sga, sgb = _proj(
            h, seq, vec(mix_pre_g[l]), _split_w_in(w_in[l]),
            _pad_decay_up(gla_decay_up_f[l], 0), _pad_decay_up(gla_decay_up_b[l], GLA_RANK),
            vec(gla_decay_bias_f[l]), vec(gla_decay_bias_b[l]),
            vec(att_q_norm_g[l]), vec(att_k_norm_g[l]))
        per_seq = lambda a: a.reshape(batch, seq, a.shape[-1])
        oa = _gla(per_seq(gq), per_seq(gk), per_seq(gv), per_seq(sr), per_seq(lgf), per_seq(lgb),
                  gla_out_g[l].reshape(GLA_HEADS, 1, GLA_DV).astype(F32), batch, seq)
        ob = _attn(_score_bound(att_q_norm_g[l], att_k_norm_g[l]), aq, per_seq(ak), per_seq(av), batch, seq)
        h = _merge(h, oa.reshape(t, GLA_VAL_W), ob.reshape(t, ATT_Q_W), sga, sgb,
                   w_branch_a[l].astype(BF16), w_branch_b[l].astype(BF16), w_out[l].astype(BF16),
                   vec(mix_post_g[l]))
        h = _ffn(h, vec(ffn2_pre_g[l]), ffn2_w_in[l].astype(BF16), ffn2_w_out[l].astype(BF16),
                 vec(ffn2_post_g[l]))
    return h.reshape(batch, seq, d)
```

```python
import functools

import numpy as np
import jax
import jax.numpy as jnp
from jax import lax
from jax.experimental import pallas as pl
from jax.experimental.pallas import tpu as pltpu

F32 = jnp.float32
BF16 = jnp.bfloat16

D_MODEL = 1024
GRID_W = 64
D_FF = 2816
EPS = 1e-6
GLA_HEADS = 4
GLA_DK = 128
GLA_DV = 256
GLA_RANK = 16
GLA_TAU = 16.0
ATT_Q_HEADS = 8
ATT_KV_HEADS = 2
ATT_HEAD_DIM = 128
ATT_GROUP = ATT_Q_HEADS // ATT_KV_HEADS
ROPE_THETA = 10000.0
GLA_KEY_W = GLA_HEADS * GLA_DK
GLA_VAL_W = GLA_HEADS * GLA_DV
ATT_Q_W = ATT_Q_HEADS * ATT_HEAD_DIM
ATT_KV_W = ATT_KV_HEADS * ATT_HEAD_DIM
IN_SPLITS = (GLA_KEY_W, GLA_KEY_W, GLA_VAL_W, GLA_VAL_W, GLA_RANK, GLA_RANK,
             ATT_Q_W, ATT_KV_W, ATT_KV_W, D_MODEL, D_MODEL)
IN_OFFSETS = tuple(int(s) for s in np.cumsum(IN_SPLITS)[:-1])
LOG2_E = 1.4426950408889634

V7X_LANES = 128
V7X_VMEM_BYTES = 64 * 1024 * 1024
VMEM_LIMIT = V7X_VMEM_BYTES - 8 * 1024 * 1024

ROW_TILE = 256
ROW_PARTS = 4
PROJ_ROW_TILE = 512
ATT_Q_TILE = 512
ATT_SHIFT_LIMIT_LOG2 = 60.0
GLA_C = 64
GLA_LEVELS = GLA_C.bit_length() - 1
GLA_GROUP = 8
GLA_SLAB = 64
GLA_WIDE_C = 256
GLA_MILD_LOG2 = 48.0
FF_CHUNKS = ((0, 1024), (1024, 1024), (2048, 768))


def _const_spec(shape):
    zeros = (0,) * len(shape)
    return pl.BlockSpec(shape, lambda *_: zeros, pipeline_mode=pl.Buffered(1))


def _params(n_axes):
    return pltpu.CompilerParams(dimension_semantics=("arbitrary",) * n_axes,
                                vmem_limit_bytes=VMEM_LIMIT)


def _rms(x, g):
    ms = jnp.mean(x * x, axis=-1, keepdims=True)
    return x * lax.rsqrt(ms + EPS) * g


def _dot(a, b):
    return jnp.dot(a, b, preferred_element_type=F32)


def _dot_nt(a, b):
    return lax.dot_general(a, b, (((1,), (1,)), ((), ())), preferred_element_type=F32)


def _dot_tn(a, b):
    return lax.dot_general(a, b, (((0,), (0,)), ((), ())), preferred_element_type=F32)


def _row_parts(ref):
    rows = ref.shape[0] // ROW_PARTS
    return [slice(p * rows, (p + 1) * rows) for p in range(ROW_PARTS)]


def _ffn_kernel(x_ref, gpre_ref, win_ref, wout_ref, gpost_ref, o_ref):
    for rows in _row_parts(x_ref):
        x = x_ref[rows, :]
        xn = _rms(x, gpre_ref[...]).astype(BF16)
        acc = None
        for c0, cw in FF_CHUNKS:
            gate = _dot(xn, win_ref[:, c0:c0 + cw])
            up = _dot(xn, win_ref[:, D_FF + c0:D_FF + c0 + cw])
            act = (gate * jax.nn.sigmoid(gate) * up).astype(BF16)
            part = _dot(act, wout_ref[c0:c0 + cw, :])
            acc = part if acc is None else acc + part
        o_ref[rows, :] = x + 0.5 * _rms(acc, gpost_ref[...])


def _ffn(x, g_pre, w_in, w_out, g_post):
    t = x.shape[0]
    tm = min(ROW_TILE * ROW_PARTS, t)
    row = pl.BlockSpec((tm, D_MODEL), lambda i: (i, 0))
    return pl.pallas_call(
        _ffn_kernel,
        grid=(t // tm,),
        in_specs=[row, _const_spec((1, D_MODEL)), _const_spec((D_MODEL, 2 * D_FF)),
                  _const_spec((D_FF, D_MODEL)), _const_spec((1, D_MODEL))],
        out_specs=row,
        out_shape=jax.ShapeDtypeStruct((t, D_MODEL), F32),
        compiler_params=_params(1),
        name="ffn",
    )(x, g_pre, w_in, w_out, g_post)


def _log_sigmoid(z):
    return jnp.minimum(z, 0.0) - jnp.log(1.0 + jnp.exp(-jnp.abs(z)))


def _rope(x, cos, sin_up, sin_dn):
    return (x * cos + pltpu.roll(x, V7X_LANES - 32, axis=1) * sin_up
            + pltpu.roll(x, 32, axis=1) * sin_dn)


def _proj_kernel(h_ref, gmix_ref, wz_ref, watt_ref, wgate_ref, wplain_ref, upf_ref, upb_ref,
                 bf_ref, bb_ref, qg_ref, kg_ref, cos_ref, sup_ref, sdn_ref,
                 gq_o, gk_o, gv_o, sr_o, lf_o, lb_o, aq_o, ak_o, av_o, sga_o, sgb_o):
    u = _rms(h_ref[...], gmix_ref[...]).astype(BF16)
    za = _dot(u, wz_ref[...]).astype(BF16)

    cos, sup, sdn = cos_ref[...], sup_ref[...], sdn_ref[...]
    att = _dot(u, watt_ref[...])
    qg = qg_ref[...] * (ATT_HEAD_DIM ** -0.5 * LOG2_E)
    for hd in range(ATT_Q_HEADS):
        sl = slice(hd * ATT_HEAD_DIM, (hd + 1) * ATT_HEAD_DIM)
        aq_o[hd] = _rope(_rms(att[:, sl], qg), cos, sup, sdn).astype(BF16)
    for hd in range(ATT_KV_HEADS):
        sl = slice(hd * ATT_HEAD_DIM, (hd + 1) * ATT_HEAD_DIM)
        ak_o[:, sl] = _rope(_rms(att[:, ATT_Q_W + sl.start:ATT_Q_W + sl.stop], kg_ref[...]),
                            cos, sup, sdn).astype(BF16)

    gate = _dot(u, wgate_ref[...])
    sga_o[...] = jax.nn.sigmoid(gate[:, 0:D_MODEL]).astype(BF16)
    sgb_o[...] = jax.nn.sigmoid(gate[:, D_MODEL:2 * D_MODEL]).astype(BF16)
    gr = gate[:, 2 * D_MODEL:]
    sr_o[...] = (gr * jax.nn.sigmoid(gr)).astype(BF16)
    lf_o[...] = _log_sigmoid(_dot(za, upf_ref[...]) + bf_ref[...]) * (LOG2_E / GLA_TAU)
    lb_o[...] = _log_sigmoid(_dot(za, upb_ref[...]) + bb_ref[...]) * (LOG2_E / GLA_TAU)

    plain = _dot(u, wplain_ref[...])
    gq_o[...] = (plain[:, 0:GLA_KEY_W] * (GLA_DK ** -0.5)).astype(BF16)
    gk_o[...] = plain[:, GLA_KEY_W:2 * GLA_KEY_W].astype(BF16)
    gv_o[...] = plain[:, 2 * GLA_KEY_W:2 * GLA_KEY_W + GLA_VAL_W].astype(BF16)
    av_o[...] = plain[:, 2 * GLA_KEY_W + GLA_VAL_W:].astype(BF16)


def _rope_tables(seq):
    pos = np.arange(seq)
    half = ATT_HEAD_DIM // 2
    inv_freq = ROPE_THETA ** (-np.arange(0, half, 2, dtype=np.float64) / half)
    ang_r = (pos // GRID_W)[:, None] * inv_freq
    ang_c = (pos % GRID_W)[:, None] * inv_freq
    zero = np.zeros_like(ang_r)
    cos = np.concatenate([np.cos(ang_r)] * 2 + [np.cos(ang_c)] * 2, axis=-1)
    sin_up = np.concatenate([-np.sin(ang_r), zero, -np.sin(ang_c), zero], axis=-1)
    sin_dn = np.concatenate([zero, np.sin(ang_r), zero, np.sin(ang_c)], axis=-1)
    return tuple(jnp.asarray(t, F32) for t in (cos, sin_up, sin_dn))


def _proj(h, seq, g_mix, w_parts, up_f, up_b, bias_f, bias_b, q_gain, k_gain):
    t = h.shape[0]
    tm = min(PROJ_ROW_TILE, seq)
    tiles_per_seq = seq // tm
    cos, sin_up, sin_dn = _rope_tables(seq)

    def row(width):
        return pl.BlockSpec((tm, width), lambda i: (i, 0))

    table = pl.BlockSpec((tm, ATT_HEAD_DIM), lambda i: (i % tiles_per_seq, 0))
    consts_a = (g_mix, *w_parts, up_f, up_b, bias_f, bias_b, q_gain, k_gain)
    out_widths = (GLA_KEY_W, GLA_KEY_W, GLA_VAL_W, GLA_VAL_W, GLA_KEY_W, GLA_KEY_W,
                  None, ATT_KV_W, ATT_KV_W, D_MODEL, D_MODEL)
    out_dtypes = (BF16, BF16, BF16, BF16, F32, F32, BF16, BF16, BF16, BF16, BF16)
    q_heads = pl.BlockSpec((ATT_Q_HEADS, tm, ATT_HEAD_DIM), lambda i: (0, i, 0))
    return pl.pallas_call(
        _proj_kernel,
        grid=(t // tm,),
        in_specs=[row(D_MODEL)] + [_const_spec(c.shape) for c in consts_a] + [table] * 3,
        out_specs=[q_heads if w is None else row(w) for w in out_widths],
        out_shape=[jax.ShapeDtypeStruct((ATT_Q_HEADS, t, ATT_HEAD_DIM) if w is None else (t, w), d)
                   for w, d in zip(out_widths, out_dtypes)],
        compiler_params=_params(1),
        name="proj",
    )(h, *consts_a, cos, sin_up, sin_dn)


def _gla_constants():
    c, nl = GLA_C, GLA_LEVELS
    i = np.arange(c)[:, None]
    m = np.arange(c)[None, :]
    sum_f = [m <= i, m > i]
    sum_b = [m >= i, m < i]
    role_f, role_b, mask_f, mask_b = [], [], [], []
    for lvl in range(nl):
        s = c >> (lvl + 1)
        start = (i // (2 * s)) * 2 * s
        mid = start + s
        upper = (i % (2 * s)) >= s
        sum_f.append(np.where(upper, (m >= mid) & (m <= i), (m > i) & (m < mid)))
        sum_b.append(np.where(upper, (m >= mid) & (m < i), (m >= i) & (m < mid)))
        same = (i // (2 * s)) == (m // (2 * s))
        m_upper = (m % (2 * s)) >= s
        role_f.append(np.broadcast_to(upper, (c, V7X_LANES)))
        role_b.append(np.broadcast_to(~upper, (c, V7X_LANES)))
        mask_f.append(same & upper & ~m_upper)
        mask_b.append(same & ~upper & m_upper)
    sum_f = np.concatenate(sum_f, axis=0).astype(np.float32)
    sum_b = np.concatenate(sum_b, axis=0).astype(np.float32)
    sum_f = np.concatenate([sum_f, sum_f], axis=1)
    sum_b = np.concatenate([sum_b, sum_b], axis=1)
    roles = np.stack(role_f + role_b).astype(np.float32)
    masks = np.stack(mask_f + mask_b + [i == m]).astype(np.float32)
    wide = np.arange(GLA_WIDE_C)
    tri = (wide[None, :] <= wide[:, None]).astype(np.float32)
    tri = np.concatenate([tri, tri], axis=1)
    return (jnp.asarray(sum_f, BF16), jnp.asarray(sum_b, BF16), jnp.asarray(roles), jnp.asarray(masks),
            jnp.asarray(tri, BF16))


def _gla_kernel(q_ref, k_ref, v_ref, sr_ref, gf_ref, gb_ref, gain_ref,
                sumf_ref, sumb_ref, role_ref, mask_ref, tri_ref,
                o_ref,
                acc_ref, qe_ref, upd_ref, snap_ref, dec_ref):
    c, nl, dk, dv = GLA_C, GLA_LEVELS, GLA_DK, GLA_DV
    n_chunks = q_ref.shape[0] // c

    def decay_sums(g, sum_ref, n_rows):
        g_hi = g.astype(BF16)
        g_lo = (g - g_hi.astype(F32)).astype(BF16)
        return _dot(sum_ref[0:n_rows, :], jnp.concatenate([g_hi, g_lo], axis=0))

    def store_intra(rows, a, v):
        acc_ref[rows, :] = _dot(a.astype(BF16), v)

    def store_edges(i, rows, v, qe_f, qe_b, kd_f, kd_b, total_f, total_b):
        qe_ref[rows, :] = jnp.concatenate([qe_f, qe_b], axis=1).astype(BF16)
        kd = jnp.concatenate([kd_f, kd_b], axis=1).astype(BF16)
        upd_ref[i] = _dot_tn(v, kd)
        total = jnp.concatenate([total_f, total_b], axis=1)
        dec_ref[pl.ds(pl.multiple_of(i * 8, 8), 8), :] = jnp.broadcast_to(total, (8, 2 * dk))

    def chunk_any_decay(i, carry):
        rows = pl.ds(pl.multiple_of(i * c, c), c)
        q, k, v = q_ref[rows, :], k_ref[rows, :], v_ref[rows, :]
        qf, kf = q.astype(F32), k.astype(F32)
        ef = jnp.exp2(decay_sums(gf_ref[rows, :], sumf_ref, (2 + nl) * c))
        eb = jnp.exp2(decay_sums(gb_ref[rows, :], sumb_ref, (2 + nl) * c))
        a = _dot_nt(q, k) * mask_ref[2 * nl]
        for lvl in range(nl):
            blk = slice((2 + lvl) * c, (3 + lvl) * c)
            xf = (jnp.where(role_ref[lvl] > 0.5, qf, kf) * ef[blk]).astype(BF16)
            a = a + _dot_nt(xf, xf) * mask_ref[lvl]
            xb = (jnp.where(role_ref[nl + lvl] > 0.5, qf, kf) * eb[blk]).astype(BF16)
            a = a + _dot_nt(xb, xb) * mask_ref[nl + lvl]
        store_intra(rows, a, v)
        store_edges(i, rows, v, qf * ef[0:c], qf * eb[0:c], kf * ef[c:2 * c], kf * eb[c:2 * c],
                    ef[c - 1:c], eb[0:1])
        return carry

    w = GLA_WIDE_C
    n_wide = q_ref.shape[0] // w

    def chunks_mild_decay(step, carry):
        ids = [step * GLA_GROUP + u for u in range(GLA_GROUP)]
        rows = [pl.ds(pl.multiple_of(i * w, w), w) for i in ids]
        g_b = [gb_ref[r, :] for r in rows]
        prefix = [decay_sums(jnp.concatenate([gf_ref[r, :], g], axis=1), tri_ref, w)
                  for r, g in zip(rows, g_b)]
        lower = (lax.broadcasted_iota(jnp.int32, (w, w), 0) >= lax.broadcasted_iota(jnp.int32, (w, w), 1))
        scores, rest = [], []
        for r, g, p in zip(rows, g_b, prefix):
            qf, kf = q_ref[r, :].astype(F32), k_ref[r, :].astype(F32)
            bf, pb = p[:, 0:dk], p[:, dk:2 * dk]
            bb = (pb[w - 1:w] - pb) + g
            mid_f, mid_b = bf[w // 2 - 1:w // 2], bb[w // 2:w // 2 + 1]
            rf, rb = bf - mid_f, bb - mid_b
            q_f, k_f = qf * jnp.exp2(rf), kf * jnp.exp2(-rf)
            q_b, k_b = qf * jnp.exp2(rb), kf * jnp.exp2(-rb)
            scores.append((_dot_nt(q_f.astype(BF16), k_f.astype(BF16)),
                           _dot_nt(q_b.astype(BF16), k_b.astype(BF16))))
            rest.append((q_f * jnp.exp2(mid_f), q_b * jnp.exp2(mid_b),
                         k_f * jnp.exp2(bf[w - 1:w] - mid_f), k_b * jnp.exp2(bb[0:1] - mid_b),
                         jnp.exp2(bf[w - 1:w]), jnp.exp2(bb[0:1])))
        for r, (s_f, s_b) in zip(rows, scores):
            store_intra(r, jnp.where(lower, s_f, s_b), v_ref[r, :])
        for i, r, parts in zip(ids, rows, rest):
            store_edges(i, r, v_ref[r, :], *parts)
        return carry

    def scan_states(n):
        for slab in range(dv // GLA_SLAB):
            srows = slice(slab * GLA_SLAB, (slab + 1) * GLA_SLAB)

            def scan(step, carry):
                sf, sb = carry
                i, j = step, n - 1 - step
                snap_ref[i, srows, 0:dk] = sf.astype(BF16)
                snap_ref[j, srows, dk:2 * dk] = sb.astype(BF16)
                dec_f = dec_ref[pl.ds(pl.multiple_of(i * 8, 8), 8), 0:dk]
                dec_b = dec_ref[pl.ds(pl.multiple_of(j * 8, 8), 8), dk:2 * dk]
                sf = (sf.reshape(-1, 8, dk) * dec_f[None]).reshape(GLA_SLAB, dk) + upd_ref[i, srows, 0:dk]
                sb = (sb.reshape(-1, 8, dk) * dec_b[None]).reshape(GLA_SLAB, dk) + upd_ref[j, srows, dk:2 * dk]
                return sf, sb

            zero = jnp.zeros((GLA_SLAB, dk), F32)
            lax.fori_loop(0, n, scan, (zero, zero), unroll=2)

    def finish(size, group):
        def body(step, carry):
            ids = [step * group + u for u in range(group)]
            rows = [pl.ds(pl.multiple_of(i * size, size), size) for i in ids]
            inter = [_dot_nt(qe_ref[r, :], snap_ref[i]) for i, r in zip(ids, rows)]
            for r, o_inter in zip(rows, inter):
                o = acc_ref[r, :] + o_inter
                o_ref[r, :] = (_rms(o, gain_ref[...]) * sr_ref[r, :].astype(F32)).astype(BF16)
            return carry
        return body

    def chunk_totals(g_ref):
        return jnp.sum(g_ref[...].reshape(n_wide, w, dk), axis=1)
    mild = jnp.min(jnp.minimum(chunk_totals(gf_ref), chunk_totals(gb_ref))) >= -GLA_MILD_LOG2

    @pl.when(mild)
    def _():
        lax.fori_loop(0, n_wide // GLA_GROUP, chunks_mild_decay, 0)
        scan_states(n_wide)
        lax.fori_loop(0, n_wide // GLA_GROUP, finish(w, GLA_GROUP), 0)

    @pl.when(jnp.logical_not(mild))
    def _():
        lax.fori_loop(0, n_chunks, chunk_any_decay, 0)
        scan_states(n_chunks)
        lax.fori_loop(0, n_chunks, finish(c, 1), 0)


def _gla(gq, gk, gv, sr, lgf, lgb, gain, batch, seq):
    consts = _gla_constants()
    n_chunks = seq // GLA_C

    def head(width):
        return pl.BlockSpec((None, seq, width), lambda b, h: (b, 0, h))

    return pl.pallas_call(
        _gla_kernel,
        grid=(batch, GLA_HEADS),
        in_specs=[head(GLA_DK), head(GLA_DK), head(GLA_DV), head(GLA_DV), head(GLA_DK), head(GLA_DK),
                  pl.BlockSpec((None, 1, GLA_DV), lambda b, h: (h, 0, 0))]
                 + [_const_spec(cst.shape) for cst in consts],
        out_specs=head(GLA_DV),
        out_shape=jax.ShapeDtypeStruct((batch, seq, GLA_VAL_W), BF16),
        scratch_shapes=[pltpu.VMEM((seq, GLA_DV), F32),
                        pltpu.VMEM((seq, 2 * GLA_DK), BF16),
                        pltpu.VMEM((n_chunks, GLA_DV, 2 * GLA_DK), F32),
                        pltpu.VMEM((n_chunks, GLA_DV, 2 * GLA_DK), BF16),
                        pltpu.VMEM((n_chunks * 8, 2 * GLA_DK), F32)],
        compiler_params=_params(2),
        name="gla",
    )(gq, gk, gv, sr, lgf, lgb, gain, *consts)


def _attn_kernel(bound_ref, q_ref, k_ref, v_ref, o_ref, vext_ref):
    hd = ATT_HEAD_DIM
    tq = q_ref.shape[1]

    @pl.when(pl.program_id(2) == 0)
    def _():
        vext_ref[:, :hd] = v_ref[...]
        vext_ref[:, hd:] = jnp.ones((v_ref.shape[0], hd), BF16)

    def finish(p):
        o_ext = _dot(p, vext_ref[...])
        o = (o_ext[:, :hd] / o_ext[:, hd:]).astype(BF16)
        for g in range(ATT_GROUP):
            o_ref[:, g * hd:(g + 1) * hd] = o[g * tq:(g + 1) * tq]

    def scores():
        q = q_ref[...].reshape(ATT_GROUP * tq, hd)
        return _dot_nt(q, k_ref[...])

    bound = bound_ref[0]
    small = bound <= ATT_SHIFT_LIMIT_LOG2

    @pl.when(small)
    def _():
        finish(jnp.exp2(scores() - bound).astype(BF16))

    @pl.when(jnp.logical_not(small))
    def _():
        s = scores()
        finish(jnp.exp2(s - jnp.max(s, axis=-1, keepdims=True)).astype(BF16))


def _attn(bound, aq, ak, av, batch, seq):
    tq = min(ATT_Q_TILE, seq)
    nq = seq // tq
    qspec = pl.BlockSpec((ATT_GROUP, tq, ATT_HEAD_DIM), lambda b, h, i: (h, b * nq + i, 0))
    kvspec = pl.BlockSpec((None, seq, ATT_HEAD_DIM), lambda b, h, i: (b, 0, h))
    ospec = pl.BlockSpec((None, tq, ATT_GROUP * ATT_HEAD_DIM), lambda b, h, i: (b, i, h))
    return pl.pallas_call(
        _attn_kernel,
        grid=(batch, ATT_KV_HEADS, nq),
        in_specs=[pl.BlockSpec(memory_space=pltpu.SMEM), qspec, kvspec, kvspec],
        out_specs=ospec,
        out_shape=jax.ShapeDtypeStruct((batch, seq, ATT_Q_W), BF16),
        scratch_shapes=[pltpu.VMEM((seq, 2 * ATT_HEAD_DIM), BF16)],
        compiler_params=_params(3),
        name="attn",
    )(bound, aq, ak, av)


def _score_bound(q_gain, k_gain):
    return (jnp.max(jnp.abs(q_gain)) * jnp.max(jnp.abs(k_gain))
            * (1.02 * ATT_HEAD_DIM ** 0.5 * LOG2_E)).reshape(1).astype(F32)


def _merge_kernel(h_ref, oa_ref, ob_ref, sga_ref, sgb_ref, wa_ref, wb_ref, wo_ref, gpost_ref, o_ref):
    for rows in _row_parts(h_ref):
        ya = _dot(oa_ref[rows, :], wa_ref[...])
        yb = _dot(ob_ref[rows, :], wb_ref[...])
        merged = sga_ref[rows, :].astype(F32) * ya + sgb_ref[rows, :].astype(F32) * yb
        m = _dot(merged.astype(BF16), wo_ref[...])
        o_ref[rows, :] = h_ref[rows, :] + _rms(m, gpost_ref[...])


def _merge(h, oa, ob, sga, sgb, w_a, w_b, w_o, g_post):
    t = h.shape[0]
    tm = min(ROW_TILE * ROW_PARTS, t)
    row = pl.BlockSpec((tm, D_MODEL), lambda i: (i, 0))
    sq = _const_spec((D_MODEL, D_MODEL))
    return pl.pallas_call(
        _merge_kernel,
        grid=(t // tm,),
        in_specs=[row] * 5 + [sq, sq, sq, _const_spec((1, D_MODEL))],
        out_specs=row,
        out_shape=jax.ShapeDtypeStruct((t, D_MODEL), F32),
        compiler_params=_params(1),
        name="merge",
    )(h, oa, ob, sga, sgb, w_a, w_b, w_o, g_post)


def _split_w_in(w_in):
    gq, gk, gv, gr, za_f, za_b, aq, ak, av, ga, gb = jnp.split(w_in.astype(BF16), IN_OFFSETS, axis=-1)
    pad = jnp.zeros((D_MODEL, V7X_LANES - 2 * GLA_RANK), BF16)
    cat = lambda *parts: jnp.concatenate(parts, axis=-1)
    return cat(za_f, za_b, pad), cat(aq, ak), cat(ga, gb, gr), cat(gq, gk, gv, av)


def _pad_decay_up(up, first_row):
    out = jnp.zeros((V7X_LANES, GLA_KEY_W), BF16)
    return lax.dynamic_update_slice(out, up.astype(BF16), (first_row, 0))


def kernel(x, ffn1_pre_g, ffn1_w_in, ffn1_w_out, ffn1_post_g, mix_pre_g, w_in, gla_decay_up_f,
           gla_decay_bias_f, gla_decay_up_b, gla_decay_bias_b, gla_out_g, w_branch_a, att_q_norm_g,
           att_k_norm_g, w_branch_b, w_out, mix_post_g, ffn2_pre_g, ffn2_w_in, ffn2_w_out, ffn2_post_g):
    batch, seq, d = x.shape
    assert d == D_MODEL and seq % max(GLA_WIDE_C * GLA_GROUP, ATT_Q_TILE, GRID_W) == 0
    assert seq % PROJ_ROW_TILE == 0 or seq < PROJ_ROW_TILE
    depth = w_in.shape[0]
    t = batch * seq
    h = x.reshape(t, d)
    vec = lambda g: g.reshape(1, -1).astype(F32)
    for l in range(depth):
        h = _ffn(h, vec(ffn1_pre_g[l]), ffn1_w_in[l].astype(BF16), ffn1_w_out[l].astype(BF16),
                 vec(ffn1_post_g[l]))
        gq, gk, gv, sr, lgf, lgb, aq, ak, av, sga, sgb = _proj(
            h, seq, vec(mix_pre_g[l]), _split_w_in(w_in[l]),
            _pad_decay_up(gla_decay_up_f[l], 0), _pad_decay_up(gla_decay_up_b[l], GLA_RANK),
            vec(gla_decay_bias_f[l]), vec(gla_decay_bias_b[l]),
            vec(att_q_norm_g[l]), vec(att_k_norm_g[l]))
        per_seq = lambda a: a.reshape(batch, seq, a.shape[-1])
        oa = _gla(per_seq(gq), per_seq(gk), per_seq(gv), per_seq(sr), per_seq(lgf), per_seq(lgb),
                  gla_out_g[l].reshape(GLA_HEADS, 1, GLA_DV).astype(F32), batch, seq)
        ob = _attn(_score_bound(att_q_norm_g[l], att_k_norm_g[l]), aq, per_seq(ak), per_seq(av), batch, seq)
        h = _merge(h, oa.reshape(t, GLA_VAL_W), ob.reshape(t, ATT_Q_W), sga, sgb,
                   w_branch_a[l].astype(BF16), w_branch_b[l].astype(BF16), w_out[l].astype(BF16),
                   vec(mix_post_g[l]))
        h = _ffn(h, vec(ffn2_pre_g[l]), ffn2_w_in[l].astype(BF16), ffn2_w_out[l].astype(BF16),
                 vec(ffn2_post_g[l]))
    return h.reshape(batch, seq, d)
```

```python
import functools

import numpy as np
import jax
import jax.numpy as jnp
from jax import lax
from jax.experimental import pallas as pl
from jax.experimental.pallas import tpu as pltpu

F32 = jnp.float32
BF16 = jnp.bfloat16

D_MODEL = 1024
GRID_W = 64
D_FF = 2816
EPS = 1e-6
GLA_HEADS = 4
GLA_DK = 128
GLA_DV = 256
GLA_RANK = 16
GLA_TAU = 16.0
ATT_Q_HEADS = 8
ATT_KV_HEADS = 2
ATT_HEAD_DIM = 128
ATT_GROUP = ATT_Q_HEADS // ATT_KV_HEADS
ROPE_THETA = 10000.0
GLA_KEY_W = GLA_HEADS * GLA_DK
GLA_VAL_W = GLA_HEADS * GLA_DV
ATT_Q_W = ATT_Q_HEADS * ATT_HEAD_DIM
ATT_KV_W = ATT_KV_HEADS * ATT_HEAD_DIM
IN_SPLITS = (GLA_KEY_W, GLA_KEY_W, GLA_VAL_W, GLA_VAL_W, GLA_RANK, GLA_RANK,
             ATT_Q_W, ATT_KV_W, ATT_KV_W, D_MODEL, D_MODEL)
IN_OFFSETS = tuple(int(s) for s in np.cumsum(IN_SPLITS)[:-1])
LOG2_E = 1.4426950408889634

V7X_LANES = 128
V7X_VMEM_BYTES = 64 * 1024 * 1024
VMEM_LIMIT = V7X_VMEM_BYTES - 8 * 1024 * 1024

STEP_ROWS = 1024
FFN_ROWS = 256
MERGE_ROWS = 512
PROJ_ROW_TILE = 512
ATT_Q_TILE = 512
ATT_SHIFT_LIMIT_LOG2 = 60.0
GLA_C = 64
GLA_LEVELS = GLA_C.bit_length() - 1
GLA_SLAB = 64
GLA_WIDE_C = 256
GLA_MILD_LOG2 = 48.0
FF_CHUNKS = ((0, 1024), (1024, 1024), (2048, 768))


def _const_spec(shape):
    zeros = (0,) * len(shape)
    return pl.BlockSpec(shape, lambda *_: zeros, pipeline_mode=pl.Buffered(1))


def _params(n_axes):
    return pltpu.CompilerParams(dimension_semantics=("arbitrary",) * n_axes,
                                vmem_limit_bytes=VMEM_LIMIT)


def _rms(x, g):
    ms = jnp.mean(x * x, axis=-1, keepdims=True)
    return x * lax.rsqrt(ms + EPS) * g


def _dot(a, b):
    return jnp.dot(a, b, preferred_element_type=F32)


def _dot_nt(a, b):
    return lax.dot_general(a, b, (((1,), (1,)), ((), ())), preferred_element_type=F32)


def _dot_tn(a, b):
    return lax.dot_general(a, b, (((0,), (0,)), ((), ())), preferred_element_type=F32)


def _row_parts(ref, rows):
    rows = min(rows, ref.shape[0])
    return [slice(r, r + rows) for r in range(0, ref.shape[0], rows)]


def _ffn_kernel(x_ref, gpre_ref, win_ref, wout_ref, gpost_ref, o_ref):
    for rows in _row_parts(x_ref, FFN_ROWS):
        x = x_ref[rows, :]
        xn = _rms(x, gpre_ref[...]).astype(BF16)
        acc = None
        for c0, cw in FF_CHUNKS:
            gate = _dot(xn, win_ref[:, c0:c0 + cw])
            up = _dot(xn, win_ref[:, D_FF + c0:D_FF + c0 + cw])
            act = (gate * jax.nn.sigmoid(gate) * up).astype(BF16)
            part = _dot(act, wout_ref[c0:c0 + cw, :])
            acc = part if acc is None else acc + part
        o_ref[rows, :] = x + 0.5 * _rms(acc, gpost_ref[...])


def _ffn(x, g_pre, w_in, w_out, g_post):
    t = x.shape[0]
    tm = min(STEP_ROWS, t)
    row = pl.BlockSpec((tm, D_MODEL), lambda i: (i, 0))
    return pl.pallas_call(
        _ffn_kernel,
        grid=(t // tm,),
        in_specs=[row, _const_spec((1, D_MODEL)), _const_spec((D_MODEL, 2 * D_FF)),
                  _const_spec((D_FF, D_MODEL)), _const_spec((1, D_MODEL))],
        out_specs=row,
        out_shape=jax.ShapeDtypeStruct((t, D_MODEL), F32),
        compiler_params=_params(1),
        name="ffn",
    )(x, g_pre, w_in, w_out, g_post)


def _log_sigmoid(z):
    return jnp.minimum(z, 0.0) - jnp.log(1.0 + jnp.exp(-jnp.abs(z)))


def _rope(x, cos, sin_up, sin_dn):
    return (x * cos + pltpu.roll(x, V7X_LANES - 32, axis=1) * sin_up
            + pltpu.roll(x, 32, axis=1) * sin_dn)


def _proj_kernel(h_ref, gmix_ref, wz_ref, watt_ref, wgate_ref, wplain_ref, upf_ref, upb_ref,
                 bf_ref, bb_ref, qg_ref, kg_ref, cos_ref, sup_ref, sdn_ref,
                 gq_o, gk_o, gv_o, sr_o, lf_o, lb_o, aq_o, ak_o, av_o, sga_o, sgb_o, dmin_o):
    u = _rms(h_ref[...], gmix_ref[...]).astype(BF16)
    za = _dot(u, wz_ref[...]).astype(BF16)

    cos, sup, sdn = cos_ref[...], sup_ref[...], sdn_ref[...]
    att = _dot(u, watt_ref[...])
    qg = qg_ref[...] * (ATT_HEAD_DIM ** -0.5 * LOG2_E)
    for hd in range(ATT_Q_HEADS):
        sl = slice(hd * ATT_HEAD_DIM, (hd + 1) * ATT_HEAD_DIM)
        aq_o[hd] = _rope(_rms(att[:, sl], qg), cos, sup, sdn).astype(BF16)
    for hd in range(ATT_KV_HEADS):
        sl = slice(hd * ATT_HEAD_DIM, (hd + 1) * ATT_HEAD_DIM)
        ak_o[:, sl] = _rope(_rms(att[:, ATT_Q_W + sl.start:ATT_Q_W + sl.stop], kg_ref[...]),
                            cos, sup, sdn).astype(BF16)

    gate = _dot(u, wgate_ref[...])
    sga_o[...] = jax.nn.sigmoid(gate[:, 0:D_MODEL]).astype(BF16)
    sgb_o[...] = jax.nn.sigmoid(gate[:, D_MODEL:2 * D_MODEL]).astype(BF16)
    gr = gate[:, 2 * D_MODEL:]
    sr_o[...] = (gr * jax.nn.sigmoid(gr)).astype(BF16)
    lf = _log_sigmoid(_dot(za, upf_ref[...]) + bf_ref[...]) * (LOG2_E / GLA_TAU)
    lb = _log_sigmoid(_dot(za, upb_ref[...]) + bb_ref[...]) * (LOG2_E / GLA_TAU)
    lf_o[...] = lf
    lb_o[...] = lb
    totals = jnp.minimum(lf.reshape(-1, GLA_WIDE_C, GLA_KEY_W).sum(axis=1),
                         lb.reshape(-1, GLA_WIDE_C, GLA_KEY_W).sum(axis=1))
    worst = jnp.min(totals, axis=0, keepdims=True)
    worst = functools.reduce(jnp.minimum, [worst[:, l:l + V7X_LANES] for l in range(0, GLA_KEY_W, V7X_LANES)])
    dmin_o[...] = jnp.broadcast_to(worst, dmin_o.shape)

    plain = _dot(u, wplain_ref[...])
    gq_o[...] = (plain[:, 0:GLA_KEY_W] * (GLA_DK ** -0.5)).astype(BF16)
    gk_o[...] = plain[:, GLA_KEY_W:2 * GLA_KEY_W].astype(BF16)
    gv_o[...] = plain[:, 2 * GLA_KEY_W:2 * GLA_KEY_W + GLA_VAL_W].astype(BF16)
    av_o[...] = plain[:, 2 * GLA_KEY_W + GLA_VAL_W:].astype(BF16)


def _rope_tables(seq):
    pos = np.arange(seq)
    half = ATT_HEAD_DIM // 2
    inv_freq = ROPE_THETA ** (-np.arange(0, half, 2, dtype=np.float64) / half)
    ang_r = (pos // GRID_W)[:, None] * inv_freq
    ang_c = (pos % GRID_W)[:, None] * inv_freq
    zero = np.zeros_like(ang_r)
    cos = np.concatenate([np.cos(ang_r)] * 2 + [np.cos(ang_c)] * 2, axis=-1)
    sin_up = np.concatenate([-np.sin(ang_r), zero, -np.sin(ang_c), zero], axis=-1)
    sin_dn = np.concatenate([zero, np.sin(ang_r), zero, np.sin(ang_c)], axis=-1)
    return tuple(jnp.asarray(t, F32) for t in (cos, sin_up, sin_dn))


def _proj(h, seq, g_mix, w_parts, up_f, up_b, bias_f, bias_b, q_gain, k_gain):
    t = h.shape[0]
    tm = min(PROJ_ROW_TILE, seq)
    tiles_per_seq = seq // tm
    cos, sin_up, sin_dn = _rope_tables(seq)

    def row(width):
        return pl.BlockSpec((tm, width), lambda i: (i, 0))

    table = pl.BlockSpec((tm, ATT_HEAD_DIM), lambda i: (i % tiles_per_seq, 0))
    consts_a = (g_mix, *w_parts, up_f, up_b, bias_f, bias_b, q_gain, k_gain)
    out_widths = (GLA_KEY_W, GLA_KEY_W, GLA_VAL_W, GLA_VAL_W, GLA_KEY_W, GLA_KEY_W,
                  None, ATT_KV_W, ATT_KV_W, D_MODEL, D_MODEL)
    out_dtypes = (BF16, BF16, BF16, BF16, F32, F32, BF16, BF16, BF16, BF16, BF16)
    q_heads = pl.BlockSpec((ATT_Q_HEADS, tm, ATT_HEAD_DIM), lambda i: (0, i, 0))
    return pl.pallas_call(
        _proj_kernel,
        grid=(t // tm,),
        in_specs=[row(D_MODEL)] + [_const_spec(c.shape) for c in consts_a] + [table] * 3,
        out_specs=[q_heads if w is None else row(w) for w in out_widths]
                  + [pl.BlockSpec((8, V7X_LANES), lambda i: (i, 0))],
        out_shape=[jax.ShapeDtypeStruct((ATT_Q_HEADS, t, ATT_HEAD_DIM) if w is None else (t, w), d)
                   for w, d in zip(out_widths, out_dtypes)]
                  + [jax.ShapeDtypeStruct((8 * (t // tm), V7X_LANES), F32)],
        compiler_params=_params(1),
        name="proj",
    )(h, *consts_a, cos, sin_up, sin_dn)


def _gla_constants():
    c, nl = GLA_C, GLA_LEVELS
    i = np.arange(c)[:, None]
    m = np.arange(c)[None, :]
    sum_f = [m <= i, m > i]
    sum_b = [m >= i, m < i]
    role_f, role_b, mask_f, mask_b = [], [], [], []
    for lvl in range(nl):
        s = c >> (lvl + 1)
        start = (i // (2 * s)) * 2 * s
        mid = start + s
        upper = (i % (2 * s)) >= s
        sum_f.append(np.where(upper, (m >= mid) & (m <= i), (m > i) & (m < mid)))
        sum_b.append(np.where(upper, (m >= mid) & (m < i), (m >= i) & (m < mid)))
        same = (i // (2 * s)) == (m // (2 * s))
        m_upper = (m % (2 * s)) >= s
        role_f.append(np.broadcast_to(upper, (c, V7X_LANES)))
        role_b.append(np.broadcast_to(~upper, (c, V7X_LANES)))
        mask_f.append(same & upper & ~m_upper)
        mask_b.append(same & ~upper & m_upper)
    sum_f = np.concatenate(sum_f, axis=0).astype(np.float32)
    sum_b = np.concatenate(sum_b, axis=0).astype(np.float32)
    sum_f = np.concatenate([sum_f, sum_f], axis=1)
    sum_b = np.concatenate([sum_b, sum_b], axis=1)
    roles = np.stack(role_f + role_b).astype(np.float32)
    masks = np.stack(mask_f + mask_b + [i == m]).astype(np.float32)
    return jnp.asarray(sum_f, BF16), jnp.asarray(sum_b, BF16), jnp.asarray(roles), jnp.asarray(masks)


def _gla_wide_prefix_matrix():
    wide = np.arange(GLA_WIDE_C)
    tri = (wide[None, :] <= wide[:, None]).astype(np.float32)
    return jnp.asarray(np.concatenate([tri, tri], axis=1), BF16)


def _decay_sums(g, sum_ref, n_rows):
    g_hi = g.astype(BF16)
    g_lo = (g - g_hi.astype(F32)).astype(BF16)
    return _dot(sum_ref[0:n_rows, :], jnp.concatenate([g_hi, g_lo], axis=0))


def _gla_scratch(n_chunks, rows):
    return [pltpu.VMEM((rows, GLA_DV), F32),
            pltpu.VMEM((rows, 2 * GLA_DK), BF16),
            pltpu.VMEM((n_chunks, GLA_DV, 2 * GLA_DK), F32),
            pltpu.VMEM((n_chunks, GLA_DV, 2 * GLA_DK), BF16),
            pltpu.VMEM((n_chunks * 8, 2 * GLA_DK), F32)]


def _gla_store_edges(scratch, i, rows, v, qe_f, qe_b, kd_f, kd_b, total_f, total_b):
    _, qe_ref, upd_ref, _, dec_ref = scratch
    qe_ref[rows, :] = jnp.concatenate([qe_f, qe_b], axis=1).astype(BF16)
    kd = jnp.concatenate([kd_f, kd_b], axis=1).astype(BF16)
    upd_ref[i] = _dot_tn(v, kd)
    total = jnp.concatenate([total_f, total_b], axis=1)
    dec_ref[pl.ds(pl.multiple_of(i * 8, 8), 8), :] = jnp.broadcast_to(total, (8, 2 * GLA_DK))


def _gla_scan_step(scratch, srows, i, j, sf, sb):
    _, _, upd_ref, snap_ref, dec_ref = scratch
    dk = GLA_DK
    snap_ref[i, srows, 0:dk] = sf.astype(BF16)
    snap_ref[j, srows, dk:2 * dk] = sb.astype(BF16)
    dec_f = dec_ref[pl.ds(pl.multiple_of(i * 8, 8), 8), 0:dk]
    dec_b = dec_ref[pl.ds(pl.multiple_of(j * 8, 8), 8), dk:2 * dk]
    sf = (sf.reshape(-1, 8, dk) * dec_f[None]).reshape(GLA_SLAB, dk) + upd_ref[i, srows, 0:dk]
    sb = (sb.reshape(-1, 8, dk) * dec_b[None]).reshape(GLA_SLAB, dk) + upd_ref[j, srows, dk:2 * dk]
    return sf, sb


def _gla_finish(scratch, rows, o_inter, gain_ref, sr_ref, o_ref):
    o = scratch[0][rows, :] + o_inter
    o_ref[rows, :] = (_rms(o, gain_ref[...]) * sr_ref[rows, :].astype(F32)).astype(BF16)


def _gla_kernel(q_ref, k_ref, v_ref, sr_ref, gf_ref, gb_ref, gain_ref,
                sumf_ref, sumb_ref, role_ref, mask_ref,
                o_ref, *scratch):
    c, nl, dv = GLA_C, GLA_LEVELS, GLA_DV
    n_chunks = q_ref.shape[0] // c
    acc_ref, qe_ref, _, snap_ref, _ = scratch

    def chunk_any_decay(i, carry):
        rows = pl.ds(pl.multiple_of(i * c, c), c)
        q, k, v = q_ref[rows, :], k_ref[rows, :], v_ref[rows, :]
        qf, kf = q.astype(F32), k.astype(F32)
        ef = jnp.exp2(_decay_sums(gf_ref[rows, :], sumf_ref, (2 + nl) * c))
        eb = jnp.exp2(_decay_sums(gb_ref[rows, :], sumb_ref, (2 + nl) * c))
        a = _dot_nt(q, k) * mask_ref[2 * nl]
        for lvl in range(nl):
            blk = slice((2 + lvl) * c, (3 + lvl) * c)
            xf = (jnp.where(role_ref[lvl] > 0.5, qf, kf) * ef[blk]).astype(BF16)
            a = a + _dot_nt(xf, xf) * mask_ref[lvl]
            xb = (jnp.where(role_ref[nl + lvl] > 0.5, qf, kf) * eb[blk]).astype(BF16)
            a = a + _dot_nt(xb, xb) * mask_ref[nl + lvl]
        acc_ref[rows, :] = _dot(a.astype(BF16), v)
        _gla_store_edges(scratch, i, rows, v, qf * ef[0:c], qf * eb[0:c], kf * ef[c:2 * c], kf * eb[c:2 * c],
                         ef[c - 1:c], eb[0:1])
        return carry

    lax.fori_loop(0, n_chunks, chunk_any_decay, 0)

    for slab in range(dv // GLA_SLAB):
        srows = slice(slab * GLA_SLAB, (slab + 1) * GLA_SLAB)
        zero = jnp.zeros((GLA_SLAB, GLA_DK), F32)
        lax.fori_loop(0, n_chunks,
                      lambda step, carry: _gla_scan_step(scratch, srows, step, n_chunks - 1 - step, *carry),
                      (zero, zero), unroll=2)

    def finish(i, carry):
        rows = pl.ds(pl.multiple_of(i * c, c), c)
        _gla_finish(scratch, rows, _dot_nt(qe_ref[rows, :], snap_ref[i]), gain_ref, sr_ref, o_ref)
        return carry

    lax.fori_loop(0, n_chunks, finish, 0)


def _gla_head_spec(seq, width):
    return pl.BlockSpec((None, seq, width), lambda b, h: (b, 0, h))


def _gla(gq, gk, gv, sr, lgf, lgb, gain, batch, seq):
    consts = _gla_constants()
    head = functools.partial(_gla_head_spec, seq)
    return pl.pallas_call(
        _gla_kernel,
        grid=(batch, GLA_HEADS),
        in_specs=[head(GLA_DK), head(GLA_DK), head(GLA_DV), head(GLA_DV), head(GLA_DK), head(GLA_DK),
                  pl.BlockSpec((None, 1, GLA_DV), lambda b, h: (h, 0, 0))]
                 + [_const_spec(cst.shape) for cst in consts],
        out_specs=head(GLA_DV),
        out_shape=jax.ShapeDtypeStruct((batch, seq, GLA_VAL_W), BF16),
        scratch_shapes=_gla_scratch(seq // GLA_C, seq),
        compiler_params=_params(2),
        name="gla",
    )(gq, gk, gv, sr, lgf, lgb, gain, *consts)


def _attn_kernel(bound_ref, q_ref, k_ref, v_ref, o_ref, vext_ref):
    hd = ATT_HEAD_DIM
    tq = q_ref.shape[1]

    @pl.when(pl.program_id(2) == 0)
    def _():
        vext_ref[:, :hd] = v_ref[...]
        vext_ref[:, hd:] = jnp.ones((v_ref.shape[0], hd), BF16)

    def finish(p):
        o_ext = _dot(p, vext_ref[...])
        o = (o_ext[:, :hd] / o_ext[:, hd:]).astype(BF16)
        for g in range(ATT_GROUP):
            o_ref[:, g * hd:(g + 1) * hd] = o[g * tq:(g + 1) * tq]

    def scores():
        q = q_ref[...].reshape(ATT_GROUP * tq, hd)
        return _dot_nt(q, k_ref[...])

    bound = bound_ref[0]
    small = bound <= ATT_SHIFT_LIMIT_LOG2

    @pl.when(small)
    def _():
        finish(jnp.exp2(scores() - bound).astype(BF16))

    @pl.when(jnp.logical_not(small))
    def _():
        s = scores()
        finish(jnp.exp2(s - jnp.max(s, axis=-1, keepdims=True)).astype(BF16))


def _attn(bound, aq, ak, av, batch, seq):
    tq = min(ATT_Q_TILE, seq)
    nq = seq // tq
    qspec = pl.BlockSpec((ATT_GROUP, tq, ATT_HEAD_DIM), lambda b, h, i: (h, b * nq + i, 0))
    kvspec = pl.BlockSpec((None, seq, ATT_HEAD_DIM), lambda b, h, i: (b, 0, h))
    ospec = pl.BlockSpec((None, tq, ATT_GROUP * ATT_HEAD_DIM), lambda b, h, i: (b, i, h))
    return pl.pallas_call(
        _attn_kernel,
        grid=(batch, ATT_KV_HEADS, nq),
        in_specs=[pl.BlockSpec(memory_space=pltpu.SMEM), qspec, kvspec, kvspec],
        out_specs=ospec,
        out_shape=jax.ShapeDtypeStruct((batch, seq, ATT_Q_W), BF16),
        scratch_shapes=[pltpu.VMEM((seq, 2 * ATT_HEAD_DIM), BF16)],
        compiler_params=_params(3),
        name="attn",
    )(bound, aq, ak, av)


def _score_bound(q_gain, k_gain):
    return (jnp.max(jnp.abs(q_gain)) * jnp.max(jnp.abs(k_gain))
            * (1.02 * ATT_HEAD_DIM ** 0.5 * LOG2_E)).reshape(1).astype(F32)


class _GlaMild:
    def __init__(self, q_ref, k_ref, v_ref, sr_ref, gf_ref, gb_ref, gain_ref, tri_ref, o_ref, scratch):
        self.refs = (q_ref, k_ref, v_ref, sr_ref, gf_ref, gb_ref, gain_ref, tri_ref, o_ref)
        self.scratch = scratch
        w = GLA_WIDE_C
        self.n = q_ref.shape[0] // w
        self.rows = [slice(i * w, (i + 1) * w) for i in range(self.n)]

    def prefix_sums(self):
        _, _, _, _, gf_ref, gb_ref, _, tri_ref, _ = self.refs
        self.g_b = [gb_ref[r, :] for r in self.rows]
        self.prefix = [_decay_sums(jnp.concatenate([gf_ref[r, :], g], axis=1), tri_ref, GLA_WIDE_C)
                       for r, g in zip(self.rows, self.g_b)]

    def scores(self):
        q_ref, k_ref = self.refs[0:2]
        w, dk = GLA_WIDE_C, GLA_DK
        self.score, self.edge = [], []
        for r, g, p in zip(self.rows, self.g_b, self.prefix):
            qf, kf = q_ref[r, :].astype(F32), k_ref[r, :].astype(F32)
            bf, pb = p[:, 0:dk], p[:, dk:2 * dk]
            bb = (pb[w - 1:w] - pb) + g
            mid_f, mid_b = bf[w // 2 - 1:w // 2], bb[w // 2:w // 2 + 1]
            rf, rb = bf - mid_f, bb - mid_b
            q_f, k_f = qf * jnp.exp2(rf), kf * jnp.exp2(-rf)
            q_b, k_b = qf * jnp.exp2(rb), kf * jnp.exp2(-rb)
            self.score.append((_dot_nt(q_f.astype(BF16), k_f.astype(BF16)),
                               _dot_nt(q_b.astype(BF16), k_b.astype(BF16))))
            self.edge.append((q_f * jnp.exp2(mid_f), q_b * jnp.exp2(mid_b),
                              k_f * jnp.exp2(bf[w - 1:w] - mid_f), k_b * jnp.exp2(bb[0:1] - mid_b),
                              jnp.exp2(bf[w - 1:w]), jnp.exp2(bb[0:1])))

    def intra(self):
        v_ref, w = self.refs[2], GLA_WIDE_C
        lower = lax.broadcasted_iota(jnp.int32, (w, w), 0) >= lax.broadcasted_iota(jnp.int32, (w, w), 1)
        for r, (s_f, s_b) in zip(self.rows, self.score):
            self.scratch[0][r, :] = _dot(jnp.where(lower, s_f, s_b).astype(BF16), v_ref[r, :])

    def edges(self):
        v_ref = self.refs[2]
        for i, (r, parts) in enumerate(zip(self.rows, self.edge)):
            _gla_store_edges(self.scratch, i, r, v_ref[r, :], *parts)

    def scan(self):
        for slab in range(GLA_DV // GLA_SLAB):
            srows = slice(slab * GLA_SLAB, (slab + 1) * GLA_SLAB)
            sf = sb = jnp.zeros((GLA_SLAB, GLA_DK), F32)
            for step in range(self.n):
                sf, sb = _gla_scan_step(self.scratch, srows, step, self.n - 1 - step, sf, sb)

    def inter(self):
        qe_ref, snap_ref = self.scratch[1], self.scratch[3]
        self.o_inter = [_dot_nt(qe_ref[r, :], snap_ref[i]) for i, r in enumerate(self.rows)]

    def finish(self):
        sr_ref, gain_ref, o_ref = self.refs[3], self.refs[6], self.refs[8]
        for r, o_inter in zip(self.rows, self.o_inter):
            _gla_finish(self.scratch, r, o_inter, gain_ref, sr_ref, o_ref)


def _mixers_mild_kernel(bound_ref, gq_ref, gk_ref, gv_ref, sr_ref, gf_ref, gb_ref, gain_ref, tri_ref,
                        aq_ref, ak_ref, av_ref,
                        oa_ref, ob_ref, vext_ref, *scratch):
    hd, tq = ATT_HEAD_DIM, min(ATT_Q_TILE, aq_ref.shape[1])
    bound = bound_ref[0]
    vext_ref[:, :hd] = av_ref[...]
    vext_ref[:, hd:] = jnp.ones((av_ref.shape[0], hd), BF16)

    def attention_unit(tile, g):
        rows = slice(tile * tq, (tile + 1) * tq)
        p = jnp.exp2(_dot_nt(aq_ref[g, rows, :], ak_ref[...]) - bound).astype(BF16)
        o_ext = _dot(p, vext_ref[...])
        ob_ref[rows, g * hd:(g + 1) * hd] = (o_ext[:, :hd] / o_ext[:, hd:]).astype(BF16)

    units = [(tile, g) for tile in range(aq_ref.shape[1] // tq) for g in range(ATT_GROUP)]
    gla = _GlaMild(gq_ref, gk_ref, gv_ref, sr_ref, gf_ref, gb_ref, gain_ref, tri_ref, oa_ref, scratch)
    stages = [gla.prefix_sums, gla.scores, gla.intra, gla.edges, gla.scan, gla.inter, gla.finish]
    per_stage = -(-len(units) // (len(stages) + 1))
    for stage in stages:
        for unit in units[:per_stage]:
            attention_unit(*unit)
        units = units[per_stage:]
        stage()
    for unit in units:
        attention_unit(*unit)


def _mixers_mild(bound, gq, gk, gv, sr, lgf, lgb, gain, aq, ak, av, batch, seq):
    share = GLA_HEADS // ATT_KV_HEADS
    rows = seq // share
    head = functools.partial(_gla_head_spec, seq)
    qspec = pl.BlockSpec((ATT_GROUP, rows, ATT_HEAD_DIM), lambda b, h: (h // share, b * share + h % share, 0))
    kvspec = pl.BlockSpec((None, seq, ATT_HEAD_DIM), lambda b, h: (b, 0, h // share))
    obspec = pl.BlockSpec((None, rows, ATT_GROUP * ATT_HEAD_DIM), lambda b, h: (b, h % share, h // share))
    tri = _gla_wide_prefix_matrix()
    return pl.pallas_call(
        _mixers_mild_kernel,
        grid=(batch, GLA_HEADS),
        in_specs=[pl.BlockSpec(memory_space=pltpu.SMEM),
                  head(GLA_DK), head(GLA_DK), head(GLA_DV), head(GLA_DV), head(GLA_DK), head(GLA_DK),
                  pl.BlockSpec((None, 1, GLA_DV), lambda b, h: (h, 0, 0)), _const_spec(tri.shape),
                  qspec, kvspec, kvspec],
        out_specs=[head(GLA_DV), obspec],
        out_shape=[jax.ShapeDtypeStruct((batch, seq, GLA_VAL_W), BF16),
                   jax.ShapeDtypeStruct((batch, seq, ATT_Q_W), BF16)],
        scratch_shapes=[pltpu.VMEM((seq, 2 * ATT_HEAD_DIM), BF16)] + _gla_scratch(seq // GLA_WIDE_C, seq),
        compiler_params=_params(2),
        name="mixers_mild",
    )(bound, gq, gk, gv, sr, lgf, lgb, gain, tri, aq, ak, av)


def _merge_kernel(h_ref, oa_ref, ob_ref, sga_ref, sgb_ref, wa_ref, wb_ref, wo_ref, gpost_ref, o_ref):
    for rows in _row_parts(h_ref, MERGE_ROWS):
        ya = _dot(oa_ref[rows, :], wa_ref[...])
        yb = _dot(ob_ref[rows, :], wb_ref[...])
        merged = sga_ref[rows, :].astype(F32) * ya + sgb_ref[rows, :].astype(F32) * yb
        m = _dot(merged.astype(BF16), wo_ref[...])
        o_ref[rows, :] = h_ref[rows, :] + _rms(m, gpost_ref[...])


def _merge(h, oa, ob, sga, sgb, w_a, w_b, w_o, g_post):
    t = h.shape[0]
    tm = min(STEP_ROWS, t)
    row = pl.BlockSpec((tm, D_MODEL), lambda i: (i, 0))
    sq = _const_spec((D_MODEL, D_MODEL))
    return pl.pallas_call(
        _merge_kernel,
        grid=(t // tm,),
        in_specs=[row] * 5 + [sq, sq, sq, _const_spec((1, D_MODEL))],
        out_specs=row,
        out_shape=jax.ShapeDtypeStruct((t, D_MODEL), F32),
        compiler_params=_params(1),
        name="merge",
    )(h, oa, ob, sga, sgb, w_a, w_b, w_o, g_post)


def _split_w_in(w_in):
    gq, gk, gv, gr, za_f, za_b, aq, ak, av, ga, gb = jnp.split(w_in.astype(BF16), IN_OFFSETS, axis=-1)
    pad = jnp.zeros((D_MODEL, V7X_LANES - 2 * GLA_RANK), BF16)
    cat = lambda *parts: jnp.concatenate(parts, axis=-1)
    return cat(za_f, za_b, pad), cat(aq, ak), cat(ga, gb, gr), cat(gq, gk, gv, av)


def _pad_decay_up(up, first_row):
    out = jnp.zeros((V7X_LANES, GLA_KEY_W), BF16)
    return lax.dynamic_update_slice(out, up.astype(BF16), (first_row, 0))


def kernel(x, ffn1_pre_g, ffn1_w_in, ffn1_w_out, ffn1_post_g, mix_pre_g, w_in, gla_decay_up_f,
           gla_decay_bias_f, gla_decay_up_b, gla_decay_bias_b, gla_out_g, w_branch_a, att_q_norm_g,
           att_k_norm_g, w_branch_b, w_out, mix_post_g, ffn2_pre_g, ffn2_w_in, ffn2_w_out, ffn2_post_g):
    batch, seq, d = x.shape
    assert d == D_MODEL and seq % max(GLA_WIDE_C, GRID_W) == 0
    assert seq % ATT_Q_TILE == 0 or seq < ATT_Q_TILE
    assert seq % PROJ_ROW_TILE == 0 or seq < PROJ_ROW_TILE
    depth = w_in.shape[0]
    t = batch * seq
    h = x.reshape(t, d)
    vec = lambda g: g.reshape(1, -1).astype(F32)
    for l in range(depth):
        h = _ffn(h, vec(ffn1_pre_g[l]), ffn1_w_in[l].astype(BF16), ffn1_w_out[l].astype(BF16),
                 vec(ffn1_post_g[l]))
        gq, gk, gv, sr, lgf, lgb, aq, ak, av, sga, sgb, decay_min = _proj(
            h, seq, vec(mix_pre_g[l]), _split_w_in(w_in[l]),
            _pad_decay_up(gla_decay_up_f[l], 0), _pad_decay_up(gla_decay_up_b[l], GLA_RANK),
            vec(gla_decay_bias_f[l]), vec(gla_decay_bias_b[l]),
            vec(att_q_norm_g[l]), vec(att_k_norm_g[l]))
        per_seq = lambda a: a.reshape(batch, seq, a.shape[-1])
        gla_args = (per_seq(gq), per_seq(gk), per_seq(gv), per_seq(sr), per_seq(lgf), per_seq(lgb),
                    gla_out_g[l].reshape(GLA_HEADS, 1, GLA_DV).astype(F32))
        att_args = (aq, per_seq(ak), per_seq(av), batch, seq)
        bound = _score_bound(att_q_norm_g[l], att_k_norm_g[l])
        mild = jnp.logical_and(jnp.min(decay_min) >= -GLA_MILD_LOG2, bound[0] <= ATT_SHIFT_LIMIT_LOG2)
        oa, ob = lax.cond(mild,
                          lambda: tuple(_mixers_mild(bound, *gla_args, *att_args)),
                          lambda: (_gla(*gla_args, batch, seq), _attn(bound, *att_args)))
        h = _merge(h, oa.reshape(t, GLA_VAL_W), ob.reshape(t, ATT_Q_W), sga, sgb,
                   w_branch_a[l].astype(BF16), w_branch_b[l].astype(BF16), w_out[l].astype(BF16),
                   vec(mix_post_g[l]))
        h = _ffn(h, vec(ffn2_pre_g[l]), ffn2_w_in[l].astype(BF16), ffn2_w_out[l].astype(BF16),
                 vec(ffn2_post_g[l]))
    return h.reshape(batch, seq, d)
```

```python
import functools

import numpy as np
import jax
import jax.numpy as jnp
from jax import lax
from jax.experimental import pallas as pl
from jax.experimental.pallas import tpu as pltpu

F32 = jnp.float32
BF16 = jnp.bfloat16

D_MODEL = 1024
GRID_W = 64
D_FF = 2816
EPS = 1e-6
GLA_HEADS = 4
GLA_DK = 128
GLA_DV = 256
GLA_RANK = 16
GLA_TAU = 16.0
ATT_Q_HEADS = 8
ATT_KV_HEADS = 2
ATT_HEAD_DIM = 128
ATT_GROUP = ATT_Q_HEADS // ATT_KV_HEADS
ROPE_THETA = 10000.0
GLA_KEY_W = GLA_HEADS * GLA_DK
GLA_VAL_W = GLA_HEADS * GLA_DV
ATT_Q_W = ATT_Q_HEADS * ATT_HEAD_DIM
ATT_KV_W = ATT_KV_HEADS * ATT_HEAD_DIM
IN_SPLITS = (GLA_KEY_W, GLA_KEY_W, GLA_VAL_W, GLA_VAL_W, GLA_RANK, GLA_RANK,
             ATT_Q_W, ATT_KV_W, ATT_KV_W, D_MODEL, D_MODEL)
IN_OFFSETS = tuple(int(s) for s in np.cumsum(IN_SPLITS)[:-1])
LOG2_E = 1.4426950408889634

V7X_LANES = 128
V7X_VMEM_BYTES = 64 * 1024 * 1024
VMEM_LIMIT = V7X_VMEM_BYTES - 8 * 1024 * 1024

STEP_ROWS = 1024
FFN_ROWS = 256
MERGE_ROWS = 512
PROJ_ROW_TILE = 512
ATT_Q_TILE = 512
ATT_SHIFT_LIMIT_LOG2 = 60.0
GLA_C = 64
GLA_LEVELS = GLA_C.bit_length() - 1
GLA_SLAB = 64
GLA_WIDE_C = 256
GLA_MILD_LOG2 = 48.0
FF_CHUNKS = ((0, 1024), (1024, 1024), (2048, 768))


def _const_spec(shape):
    zeros = (0,) * len(shape)
    return pl.BlockSpec(shape, lambda *_: zeros, pipeline_mode=pl.Buffered(1))


def _params(n_axes):
    return pltpu.CompilerParams(dimension_semantics=("arbitrary",) * n_axes,
                                vmem_limit_bytes=VMEM_LIMIT)


def _rms(x, g):
    ms = jnp.mean(x * x, axis=-1, keepdims=True)
    return x * lax.rsqrt(ms + EPS) * g


def _dot(a, b):
    return jnp.dot(a, b, preferred_element_type=F32)


def _dot_nt(a, b):
    return lax.dot_general(a, b, (((1,), (1,)), ((), ())), preferred_element_type=F32)


def _dot_tn(a, b):
    return lax.dot_general(a, b, (((0,), (0,)), ((), ())), preferred_element_type=F32)


def _row_parts(ref, rows):
    rows = min(rows, ref.shape[0])
    return [slice(r, r + rows) for r in range(0, ref.shape[0], rows)]


def _ffn_kernel(x_ref, gpre_ref, win_ref, wout_ref, gpost_ref, o_ref):
    for rows in _row_parts(x_ref, FFN_ROWS):
        x = x_ref[rows, :]
        xn = _rms(x, gpre_ref[...]).astype(BF16)
        acc = None
        for c0, cw in FF_CHUNKS:
            gate = _dot(xn, win_ref[:, c0:c0 + cw])
            up = _dot(xn, win_ref[:, D_FF + c0:D_FF + c0 + cw])
            act = (gate * jax.nn.sigmoid(gate) * up).astype(BF16)
            part = _dot(act, wout_ref[c0:c0 + cw, :])
            acc = part if acc is None else acc + part
        o_ref[rows, :] = x + 0.5 * _rms(acc, gpost_ref[...])


def _ffn(x, g_pre, w_in, w_out, g_post):
    t = x.shape[0]
    tm = min(STEP_ROWS, t)
    row = pl.BlockSpec((tm, D_MODEL), lambda i: (i, 0))
    return pl.pallas_call(
        _ffn_kernel,
        grid=(t // tm,),
        in_specs=[row, _const_spec((1, D_MODEL)), _const_spec((D_MODEL, 2 * D_FF)),
                  _const_spec((D_FF, D_MODEL)), _const_spec((1, D_MODEL))],
        out_specs=row,
        out_shape=jax.ShapeDtypeStruct((t, D_MODEL), F32),
        compiler_params=_params(1),
        name="ffn",
    )(x, g_pre, w_in, w_out, g_post)


def _log_sigmoid(z):
    return jnp.minimum(z, 0.0) - jnp.log(1.0 + jnp.exp(-jnp.abs(z)))


def _rope(x, cos, sin_up, sin_dn):
    return (x * cos + pltpu.roll(x, V7X_LANES - 32, axis=1) * sin_up
            + pltpu.roll(x, 32, axis=1) * sin_dn)


def _proj_kernel(h_ref, gmix_ref, wz_ref, watt_ref, wgate_ref, wplain_ref, upf_ref, upb_ref,
                 bf_ref, bb_ref, qg_ref, kg_ref, cos_ref, sup_ref, sdn_ref,
                 gq_o, gk_o, gv_o, sr_o, lf_o, lb_o, aq_o, ak_o, av_o, sga_o, sgb_o, dmin_o):
    u = _rms(h_ref[...], gmix_ref[...]).astype(BF16)
    za = _dot(u, wz_ref[...]).astype(BF16)

    cos, sup, sdn = cos_ref[...], sup_ref[...], sdn_ref[...]
    att = _dot(u, watt_ref[...])
    qg = qg_ref[...] * (ATT_HEAD_DIM ** -0.5 * LOG2_E)
    for hd in range(ATT_Q_HEADS):
        sl = slice(hd * ATT_HEAD_DIM, (hd + 1) * ATT_HEAD_DIM)
        aq_o[hd] = _rope(_rms(att[:, sl], qg), cos, sup, sdn).astype(BF16)
    for hd in range(ATT_KV_HEADS):
        sl = slice(hd * ATT_HEAD_DIM, (hd + 1) * ATT_HEAD_DIM)
        ak_o[:, sl] = _rope(_rms(att[:, ATT_Q_W + sl.start:ATT_Q_W + sl.stop], kg_ref[...]),
                            cos, sup, sdn).astype(BF16)

    gate = _dot(u, wgate_ref[...])
    sga_o[...] = jax.nn.sigmoid(gate[:, 0:D_MODEL]).astype(BF16)
    sgb_o[...] = jax.nn.sigmoid(gate[:, D_MODEL:2 * D_MODEL]).astype(BF16)
    gr = gate[:, 2 * D_MODEL:]
    sr_o[...] = (gr * jax.nn.sigmoid(gr)).astype(BF16)
    lf = _log_sigmoid(_dot(za, upf_ref[...]) + bf_ref[...]) * (LOG2_E / GLA_TAU)
    lb = _log_sigmoid(_dot(za, upb_ref[...]) + bb_ref[...]) * (LOG2_E / GLA_TAU)
    lf_o[...] = lf
    lb_o[...] = lb
    totals = jnp.minimum(lf.reshape(-1, GLA_WIDE_C, GLA_KEY_W).sum(axis=1),
                         lb.reshape(-1, GLA_WIDE_C, GLA_KEY_W).sum(axis=1))
    worst = jnp.min(totals, axis=0, keepdims=True)
    worst = functools.reduce(jnp.minimum, [worst[:, l:l + V7X_LANES] for l in range(0, GLA_KEY_W, V7X_LANES)])
    dmin_o[...] = jnp.broadcast_to(worst, dmin_o.shape)

    plain = _dot(u, wplain_ref[...])
    gq_o[...] = (plain[:, 0:GLA_KEY_W] * (GLA_DK ** -0.5)).astype(BF16)
    gk_o[...] = plain[:, GLA_KEY_W:2 * GLA_KEY_W].astype(BF16)
    gv_o[...] = plain[:, 2 * GLA_KEY_W:2 * GLA_KEY_W + GLA_VAL_W].astype(BF16)
    av_o[...] = plain[:, 2 * GLA_KEY_W + GLA_VAL_W:].astype(BF16)


def _rope_tables(seq):
    pos = np.arange(seq)
    half = ATT_HEAD_DIM // 2
    inv_freq = ROPE_THETA ** (-np.arange(0, half, 2, dtype=np.float64) / half)
    ang_r = (pos // GRID_W)[:, None] * inv_freq
    ang_c = (pos % GRID_W)[:, None] * inv_freq
    zero = np.zeros_like(ang_r)
    cos = np.concatenate([np.cos(ang_r)] * 2 + [np.cos(ang_c)] * 2, axis=-1)
    sin_up = np.concatenate([-np.sin(ang_r), zero, -np.sin(ang_c), zero], axis=-1)
    sin_dn = np.concatenate([zero, np.sin(ang_r), zero, np.sin(ang_c)], axis=-1)
    return tuple(jnp.asarray(t, F32) for t in (cos, sin_up, sin_dn))


def _proj(h, seq, g_mix, w_parts, up_f, up_b, bias_f, bias_b, q_gain, k_gain):
    t = h.shape[0]
    tm = min(PROJ_ROW_TILE, seq)
    tiles_per_seq = seq // tm
    cos, sin_up, sin_dn = _rope_tables(seq)

    def row(width):
        return pl.BlockSpec((tm, width), lambda i: (i, 0))

    table = pl.BlockSpec((tm, ATT_HEAD_DIM), lambda i: (i % tiles_per_seq, 0))
    consts_a = (g_mix, *w_parts, up_f, up_b, bias_f, bias_b, q_gain, k_gain)
    out_widths = (GLA_KEY_W, GLA_KEY_W, GLA_VAL_W, GLA_VAL_W, GLA_KEY_W, GLA_KEY_W,
                  None, ATT_KV_W, ATT_KV_W, D_MODEL, D_MODEL)
    out_dtypes = (BF16, BF16, BF16, BF16, F32, F32, BF16, BF16, BF16, BF16, BF16)
    q_heads = pl.BlockSpec((ATT_Q_HEADS, tm, ATT_HEAD_DIM), lambda i: (0, i, 0))
    return pl.pallas_call(
        _proj_kernel,
        grid=(t // tm,),
        in_specs=[row(D_MODEL)] + [_const_spec(c.shape) for c in consts_a] + [table] * 3,
        out_specs=[q_heads if w is None else row(w) for w in out_widths]
                  + [pl.BlockSpec((8, V7X_LANES), lambda i: (i, 0))],
        out_shape=[jax.ShapeDtypeStruct((ATT_Q_HEADS, t, ATT_HEAD_DIM) if w is None else (t, w), d)
                   for w, d in zip(out_widths, out_dtypes)]
                  + [jax.ShapeDtypeStruct((8 * (t // tm), V7X_LANES), F32)],
        compiler_params=_params(1),
        name="proj",
    )(h, *consts_a, cos, sin_up, sin_dn)


def _gla_constants():
    c, nl = GLA_C, GLA_LEVELS
    i = np.arange(c)[:, None]
    m = np.arange(c)[None, :]
    sum_f = [m <= i, m > i]
    sum_b = [m >= i, m < i]
    role_f, role_b, mask_f, mask_b = [], [], [], []
    for lvl in range(nl):
        s = c >> (lvl + 1)
        start = (i // (2 * s)) * 2 * s
        mid = start + s
        upper = (i % (2 * s)) >= s
        sum_f.append(np.where(upper, (m >= mid) & (m <= i), (m > i) & (m < mid)))
        sum_b.append(np.where(upper, (m >= mid) & (m < i), (m >= i) & (m < mid)))
        same = (i // (2 * s)) == (m // (2 * s))
        m_upper = (m % (2 * s)) >= s
        role_f.append(np.broadcast_to(upper, (c, V7X_LANES)))
        role_b.append(np.broadcast_to(~upper, (c, V7X_LANES)))
        mask_f.append(same & upper & ~m_upper)
        mask_b.append(same & ~upper & m_upper)
    sum_f = np.concatenate(sum_f, axis=0).astype(np.float32)
    sum_b = np.concatenate(sum_b, axis=0).astype(np.float32)
    sum_f = np.concatenate([sum_f, sum_f], axis=1)
    sum_b = np.concatenate([sum_b, sum_b], axis=1)
    roles = np.stack(role_f + role_b).astype(np.float32)
    masks = np.stack(mask_f + mask_b + [i == m]).astype(np.float32)
    return jnp.asarray(sum_f, BF16), jnp.asarray(sum_b, BF16), jnp.asarray(roles), jnp.asarray(masks)


def _gla_wide_prefix_matrix():
    wide = np.arange(GLA_WIDE_C)
    tri = (wide[None, :] <= wide[:, None]).astype(np.float32)
    return jnp.asarray(np.concatenate([tri, tri], axis=1), BF16)


def _decay_sums(g, sum_ref, n_rows):
    g_hi = g.astype(BF16)
    g_lo = (g - g_hi.astype(F32)).astype(BF16)
    return _dot(sum_ref[0:n_rows, :], jnp.concatenate([g_hi, g_lo], axis=0))


def _gla_scratch(n_chunks, rows):
    return [pltpu.VMEM((rows, GLA_DV), F32),
            pltpu.VMEM((rows, 2 * GLA_DK), BF16),
            pltpu.VMEM((n_chunks, GLA_DV, 2 * GLA_DK), F32),
            pltpu.VMEM((n_chunks, GLA_DV, 2 * GLA_DK), BF16),
            pltpu.VMEM((n_chunks * 8, 2 * GLA_DK), F32)]


def _gla_store_edges(scratch, i, rows, v, qe_f, qe_b, kd_f, kd_b, total_f, total_b):
    _, qe_ref, upd_ref, _, dec_ref = scratch
    qe_ref[rows, :] = jnp.concatenate([qe_f, qe_b], axis=1).astype(BF16)
    kd = jnp.concatenate([kd_f, kd_b], axis=1).astype(BF16)
    upd_ref[i] = _dot_tn(v, kd)
    total = jnp.concatenate([total_f, total_b], axis=1)
    dec_ref[pl.ds(pl.multiple_of(i * 8, 8), 8), :] = jnp.broadcast_to(total, (8, 2 * GLA_DK))


def _gla_scan_step(scratch, srows, i, j, sf, sb):
    _, _, upd_ref, snap_ref, dec_ref = scratch
    dk = GLA_DK
    snap_ref[i, srows, 0:dk] = sf.astype(BF16)
    snap_ref[j, srows, dk:2 * dk] = sb.astype(BF16)
    dec_f = dec_ref[pl.ds(pl.multiple_of(i * 8, 8), 8), 0:dk]
    dec_b = dec_ref[pl.ds(pl.multiple_of(j * 8, 8), 8), dk:2 * dk]
    sf = (sf.reshape(-1, 8, dk) * dec_f[None]).reshape(GLA_SLAB, dk) + upd_ref[i, srows, 0:dk]
    sb = (sb.reshape(-1, 8, dk) * dec_b[None]).reshape(GLA_SLAB, dk) + upd_ref[j, srows, dk:2 * dk]
    return sf, sb


def _gla_finish(scratch, rows, o_inter, gain_ref, sr_ref, o_ref):
    o = scratch[0][rows, :] + o_inter
    o_ref[rows, :] = (_rms(o, gain_ref[...]) * sr_ref[rows, :].astype(F32)).astype(BF16)


def _gla_any_decay(q_ref, k_ref, v_ref, sr_ref, gf_ref, gb_ref, gain_ref,
                   sumf_ref, sumb_ref, role_ref, mask_ref, o_ref, scratch):
    c, nl, dv = GLA_C, GLA_LEVELS, GLA_DV
    n_chunks = q_ref.shape[0] // c
    acc_ref, qe_ref, _, snap_ref, _ = scratch

    def chunk_any_decay(i, carry):
        rows = pl.ds(pl.multiple_of(i * c, c), c)
        q, k, v = q_ref[rows, :], k_ref[rows, :], v_ref[rows, :]
        qf, kf = q.astype(F32), k.astype(F32)
        ef = jnp.exp2(_decay_sums(gf_ref[rows, :], sumf_ref, (2 + nl) * c))
        eb = jnp.exp2(_decay_sums(gb_ref[rows, :], sumb_ref, (2 + nl) * c))
        a = _dot_nt(q, k) * mask_ref[2 * nl]
        for lvl in range(nl):
            blk = slice((2 + lvl) * c, (3 + lvl) * c)
            xf = (jnp.where(role_ref[lvl] > 0.5, qf, kf) * ef[blk]).astype(BF16)
            a = a + _dot_nt(xf, xf) * mask_ref[lvl]
            xb = (jnp.where(role_ref[nl + lvl] > 0.5, qf, kf) * eb[blk]).astype(BF16)
            a = a + _dot_nt(xb, xb) * mask_ref[nl + lvl]
        acc_ref[rows, :] = _dot(a.astype(BF16), v)
        _gla_store_edges(scratch, i, rows, v, qf * ef[0:c], qf * eb[0:c], kf * ef[c:2 * c], kf * eb[c:2 * c],
                         ef[c - 1:c], eb[0:1])
        return carry

    lax.fori_loop(0, n_chunks, chunk_any_decay, 0)

    for slab in range(dv // GLA_SLAB):
        srows = slice(slab * GLA_SLAB, (slab + 1) * GLA_SLAB)
        zero = jnp.zeros((GLA_SLAB, GLA_DK), F32)
        lax.fori_loop(0, n_chunks,
                      lambda step, carry: _gla_scan_step(scratch, srows, step, n_chunks - 1 - step, *carry),
                      (zero, zero), unroll=2)

    def finish(i, carry):
        rows = pl.ds(pl.multiple_of(i * c, c), c)
        _gla_finish(scratch, rows, _dot_nt(qe_ref[rows, :], snap_ref[i]), gain_ref, sr_ref, o_ref)
        return carry

    lax.fori_loop(0, n_chunks, finish, 0)


def _gla_head_spec(seq, width):
    return pl.BlockSpec((None, seq, width), lambda b, h: (b, 0, h))


def _score_bound(q_gain, k_gain):
    return (jnp.max(jnp.abs(q_gain)) * jnp.max(jnp.abs(k_gain))
            * (1.02 * ATT_HEAD_DIM ** 0.5 * LOG2_E))


class _GlaMild:
    def __init__(self, q_ref, k_ref, v_ref, sr_ref, gf_ref, gb_ref, gain_ref, tri_ref, o_ref, scratch):
        self.refs = (q_ref, k_ref, v_ref, sr_ref, gf_ref, gb_ref, gain_ref, tri_ref, o_ref)
        self.scratch = scratch
        w = GLA_WIDE_C
        self.n = q_ref.shape[0] // w
        self.rows = [slice(i * w, (i + 1) * w) for i in range(self.n)]

    def prefix_sums(self):
        _, _, _, _, gf_ref, gb_ref, _, tri_ref, _ = self.refs
        self.g_b = [gb_ref[r, :] for r in self.rows]
        self.prefix = [_decay_sums(jnp.concatenate([gf_ref[r, :], g], axis=1), tri_ref, GLA_WIDE_C)
                       for r, g in zip(self.rows, self.g_b)]

    def scores(self):
        q_ref, k_ref = self.refs[0:2]
        w, dk = GLA_WIDE_C, GLA_DK
        self.score, self.edge = [], []
        for r, g, p in zip(self.rows, self.g_b, self.prefix):
            qf, kf = q_ref[r, :].astype(F32), k_ref[r, :].astype(F32)
            bf, pb = p[:, 0:dk], p[:, dk:2 * dk]
            bb = (pb[w - 1:w] - pb) + g
            mid_f, mid_b = bf[w // 2 - 1:w // 2], bb[w // 2:w // 2 + 1]
            rf, rb = bf - mid_f, bb - mid_b
            q_f, k_f = qf * jnp.exp2(rf), kf * jnp.exp2(-rf)
            q_b, k_b = qf * jnp.exp2(rb), kf * jnp.exp2(-rb)
            self.score.append((_dot_nt(q_f.astype(BF16), k_f.astype(BF16)),
                               _dot_nt(q_b.astype(BF16), k_b.astype(BF16))))
            self.edge.append((q_f * jnp.exp2(mid_f), q_b * jnp.exp2(mid_b),
                              k_f * jnp.exp2(bf[w - 1:w] - mid_f), k_b * jnp.exp2(bb[0:1] - mid_b),
                              jnp.exp2(bf[w - 1:w]), jnp.exp2(bb[0:1])))

    def intra(self):
        v_ref, w = self.refs[2], GLA_WIDE_C
        lower = lax.broadcasted_iota(jnp.int32, (w, w), 0) >= lax.broadcasted_iota(jnp.int32, (w, w), 1)
        for r, (s_f, s_b) in zip(self.rows, self.score):
            self.scratch[0][r, :] = _dot(jnp.where(lower, s_f, s_b).astype(BF16), v_ref[r, :])

    def edges(self):
        v_ref = self.refs[2]
        for i, (r, parts) in enumerate(zip(self.rows, self.edge)):
            _gla_store_edges(self.scratch, i, r, v_ref[r, :], *parts)

    def scan(self):
        for slab in range(GLA_DV // GLA_SLAB):
            srows = slice(slab * GLA_SLAB, (slab + 1) * GLA_SLAB)
            sf = sb = jnp.zeros((GLA_SLAB, GLA_DK), F32)
            for step in range(self.n):
                sf, sb = _gla_scan_step(self.scratch, srows, step, self.n - 1 - step, sf, sb)

    def inter(self):
        qe_ref, snap_ref = self.scratch[1], self.scratch[3]
        self.o_inter = [_dot_nt(qe_ref[r, :], snap_ref[i]) for i, r in enumerate(self.rows)]

    def finish(self):
        sr_ref, gain_ref, o_ref = self.refs[3], self.refs[6], self.refs[8]
        for r, o_inter in zip(self.rows, self.o_inter):
            _gla_finish(self.scratch, r, o_inter, gain_ref, sr_ref, o_ref)


def _mixers_kernel(limits_ref, gq_ref, gk_ref, gv_ref, sr_ref, gf_ref, gb_ref, gain_ref,
                   sumf_ref, sumb_ref, role_ref, mask_ref, tri_ref, aq_ref, ak_ref, av_ref,
                   oa_ref, ob_ref, vext_ref, *scratch):
    hd, tq = ATT_HEAD_DIM, min(ATT_Q_TILE, aq_ref.shape[1])
    tiles = aq_ref.shape[1] // tq
    bound = limits_ref[0]
    vext_ref[:, :hd] = av_ref[...]
    vext_ref[:, hd:] = jnp.ones((av_ref.shape[0], hd), BF16)

    def weighted_values(p):
        o_ext = _dot(p, vext_ref[...])
        return (o_ext[:, :hd] / o_ext[:, hd:]).astype(BF16)

    mild = jnp.logical_and(bound <= ATT_SHIFT_LIMIT_LOG2, limits_ref[1] >= -GLA_MILD_LOG2)

    @pl.when(mild)
    def _():
        def attention_unit(tile, g):
            rows = slice(tile * tq, (tile + 1) * tq)
            p = jnp.exp2(_dot_nt(aq_ref[g, rows, :], ak_ref[...]) - bound).astype(BF16)
            ob_ref[rows, g * hd:(g + 1) * hd] = weighted_values(p)

        units = [(tile, g) for tile in range(tiles) for g in range(ATT_GROUP)]
        gla = _GlaMild(gq_ref, gk_ref, gv_ref, sr_ref, gf_ref, gb_ref, gain_ref, tri_ref, oa_ref, scratch)
        stages = [gla.prefix_sums, gla.scores, gla.intra, gla.edges, gla.scan, gla.inter, gla.finish]
        per_stage = -(-len(units) // (len(stages) + 1))
        for stage in stages:
            for unit in units[:per_stage]:
                attention_unit(*unit)
            units = units[per_stage:]
            stage()
        for unit in units:
            attention_unit(*unit)

    @pl.when(jnp.logical_not(mild))
    def _():
        _gla_any_decay(gq_ref, gk_ref, gv_ref, sr_ref, gf_ref, gb_ref, gain_ref,
                       sumf_ref, sumb_ref, role_ref, mask_ref, oa_ref, scratch)
        for tile in range(tiles):
            rows = slice(tile * tq, (tile + 1) * tq)
            q = aq_ref[:, rows, :].reshape(ATT_GROUP * tq, hd)
            s = _dot_nt(q, ak_ref[...])
            o = weighted_values(jnp.exp2(s - jnp.max(s, axis=-1, keepdims=True)).astype(BF16))
            for g in range(ATT_GROUP):
                ob_ref[rows, g * hd:(g + 1) * hd] = o[g * tq:(g + 1) * tq]


def _mixers(limits, gq, gk, gv, sr, lgf, lgb, gain, aq, ak, av, batch, seq):
    share = GLA_HEADS // ATT_KV_HEADS
    rows = seq // share
    head = functools.partial(_gla_head_spec, seq)
    qspec = pl.BlockSpec((ATT_GROUP, rows, ATT_HEAD_DIM), lambda b, h: (h // share, b * share + h % share, 0))
    kvspec = pl.BlockSpec((None, seq, ATT_HEAD_DIM), lambda b, h: (b, 0, h // share))
    obspec = pl.BlockSpec((None, rows, ATT_GROUP * ATT_HEAD_DIM), lambda b, h: (b, h % share, h // share))
    consts = (*_gla_constants(), _gla_wide_prefix_matrix())
    return pl.pallas_call(
        _mixers_kernel,
        grid=(batch, GLA_HEADS),
        in_specs=[pl.BlockSpec(memory_space=pltpu.SMEM),
                  head(GLA_DK), head(GLA_DK), head(GLA_DV), head(GLA_DV), head(GLA_DK), head(GLA_DK),
                  pl.BlockSpec((None, 1, GLA_DV), lambda b, h: (h, 0, 0))]
                 + [_const_spec(cst.shape) for cst in consts] + [qspec, kvspec, kvspec],
        out_specs=[head(GLA_DV), obspec],
        out_shape=[jax.ShapeDtypeStruct((batch, seq, GLA_VAL_W), BF16),
                   jax.ShapeDtypeStruct((batch, seq, ATT_Q_W), BF16)],
        scratch_shapes=[pltpu.VMEM((seq, 2 * ATT_HEAD_DIM), BF16)] + _gla_scratch(seq // GLA_C, seq),
        compiler_params=_params(2),
        name="mixers",
    )(limits, gq, gk, gv, sr, lgf, lgb, gain, *consts, aq, ak, av)


def _merge_kernel(h_ref, oa_ref, ob_ref, sga_ref, sgb_ref, wa_ref, wb_ref, wo_ref, gpost_ref, o_ref):
    for rows in _row_parts(h_ref, MERGE_ROWS):
        ya = _dot(oa_ref[rows, :], wa_ref[...])
        yb = _dot(ob_ref[rows, :], wb_ref[...])
        merged = sga_ref[rows, :].astype(F32) * ya + sgb_ref[rows, :].astype(F32) * yb
        m = _dot(merged.astype(BF16), wo_ref[...])
        o_ref[rows, :] = h_ref[rows, :] + _rms(m, gpost_ref[...])


def _merge(h, oa, ob, sga, sgb, w_a, w_b, w_o, g_post):
    t = h.shape[0]
    tm = min(STEP_ROWS, t)
    row = pl.BlockSpec((tm, D_MODEL), lambda i: (i, 0))
    sq = _const_spec((D_MODEL, D_MODEL))
    return pl.pallas_call(
        _merge_kernel,
        grid=(t // tm,),
        in_specs=[row] * 5 + [sq, sq, sq, _const_spec((1, D_MODEL))],
        out_specs=row,
        out_shape=jax.ShapeDtypeStruct((t, D_MODEL), F32),
        compiler_params=_params(1),
        name="merge",
    )(h, oa, ob, sga, sgb, w_a, w_b, w_o, g_post)


def _split_w_in(w_in):
    gq, gk, gv, gr, za_f, za_b, aq, ak, av, ga, gb = jnp.split(w_in.astype(BF16), IN_OFFSETS, axis=-1)
    pad = jnp.zeros((D_MODEL, V7X_LANES - 2 * GLA_RANK), BF16)
    cat = lambda *parts: jnp.concatenate(parts, axis=-1)
    return cat(za_f, za_b, pad), cat(aq, ak), cat(ga, gb, gr), cat(gq, gk, gv, av)


def _pad_decay_up(up, first_row):
    out = jnp.zeros((V7X_LANES, GLA_KEY_W), BF16)
    return lax.dynamic_update_slice(out, up.astype(BF16), (first_row, 0))


def kernel(x, ffn1_pre_g, ffn1_w_in, ffn1_w_out, ffn1_post_g, mix_pre_g, w_in, gla_decay_up_f,
           gla_decay_bias_f, gla_decay_up_b, gla_decay_bias_b, gla_out_g, w_branch_a, att_q_norm_g,
           att_k_norm_g, w_branch_b, w_out, mix_post_g, ffn2_pre_g, ffn2_w_in, ffn2_w_out, ffn2_post_g):
    batch, seq, d = x.shape
    assert d == D_MODEL and seq % max(GLA_WIDE_C, GRID_W) == 0
    assert seq % ATT_Q_TILE == 0 or seq < ATT_Q_TILE
    assert seq % PROJ_ROW_TILE == 0 or seq < PROJ_ROW_TILE
    depth = w_in.shape[0]
    t = batch * seq
    h = x.reshape(t, d)
    vec = lambda g: g.reshape(1, -1).astype(F32)
    for l in range(depth):
        h = _ffn(h, vec(ffn1_pre_g[l]), ffn1_w_in[l].astype(BF16), ffn1_w_out[l].astype(BF16),
                 vec(ffn1_post_g[l]))
        gq, gk, gv, sr, lgf, lgb, aq, ak, av, sga, sgb, decay_min = _proj(
            h, seq, vec(mix_pre_g[l]), _split_w_in(w_in[l]),
            _pad_decay_up(gla_decay_up_f[l], 0), _pad_decay_up(gla_decay_up_b[l], GLA_RANK),
            vec(gla_decay_bias_f[l]), vec(gla_decay_bias_b[l]),
            vec(att_q_norm_g[l]), vec(att_k_norm_g[l]))
        per_seq = lambda a: a.reshape(batch, seq, a.shape[-1])
        limits = jnp.stack([_score_bound(att_q_norm_g[l], att_k_norm_g[l]), jnp.min(decay_min)]).astype(F32)
        oa, ob = _mixers(limits, per_seq(gq), per_seq(gk), per_seq(gv), per_seq(sr), per_seq(lgf),
                         per_seq(lgb), gla_out_g[l].reshape(GLA_HEADS, 1, GLA_DV).astype(F32),
                         aq, per_seq(ak), per_seq(av), batch, seq)
        h = _merge(h, oa.reshape(t, GLA_VAL_W), ob.reshape(t, ATT_Q_W), sga, sgb,
                   w_branch_a[l].astype(BF16), w_branch_b[l].astype(BF16), w_out[l].astype(BF16),
                   vec(mix_post_g[l]))
        h = _ffn(h, vec(ffn2_pre_g[l]), ffn2_w_in[l].astype(BF16), ffn2_w_out[l].astype(BF16),
                 vec(ffn2_post_g[l]))
    return h.reshape(batch, seq, d)
```

```python
import functools

import numpy as np
import jax
import jax.numpy as jnp
from jax import lax
from jax.experimental import pallas as pl
from jax.experimental.pallas import tpu as pltpu

F32 = jnp.float32
BF16 = jnp.bfloat16

D_MODEL = 1024
GRID_W = 64
D_FF = 2816
EPS = 1e-6
GLA_HEADS = 4
GLA_DK = 128
GLA_DV = 256
GLA_RANK = 16
GLA_TAU = 16.0
ATT_Q_HEADS = 8
ATT_KV_HEADS = 2
ATT_HEAD_DIM = 128
ATT_GROUP = ATT_Q_HEADS // ATT_KV_HEADS
ROPE_THETA = 10000.0
GLA_KEY_W = GLA_HEADS * GLA_DK
GLA_VAL_W = GLA_HEADS * GLA_DV
ATT_Q_W = ATT_Q_HEADS * ATT_HEAD_DIM
ATT_KV_W = ATT_KV_HEADS * ATT_HEAD_DIM
IN_SPLITS = (GLA_KEY_W, GLA_KEY_W, GLA_VAL_W, GLA_VAL_W, GLA_RANK, GLA_RANK,
             ATT_Q_W, ATT_KV_W, ATT_KV_W, D_MODEL, D_MODEL)
IN_OFFSETS = tuple(int(s) for s in np.cumsum(IN_SPLITS)[:-1])
LOG2_E = 1.4426950408889634

V7X_LANES = 128
V7X_VMEM_BYTES = 64 * 1024 * 1024
VMEM_LIMIT = V7X_VMEM_BYTES - 8 * 1024 * 1024

FFN_STEP_ROWS = 2048
MERGE_STEP_ROWS = 1024
FFN_ROWS = 256
MERGE_ROWS = 512
PROJ_ROW_TILE = 512
PROJ_ROWS = 256
ATT_Q_TILE = 512
ATT_SHIFT_LIMIT_LOG2 = 60.0
GLA_C = 64
GLA_LEVELS = GLA_C.bit_length() - 1
GLA_SLAB = 64
GLA_WIDE_C = 256
GLA_MILD_LOG2 = 48.0
FF_CHUNKS = ((0, 1024), (1024, 1024), (2048, 768))


def _const_spec(shape):
    zeros = (0,) * len(shape)
    return pl.BlockSpec(shape, lambda *_: zeros, pipeline_mode=pl.Buffered(1))


def _params(n_axes):
    return pltpu.CompilerParams(dimension_semantics=("arbitrary",) * n_axes,
                                vmem_limit_bytes=VMEM_LIMIT)


def _rms(x, g):
    ms = jnp.mean(x * x, axis=-1, keepdims=True)
    return x * lax.rsqrt(ms + EPS) * g


def _dot(a, b):
    return jnp.dot(a, b, preferred_element_type=F32)


def _dot_nt(a, b):
    return lax.dot_general(a, b, (((1,), (1,)), ((), ())), preferred_element_type=F32)


def _dot_tn(a, b):
    return lax.dot_general(a, b, (((0,), (0,)), ((), ())), preferred_element_type=F32)


def _row_parts(ref, rows):
    rows = min(rows, ref.shape[0])
    return [slice(r, r + rows) for r in range(0, ref.shape[0], rows)]


def _ffn_kernel(x_ref, gpre_ref, win_ref, wout_ref, gpost_ref, o_ref):
    for rows in _row_parts(x_ref, FFN_ROWS):
        x = x_ref[rows, :]
        xn = _rms(x, gpre_ref[...]).astype(BF16)
        acc = None
        for c0, cw in FF_CHUNKS:
            gate = _dot(xn, win_ref[:, c0:c0 + cw])
            up = _dot(xn, win_ref[:, D_FF + c0:D_FF + c0 + cw])
            act = (gate * jax.nn.sigmoid(gate) * up).astype(BF16)
            part = _dot(act, wout_ref[c0:c0 + cw, :])
            acc = part if acc is None else acc + part
        o_ref[rows, :] = x + 0.5 * _rms(acc, gpost_ref[...])


def _ffn(x, g_pre, w_in, w_out, g_post):
    t = x.shape[0]
    tm = min(FFN_STEP_ROWS, t)
    row = pl.BlockSpec((tm, D_MODEL), lambda i: (i, 0))
    return pl.pallas_call(
        _ffn_kernel,
        grid=(t // tm,),
        in_specs=[row, _const_spec((1, D_MODEL)), _const_spec((D_MODEL, 2 * D_FF)),
                  _const_spec((D_FF, D_MODEL)), _const_spec((1, D_MODEL))],
        out_specs=row,
        out_shape=jax.ShapeDtypeStruct((t, D_MODEL), F32),
        compiler_params=_params(1),
        name="ffn",
    )(x, g_pre, w_in, w_out, g_post)


def _log_sigmoid(z):
    return jnp.minimum(z, 0.0) - jnp.log(1.0 + jnp.exp(-jnp.abs(z)))


def _rope(x, cos, sin_up, sin_dn):
    return (x * cos + pltpu.roll(x, V7X_LANES - 32, axis=1) * sin_up
            + pltpu.roll(x, 32, axis=1) * sin_dn)


def _proj_kernel(h_ref, gmix_ref, wz_ref, watt_ref, wgate_ref, wplain_ref, upf_ref, upb_ref,
                 bf_ref, bb_ref, qg_ref, kg_ref, cos_ref, sup_ref, sdn_ref,
                 gq_o, gk_o, gv_o, sr_o, lf_o, lb_o, aq_o, ak_o, av_o, sga_o, sgb_o, dmin_o):
    qg = qg_ref[...] * (ATT_HEAD_DIM ** -0.5 * LOG2_E)
    worst = None
    for rows in _row_parts(h_ref, PROJ_ROWS):
        u = _rms(h_ref[rows, :], gmix_ref[...]).astype(BF16)
        za = _dot(u, wz_ref[...]).astype(BF16)

        cos, sup, sdn = cos_ref[rows, :], sup_ref[rows, :], sdn_ref[rows, :]
        att = _dot(u, watt_ref[...])
        for hd in range(ATT_Q_HEADS):
            sl = slice(hd * ATT_HEAD_DIM, (hd + 1) * ATT_HEAD_DIM)
            aq_o[hd, rows, :] = _rope(_rms(att[:, sl], qg), cos, sup, sdn).astype(BF16)
        for hd in range(ATT_KV_HEADS):
            sl = slice(hd * ATT_HEAD_DIM, (hd + 1) * ATT_HEAD_DIM)
            ak_o[rows, sl] = _rope(_rms(att[:, ATT_Q_W + sl.start:ATT_Q_W + sl.stop], kg_ref[...]),
                                   cos, sup, sdn).astype(BF16)

        gate = _dot(u, wgate_ref[...])
        sga_o[rows, :] = jax.nn.sigmoid(gate[:, 0:D_MODEL]).astype(BF16)
        sgb_o[rows, :] = jax.nn.sigmoid(gate[:, D_MODEL:2 * D_MODEL]).astype(BF16)
        gr = gate[:, 2 * D_MODEL:]
        sr_o[rows, :] = (gr * jax.nn.sigmoid(gr)).astype(BF16)
        lf = _log_sigmoid(_dot(za, upf_ref[...]) + bf_ref[...]) * (LOG2_E / GLA_TAU)
        lb = _log_sigmoid(_dot(za, upb_ref[...]) + bb_ref[...]) * (LOG2_E / GLA_TAU)
        lf_o[rows, :] = lf
        lb_o[rows, :] = lb
        totals = jnp.minimum(lf.reshape(-1, GLA_WIDE_C, GLA_KEY_W).sum(axis=1),
                             lb.reshape(-1, GLA_WIDE_C, GLA_KEY_W).sum(axis=1))
        part = jnp.min(totals, axis=0, keepdims=True)
        worst = part if worst is None else jnp.minimum(worst, part)

        plain = _dot(u, wplain_ref[...])
        gq_o[rows, :] = (plain[:, 0:GLA_KEY_W] * (GLA_DK ** -0.5)).astype(BF16)
        gk_o[rows, :] = plain[:, GLA_KEY_W:2 * GLA_KEY_W].astype(BF16)
        gv_o[rows, :] = plain[:, 2 * GLA_KEY_W:2 * GLA_KEY_W + GLA_VAL_W].astype(BF16)
        av_o[rows, :] = plain[:, 2 * GLA_KEY_W + GLA_VAL_W:].astype(BF16)
    worst = functools.reduce(jnp.minimum, [worst[:, l:l + V7X_LANES] for l in range(0, GLA_KEY_W, V7X_LANES)])
    dmin_o[...] = jnp.broadcast_to(worst, dmin_o.shape)


def _rope_tables(seq):
    pos = np.arange(seq)
    half = ATT_HEAD_DIM // 2
    inv_freq = ROPE_THETA ** (-np.arange(0, half, 2, dtype=np.float64) / half)
    ang_r = (pos // GRID_W)[:, None] * inv_freq
    ang_c = (pos % GRID_W)[:, None] * inv_freq
    zero = np.zeros_like(ang_r)
    cos = np.concatenate([np.cos(ang_r)] * 2 + [np.cos(ang_c)] * 2, axis=-1)
    sin_up = np.concatenate([-np.sin(ang_r), zero, -np.sin(ang_c), zero], axis=-1)
    sin_dn = np.concatenate([zero, np.sin(ang_r), zero, np.sin(ang_c)], axis=-1)
    return tuple(jnp.asarray(t, F32) for t in (cos, sin_up, sin_dn))


def _proj(h, seq, g_mix, w_parts, up_f, up_b, bias_f, bias_b, q_gain, k_gain):
    t = h.shape[0]
    tm = min(PROJ_ROW_TILE, seq)
    tiles_per_seq = seq // tm
    cos, sin_up, sin_dn = _rope_tables(seq)

    def row(width):
        return pl.BlockSpec((tm, width), lambda i: (i, 0))

    table = pl.BlockSpec((tm, ATT_HEAD_DIM), lambda i: (i % tiles_per_seq, 0))
    consts_a = (g_mix, *w_parts, up_f, up_b, bias_f, bias_b, q_gain, k_gain)
    out_widths = (GLA_KEY_W, GLA_KEY_W, GLA_VAL_W, GLA_VAL_W, GLA_KEY_W, GLA_KEY_W,
                  None, ATT_KV_W, ATT_KV_W, D_MODEL, D_MODEL)
    out_dtypes = (BF16, BF16, BF16, BF16, F32, F32, BF16, BF16, BF16, BF16, BF16)
    q_heads = pl.BlockSpec((ATT_Q_HEADS, tm, ATT_HEAD_DIM), lambda i: (0, i, 0))
    return pl.pallas_call(
        _proj_kernel,
        grid=(t // tm,),
        in_specs=[row(D_MODEL)] + [_const_spec(c.shape) for c in consts_a] + [table] * 3,
        out_specs=[q_heads if w is None else row(w) for w in out_widths]
                  + [pl.BlockSpec((8, V7X_LANES), lambda i: (i, 0))],
        out_shape=[jax.ShapeDtypeStruct((ATT_Q_HEADS, t, ATT_HEAD_DIM) if w is None else (t, w), d)
                   for w, d in zip(out_widths, out_dtypes)]
                  + [jax.ShapeDtypeStruct((8 * (t // tm), V7X_LANES), F32)],
        compiler_params=_params(1),
        name="proj",
    )(h, *consts_a, cos, sin_up, sin_dn)


def _gla_constants():
    c, nl = GLA_C, GLA_LEVELS
    i = np.arange(c)[:, None]
    m = np.arange(c)[None, :]
    sum_f = [m <= i, m > i]
    sum_b = [m >= i, m < i]
    role_f, role_b, mask_f, mask_b = [], [], [], []
    for lvl in range(nl):
        s = c >> (lvl + 1)
        start = (i // (2 * s)) * 2 * s
        mid = start + s
        upper = (i % (2 * s)) >= s
        sum_f.append(np.where(upper, (m >= mid) & (m <= i), (m > i) & (m < mid)))
        sum_b.append(np.where(upper, (m >= mid) & (m < i), (m >= i) & (m < mid)))
        same = (i // (2 * s)) == (m // (2 * s))
        m_upper = (m % (2 * s)) >= s
        role_f.append(np.broadcast_to(upper, (c, V7X_LANES)))
        role_b.append(np.broadcast_to(~upper, (c, V7X_LANES)))
        mask_f.append(same & upper & ~m_upper)
        mask_b.append(same & ~upper & m_upper)
    sum_f = np.concatenate(sum_f, axis=0).astype(np.float32)
    sum_b = np.concatenate(sum_b, axis=0).astype(np.float32)
    sum_f = np.concatenate([sum_f, sum_f], axis=1)
    sum_b = np.concatenate([sum_b, sum_b], axis=1)
    roles = np.stack(role_f + role_b).astype(np.float32)
    masks = np.stack(mask_f + mask_b + [i == m]).astype(np.float32)
    return jnp.asarray(sum_f, BF16), jnp.asarray(sum_b, BF16), jnp.asarray(roles), jnp.asarray(masks)


def _gla_wide_prefix_matrix():
    wide = np.arange(GLA_WIDE_C)
    tri = (wide[None, :] <= wide[:, None]).astype(np.float32)
    return jnp.asarray(np.concatenate([tri, tri], axis=1), BF16)


def _decay_sums(g, sum_ref, n_rows):
    g_hi = g.astype(BF16)
    g_lo = (g - g_hi.astype(F32)).astype(BF16)
    return _dot(sum_ref[0:n_rows, :], jnp.concatenate([g_hi, g_lo], axis=0))


def _gla_scratch(n_chunks, rows):
    return [pltpu.VMEM((rows, GLA_DV), F32),
            pltpu.VMEM((rows, 2 * GLA_DK), BF16),
            pltpu.VMEM((n_chunks, GLA_DV, 2 * GLA_DK), F32),
            pltpu.VMEM((n_chunks, GLA_DV, 2 * GLA_DK), BF16),
            pltpu.VMEM((n_chunks * 8, 2 * GLA_DK), F32)]


def _gla_store_edges(scratch, i, rows, v, qe_f, qe_b, kd_f, kd_b, total_f, total_b):
    _, qe_ref, upd_ref, _, dec_ref = scratch
    qe_ref[rows, :] = jnp.concatenate([qe_f, qe_b], axis=1).astype(BF16)
    kd = jnp.concatenate([kd_f, kd_b], axis=1).astype(BF16)
    upd_ref[i] = _dot_tn(v, kd)
    total = jnp.concatenate([total_f, total_b], axis=1)
    dec_ref[pl.ds(pl.multiple_of(i * 8, 8), 8), :] = jnp.broadcast_to(total, (8, 2 * GLA_DK))


def _gla_scan_step(scratch, srows, i, j, sf, sb):
    _, _, upd_ref, snap_ref, dec_ref = scratch
    dk = GLA_DK
    snap_ref[i, srows, 0:dk] = sf.astype(BF16)
    snap_ref[j, srows, dk:2 * dk] = sb.astype(BF16)
    dec_f = dec_ref[pl.ds(pl.multiple_of(i * 8, 8), 8), 0:dk]
    dec_b = dec_ref[pl.ds(pl.multiple_of(j * 8, 8), 8), dk:2 * dk]
    sf = (sf.reshape(-1, 8, dk) * dec_f[None]).reshape(GLA_SLAB, dk) + upd_ref[i, srows, 0:dk]
    sb = (sb.reshape(-1, 8, dk) * dec_b[None]).reshape(GLA_SLAB, dk) + upd_ref[j, srows, dk:2 * dk]
    return sf, sb


def _gla_finish(scratch, rows, o_inter, gain_ref, sr_ref, o_ref):
    o = scratch[0][rows, :] + o_inter
    o_ref[rows, :] = (_rms(o, gain_ref[...]) * sr_ref[rows, :].astype(F32)).astype(BF16)


def _gla_any_decay(q_ref, k_ref, v_ref, sr_ref, gf_ref, gb_ref, gain_ref,
                   sumf_ref, sumb_ref, role_ref, mask_ref, o_ref, scratch):
    c, nl, dv = GLA_C, GLA_LEVELS, GLA_DV
    n_chunks = q_ref.shape[0] // c
    acc_ref, qe_ref, _, snap_ref, _ = scratch

    def chunk_any_decay(i, carry):
        rows = pl.ds(pl.multiple_of(i * c, c), c)
        q, k, v = q_ref[rows, :], k_ref[rows, :], v_ref[rows, :]
        qf, kf = q.astype(F32), k.astype(F32)
        ef = jnp.exp2(_decay_sums(gf_ref[rows, :], sumf_ref, (2 + nl) * c))
        eb = jnp.exp2(_decay_sums(gb_ref[rows, :], sumb_ref, (2 + nl) * c))
        a = _dot_nt(q, k) * mask_ref[2 * nl]
        for lvl in range(nl):
            blk = slice((2 + lvl) * c, (3 + lvl) * c)
            xf = (jnp.where(role_ref[lvl] > 0.5, qf, kf) * ef[blk]).astype(BF16)
            a = a + _dot_nt(xf, xf) * mask_ref[lvl]
            xb = (jnp.where(role_ref[nl + lvl] > 0.5, qf, kf) * eb[blk]).astype(BF16)
            a = a + _dot_nt(xb, xb) * mask_ref[nl + lvl]
        acc_ref[rows, :] = _dot(a.astype(BF16), v)
        _gla_store_edges(scratch, i, rows, v, qf * ef[0:c], qf * eb[0:c], kf * ef[c:2 * c], kf * eb[c:2 * c],
                         ef[c - 1:c], eb[0:1])
        return carry

    lax.fori_loop(0, n_chunks, chunk_any_decay, 0)

    for slab in range(dv // GLA_SLAB):
        srows = slice(slab * GLA_SLAB, (slab + 1) * GLA_SLAB)
        zero = jnp.zeros((GLA_SLAB, GLA_DK), F32)
        lax.fori_loop(0, n_chunks,
                      lambda step, carry: _gla_scan_step(scratch, srows, step, n_chunks - 1 - step, *carry),
                      (zero, zero), unroll=2)

    def finish(i, carry):
        rows = pl.ds(pl.multiple_of(i * c, c), c)
        _gla_finish(scratch, rows, _dot_nt(qe_ref[rows, :], snap_ref[i]), gain_ref, sr_ref, o_ref)
        return carry

    lax.fori_loop(0, n_chunks, finish, 0)


def _gla_head_spec(seq, width):
    return pl.BlockSpec((None, seq, width), lambda b, h: (b, 0, h))


def _score_bound(q_gain, k_gain):
    return (jnp.max(jnp.abs(q_gain)) * jnp.max(jnp.abs(k_gain))
            * (1.02 * ATT_HEAD_DIM ** 0.5 * LOG2_E))


class _GlaMild:
    def __init__(self, q_ref, k_ref, v_ref, sr_ref, gf_ref, gb_ref, gain_ref, tri_ref, o_ref, scratch):
        self.refs = (q_ref, k_ref, v_ref, sr_ref, gf_ref, gb_ref, gain_ref, tri_ref, o_ref)
        self.scratch = scratch
        w = GLA_WIDE_C
        self.n = q_ref.shape[0] // w
        self.rows = [slice(i * w, (i + 1) * w) for i in range(self.n)]

    def prefix_sums(self):
        _, _, _, _, gf_ref, gb_ref, _, tri_ref, _ = self.refs
        self.g_b = [gb_ref[r, :] for r in self.rows]
        self.prefix = [_decay_sums(jnp.concatenate([gf_ref[r, :], g], axis=1), tri_ref, GLA_WIDE_C)
                       for r, g in zip(self.rows, self.g_b)]

    def scores(self):
        q_ref, k_ref = self.refs[0:2]
        w, dk = GLA_WIDE_C, GLA_DK
        self.score, self.edge = [], []
        for r, g, p in zip(self.rows, self.g_b, self.prefix):
            qf, kf = q_ref[r, :].astype(F32), k_ref[r, :].astype(F32)
            bf, pb = p[:, 0:dk], p[:, dk:2 * dk]
            bb = (pb[w - 1:w] - pb) + g
            mid_f, mid_b = bf[w // 2 - 1:w // 2], bb[w // 2:w // 2 + 1]
            rf, rb = bf - mid_f, bb - mid_b
            q_f, k_f = qf * jnp.exp2(rf), kf * jnp.exp2(-rf)
            q_b, k_b = qf * jnp.exp2(rb), kf * jnp.exp2(-rb)
            self.score.append((_dot_nt(q_f.astype(BF16), k_f.astype(BF16)),
                               _dot_nt(q_b.astype(BF16), k_b.astype(BF16))))
            self.edge.append((q_f * jnp.exp2(mid_f), q_b * jnp.exp2(mid_b),
                              k_f * jnp.exp2(bf[w - 1:w] - mid_f), k_b * jnp.exp2(bb[0:1] - mid_b),
                              jnp.exp2(bf[w - 1:w]), jnp.exp2(bb[0:1])))

    def intra(self):
        v_ref, w = self.refs[2], GLA_WIDE_C
        lower = lax.broadcasted_iota(jnp.int32, (w, w), 0) >= lax.broadcasted_iota(jnp.int32, (w, w), 1)
        for r, (s_f, s_b) in zip(self.rows, self.score):
            self.scratch[0][r, :] = _dot(jnp.where(lower, s_f, s_b).astype(BF16), v_ref[r, :])

    def edges(self):
        v_ref = self.refs[2]
        for i, (r, parts) in enumerate(zip(self.rows, self.edge)):
            _gla_store_edges(self.scratch, i, r, v_ref[r, :], *parts)

    def scan(self):
        for slab in range(GLA_DV // GLA_SLAB):
            srows = slice(slab * GLA_SLAB, (slab + 1) * GLA_SLAB)
            sf = sb = jnp.zeros((GLA_SLAB, GLA_DK), F32)
            for step in range(self.n):
                sf, sb = _gla_scan_step(self.scratch, srows, step, self.n - 1 - step, sf, sb)

    def inter(self):
        qe_ref, snap_ref = self.scratch[1], self.scratch[3]
        self.o_inter = [_dot_nt(qe_ref[r, :], snap_ref[i]) for i, r in enumerate(self.rows)]

    def finish(self):
        sr_ref, gain_ref, o_ref = self.refs[3], self.refs[6], self.refs[8]
        for r, o_inter in zip(self.rows, self.o_inter):
            _gla_finish(self.scratch, r, o_inter, gain_ref, sr_ref, o_ref)


def _mixers_kernel(limits_ref, gq_ref, gk_ref, gv_ref, sr_ref, gf_ref, gb_ref, gain_ref,
                   sumf_ref, sumb_ref, role_ref, mask_ref, tri_ref, aq_ref, ak_ref, av_ref,
                   oa_ref, ob_ref, vext_ref, *scratch):
    hd, tq = ATT_HEAD_DIM, min(ATT_Q_TILE, aq_ref.shape[1])
    tiles = aq_ref.shape[1] // tq
    bound = limits_ref[0]
    vext_ref[:, :hd] = av_ref[...]
    vext_ref[:, hd:] = jnp.ones((av_ref.shape[0], hd), BF16)

    def weighted_values(p):
        o_ext = _dot(p, vext_ref[...])
        return (o_ext[:, :hd] / o_ext[:, hd:]).astype(BF16)

    mild = jnp.logical_and(bound <= ATT_SHIFT_LIMIT_LOG2, limits_ref[1] >= -GLA_MILD_LOG2)

    @pl.when(mild)
    def _():
        def attention_unit(tile, g):
            rows = slice(tile * tq, (tile + 1) * tq)
            p = jnp.exp2(_dot_nt(aq_ref[g, rows, :], ak_ref[...]) - bound).astype(BF16)
            ob_ref[rows, g * hd:(g + 1) * hd] = weighted_values(p)

        units = [(tile, g) for tile in range(tiles) for g in range(ATT_GROUP)]
        gla = _GlaMild(gq_ref, gk_ref, gv_ref, sr_ref, gf_ref, gb_ref, gain_ref, tri_ref, oa_ref, scratch)
        stages = [gla.prefix_sums, gla.scores, gla.intra, gla.edges, gla.scan, gla.inter, gla.finish]
        per_stage = -(-len(units) // (len(stages) + 1))
        for stage in stages:
            for unit in units[:per_stage]:
                attention_unit(*unit)
            units = units[per_stage:]
            stage()
        for unit in units:
            attention_unit(*unit)

    @pl.when(jnp.logical_not(mild))
    def _():
        _gla_any_decay(gq_ref, gk_ref, gv_ref, sr_ref, gf_ref, gb_ref, gain_ref,
                       sumf_ref, sumb_ref, role_ref, mask_ref, oa_ref, scratch)
        for tile in range(tiles):
            rows = slice(tile * tq, (tile + 1) * tq)
            q = aq_ref[:, rows, :].reshape(ATT_GROUP * tq, hd)
            s = _dot_nt(q, ak_ref[...])
            o = weighted_values(jnp.exp2(s - jnp.max(s, axis=-1, keepdims=True)).astype(BF16))
            for g in range(ATT_GROUP):
                ob_ref[rows, g * hd:(g + 1) * hd] = o[g * tq:(g + 1) * tq]


def _mixers(limits, gq, gk, gv, sr, lgf, lgb, gain, aq, ak, av, batch, seq):
    share = GLA_HEADS // ATT_KV_HEADS
    rows = seq // share
    head = functools.partial(_gla_head_spec, seq)
    qspec = pl.BlockSpec((ATT_GROUP, rows, ATT_HEAD_DIM), lambda b, h: (h // share, b * share + h % share, 0))
    kvspec = pl.BlockSpec((None, seq, ATT_HEAD_DIM), lambda b, h: (b, 0, h // share))
    obspec = pl.BlockSpec((None, rows, ATT_GROUP * ATT_HEAD_DIM), lambda b, h: (b, h % share, h // share))
    consts = (*_gla_constants(), _gla_wide_prefix_matrix())
    return pl.pallas_call(
        _mixers_kernel,
        grid=(batch, GLA_HEADS),
        in_specs=[pl.BlockSpec(memory_space=pltpu.SMEM),
                  head(GLA_DK), head(GLA_DK), head(GLA_DV), head(GLA_DV), head(GLA_DK), head(GLA_DK),
                  pl.BlockSpec((None, 1, GLA_DV), lambda b, h: (h, 0, 0))]
                 + [_const_spec(cst.shape) for cst in consts] + [qspec, kvspec, kvspec],
        out_specs=[head(GLA_DV), obspec],
        out_shape=[jax.ShapeDtypeStruct((batch, seq, GLA_VAL_W), BF16),
                   jax.ShapeDtypeStruct((batch, seq, ATT_Q_W), BF16)],
        scratch_shapes=[pltpu.VMEM((seq, 2 * ATT_HEAD_DIM), BF16)] + _gla_scratch(seq // GLA_C, seq),
        compiler_params=_params(2),
        name="mixers",
    )(limits, gq, gk, gv, sr, lgf, lgb, gain, *consts, aq, ak, av)


def _merge_kernel(h_ref, oa_ref, ob_ref, sga_ref, sgb_ref, wa_ref, wb_ref, wo_ref, gpost_ref, o_ref):
    for rows in _row_parts(h_ref, MERGE_ROWS):
        ya = _dot(oa_ref[rows, :], wa_ref[...])
        yb = _dot(ob_ref[rows, :], wb_ref[...])
        merged = sga_ref[rows, :].astype(F32) * ya + sgb_ref[rows, :].astype(F32) * yb
        m = _dot(merged.astype(BF16), wo_ref[...])
        o_ref[rows, :] = h_ref[rows, :] + _rms(m, gpost_ref[...])


def _merge(h, oa, ob, sga, sgb, w_a, w_b, w_o, g_post):
    t = h.shape[0]
    tm = min(MERGE_STEP_ROWS, t)
    row = pl.BlockSpec((tm, D_MODEL), lambda i: (i, 0))
    sq = _const_spec((D_MODEL, D_MODEL))
    return pl.pallas_call(
        _merge_kernel,
        grid=(t // tm,),
        in_specs=[row] * 5 + [sq, sq, sq, _const_spec((1, D_MODEL))],
        out_specs=row,
        out_shape=jax.ShapeDtypeStruct((t, D_MODEL), F32),
        compiler_params=_params(1),
        name="merge",
    )(h, oa, ob, sga, sgb, w_a, w_b, w_o, g_post)


def _split_w_in(w_in):
    gq, gk, gv, gr, za_f, za_b, aq, ak, av, ga, gb = jnp.split(w_in.astype(BF16), IN_OFFSETS, axis=-1)
    pad = jnp.zeros((D_MODEL, V7X_LANES - 2 * GLA_RANK), BF16)
    cat = lambda *parts: jnp.concatenate(parts, axis=-1)
    return cat(za_f, za_b, pad), cat(aq, ak), cat(ga, gb, gr), cat(gq, gk, gv, av)


def _pad_decay_up(up, first_row):
    out = jnp.zeros((V7X_LANES, GLA_KEY_W), BF16)
    return lax.dynamic_update_slice(out, up.astype(BF16), (first_row, 0))


def kernel(x, ffn1_pre_g, ffn1_w_in, ffn1_w_out, ffn1_post_g, mix_pre_g, w_in, gla_decay_up_f,
           gla_decay_bias_f, gla_decay_up_b, gla_decay_bias_b, gla_out_g, w_branch_a, att_q_norm_g,
           att_k_norm_g, w_branch_b, w_out, mix_post_g, ffn2_pre_g, ffn2_w_in, ffn2_w_out, ffn2_post_g):
    batch, seq, d = x.shape
    assert d == D_MODEL and seq % max(GLA_WIDE_C, GRID_W) == 0
    assert seq % ATT_Q_TILE == 0 or seq < ATT_Q_TILE
    assert seq % PROJ_ROW_TILE == 0 or seq < PROJ_ROW_TILE
    depth = w_in.shape[0]
    t = batch * seq
    h = x.reshape(t, d)
    vec = lambda g: g.reshape(1, -1).astype(F32)
    for l in range(depth):
        h = _ffn(h, vec(ffn1_pre_g[l]), ffn1_w_in[l].astype(BF16), ffn1_w_out[l].astype(BF16),
                 vec(ffn1_post_g[l]))
        gq, gk, gv, sr, lgf, lgb, aq, ak, av, sga, sgb, decay_min = _proj(
            h, seq, vec(mix_pre_g[l]), _split_w_in(w_in[l]),
            _pad_decay_up(gla_decay_up_f[l], 0), _pad_decay_up(gla_decay_up_b[l], GLA_RANK),
            vec(gla_decay_bias_f[l]), vec(gla_decay_bias_b[l]),
            vec(att_q_norm_g[l]), vec(att_k_norm_g[l]))
        per_seq = lambda a: a.reshape(batch, seq, a.shape[-1])
        limits = jnp.stack([_score_bound(att_q_norm_g[l], att_k_norm_g[l]), jnp.min(decay_min)]).astype(F32)
        oa, ob = _mixers(limits, per_seq(gq), per_seq(gk), per_seq(gv), per_seq(sr), per_seq(lgf),
                         per_seq(lgb), gla_out_g[l].reshape(GLA_HEADS, 1, GLA_DV).astype(F32),
                         aq, per_seq(ak), per_seq(av), batch, seq)
        h = _merge(h, oa.reshape(t, GLA_VAL_W), ob.reshape(t, ATT_Q_W), sga, sgb,
                   w_branch_a[l].astype(BF16), w_branch_b[l].astype(BF16), w_out[l].astype(BF16),
                   vec(mix_post_g[l]))
        h = _ffn(h, vec(ffn2_pre_g[l]), ffn2_w_in[l].astype(BF16), ffn2_w_out[l].astype(BF16),
                 vec(ffn2_post_g[l]))
    return h.reshape(batch, seq, d)
```

```python
import functools

import numpy as np
import jax
import jax.numpy as jnp
from jax import lax
from jax.experimental import pallas as pl
from jax.experimental.pallas import tpu as pltpu

F32 = jnp.float32
BF16 = jnp.bfloat16

D_MODEL = 1024
GRID_W = 64
D_FF = 2816
EPS = 1e-6
GLA_HEADS = 4
GLA_DK = 128
GLA_DV = 256
GLA_RANK = 16
GLA_TAU = 16.0
ATT_Q_HEADS = 8
ATT_KV_HEADS = 2
ATT_HEAD_DIM = 128
ATT_GROUP = ATT_Q_HEADS // ATT_KV_HEADS
ROPE_THETA = 10000.0
GLA_KEY_W = GLA_HEADS * GLA_DK
GLA_VAL_W = GLA_HEADS * GLA_DV
ATT_Q_W = ATT_Q_HEADS * ATT_HEAD_DIM
ATT_KV_W = ATT_KV_HEADS * ATT_HEAD_DIM
IN_SPLITS = (GLA_KEY_W, GLA_KEY_W, GLA_VAL_W, GLA_VAL_W, GLA_RANK, GLA_RANK,
             ATT_Q_W, ATT_KV_W, ATT_KV_W, D_MODEL, D_MODEL)
IN_OFFSETS = tuple(int(s) for s in np.cumsum(IN_SPLITS)[:-1])
LOG2_E = 1.4426950408889634

V7X_LANES = 128
V7X_VMEM_BYTES = 64 * 1024 * 1024
VMEM_LIMIT = V7X_VMEM_BYTES - 8 * 1024 * 1024

FFN_STEP_ROWS = 512
MERGE_STEP_ROWS = 1024
FFN_ROWS = 256
MERGE_ROWS = 512
PROJ_ROW_TILE = 512
PROJ_ROWS = 256
ATT_Q_TILE = 512
ATT_SHIFT_LIMIT_LOG2 = 60.0
GLA_C = 64
GLA_LEVELS = GLA_C.bit_length() - 1
GLA_SLAB = 64
GLA_WIDE_C = 256
GLA_MILD_LOG2 = 48.0
FF_CHUNKS = ((0, 1024), (1024, 1024), (2048, 768))


def _const_spec(shape):
    zeros = (0,) * len(shape)
    return pl.BlockSpec(shape, lambda *_: zeros, pipeline_mode=pl.Buffered(1))


def _params(n_axes):
    return pltpu.CompilerParams(dimension_semantics=("arbitrary",) * n_axes,
                                vmem_limit_bytes=VMEM_LIMIT)


def _rms(x, g):
    ms = jnp.mean(x * x, axis=-1, keepdims=True)
    return x * lax.rsqrt(ms + EPS) * g


def _dot(a, b):
    return jnp.dot(a, b, preferred_element_type=F32)


def _dot_nt(a, b):
    return lax.dot_general(a, b, (((1,), (1,)), ((), ())), preferred_element_type=F32)


def _dot_tn(a, b):
    return lax.dot_general(a, b, (((0,), (0,)), ((), ())), preferred_element_type=F32)


def _row_parts(ref, rows):
    rows = min(rows, ref.shape[0])
    return [slice(r, r + rows) for r in range(0, ref.shape[0], rows)]


def _ffn_kernel(x_ref, gpre_ref, win_ref, wout_ref, gpost_ref, o_ref):
    for rows in _row_parts(x_ref, FFN_ROWS):
        x = x_ref[rows, :]
        xn = _rms(x, gpre_ref[...]).astype(BF16)
        acc = None
        for c0, cw in FF_CHUNKS:
            gate = _dot(xn, win_ref[:, c0:c0 + cw])
            up = _dot(xn, win_ref[:, D_FF + c0:D_FF + c0 + cw])
            act = (gate * jax.nn.sigmoid(gate) * up).astype(BF16)
            part = _dot(act, wout_ref[c0:c0 + cw, :])
            acc = part if acc is None else acc + part
        o_ref[rows, :] = x + 0.5 * _rms(acc, gpost_ref[...])


def _ffn(x, g_pre, w_in, w_out, g_post):
    t = x.shape[0]
    tm = min(FFN_STEP_ROWS, t)
    row = pl.BlockSpec((tm, D_MODEL), lambda i: (i, 0))
    return pl.pallas_call(
        _ffn_kernel,
        grid=(t // tm,),
        in_specs=[row, _const_spec((1, D_MODEL)), _const_spec((D_MODEL, 2 * D_FF)),
                  _const_spec((D_FF, D_MODEL)), _const_spec((1, D_MODEL))],
        out_specs=row,
        out_shape=jax.ShapeDtypeStruct((t, D_MODEL), F32),
        compiler_params=_params(1),
        name="ffn",
    )(x, g_pre, w_in, w_out, g_post)


def _log_sigmoid(z):
    return jnp.minimum(z, 0.0) - jnp.log(1.0 + jnp.exp(-jnp.abs(z)))


def _rope(x, cos, sin_up, sin_dn):
    return (x * cos + pltpu.roll(x, V7X_LANES - 32, axis=1) * sin_up
            + pltpu.roll(x, 32, axis=1) * sin_dn)


def _proj_kernel(h_ref, gmix_ref, wz_ref, watt_ref, wgate_ref, wplain_ref, upf_ref, upb_ref,
                 bf_ref, bb_ref, qg_ref, kg_ref, cos_ref, sup_ref, sdn_ref,
                 gq_o, gk_o, gv_o, sr_o, lf_o, lb_o, aq_o, ak_o, av_o, sga_o, sgb_o, dmin_o):
    qg = qg_ref[...] * (ATT_HEAD_DIM ** -0.5 * LOG2_E)
    worst = None
    for rows in _row_parts(h_ref, PROJ_ROWS):
        u = _rms(h_ref[rows, :], gmix_ref[...]).astype(BF16)
        za = _dot(u, wz_ref[...]).astype(BF16)

        cos, sup, sdn = cos_ref[rows, :], sup_ref[rows, :], sdn_ref[rows, :]
        att = _dot(u, watt_ref[...])
        for hd in range(ATT_Q_HEADS):
            sl = slice(hd * ATT_HEAD_DIM, (hd + 1) * ATT_HEAD_DIM)
            aq_o[hd, rows, :] = _rope(_rms(att[:, sl], qg), cos, sup, sdn).astype(BF16)
        for hd in range(ATT_KV_HEADS):
            sl = slice(hd * ATT_HEAD_DIM, (hd + 1) * ATT_HEAD_DIM)
            ak_o[rows, sl] = _rope(_rms(att[:, ATT_Q_W + sl.start:ATT_Q_W + sl.stop], kg_ref[...]),
                                   cos, sup, sdn).astype(BF16)

        gate = _dot(u, wgate_ref[...])
        sga_o[rows, :] = jax.nn.sigmoid(gate[:, 0:D_MODEL]).astype(BF16)
        sgb_o[rows, :] = jax.nn.sigmoid(gate[:, D_MODEL:2 * D_MODEL]).astype(BF16)
        gr = gate[:, 2 * D_MODEL:]
        sr_o[rows, :] = (gr * jax.nn.sigmoid(gr)).astype(BF16)
        lf = _log_sigmoid(_dot(za, upf_ref[...]) + bf_ref[...]) * (LOG2_E / GLA_TAU)
        lb = _log_sigmoid(_dot(za, upb_ref[...]) + bb_ref[...]) * (LOG2_E / GLA_TAU)
        lf_o[rows, :] = lf
        lb_o[rows, :] = lb
        totals = jnp.minimum(lf.reshape(-1, GLA_WIDE_C, GLA_KEY_W).sum(axis=1),
                             lb.reshape(-1, GLA_WIDE_C, GLA_KEY_W).sum(axis=1))
        part = jnp.min(totals, axis=0, keepdims=True)
        worst = part if worst is None else jnp.minimum(worst, part)

        plain = _dot(u, wplain_ref[...])
        gq_o[rows, :] = (plain[:, 0:GLA_KEY_W] * (GLA_DK ** -0.5)).astype(BF16)
        gk_o[rows, :] = plain[:, GLA_KEY_W:2 * GLA_KEY_W].astype(BF16)
        gv_o[rows, :] = plain[:, 2 * GLA_KEY_W:2 * GLA_KEY_W + GLA_VAL_W].astype(BF16)
        av_o[rows, :] = plain[:, 2 * GLA_KEY_W + GLA_VAL_W:].astype(BF16)
    worst = functools.reduce(jnp.minimum, [worst[:, l:l + V7X_LANES] for l in range(0, GLA_KEY_W, V7X_LANES)])
    dmin_o[...] = jnp.broadcast_to(worst, dmin_o.shape)


def _rope_tables(seq):
    pos = np.arange(seq)
    half = ATT_HEAD_DIM // 2
    inv_freq = ROPE_THETA ** (-np.arange(0, half, 2, dtype=np.float64) / half)
    ang_r = (pos // GRID_W)[:, None] * inv_freq
    ang_c = (pos % GRID_W)[:, None] * inv_freq
    zero = np.zeros_like(ang_r)
    cos = np.concatenate([np.cos(ang_r)] * 2 + [np.cos(ang_c)] * 2, axis=-1)
    sin_up = np.concatenate([-np.sin(ang_r), zero, -np.sin(ang_c), zero], axis=-1)
    sin_dn = np.concatenate([zero, np.sin(ang_r), zero, np.sin(ang_c)], axis=-1)
    return tuple(jnp.asarray(t, F32) for t in (cos, sin_up, sin_dn))


def _proj(h, seq, g_mix, w_parts, up_f, up_b, bias_f, bias_b, q_gain, k_gain):
    t = h.shape[0]
    tm = min(PROJ_ROW_TILE, seq)
    tiles_per_seq = seq // tm
    cos, sin_up, sin_dn = _rope_tables(seq)

    def row(width):
        return pl.BlockSpec((tm, width), lambda i: (i, 0))

    table = pl.BlockSpec((tm, ATT_HEAD_DIM), lambda i: (i % tiles_per_seq, 0))
    consts_a = (g_mix, *w_parts, up_f, up_b, bias_f, bias_b, q_gain, k_gain)
    out_widths = (GLA_KEY_W, GLA_KEY_W, GLA_VAL_W, GLA_VAL_W, GLA_KEY_W, GLA_KEY_W,
                  None, ATT_KV_W, ATT_KV_W, D_MODEL, D_MODEL)
    out_dtypes = (BF16, BF16, BF16, BF16, F32, F32, BF16, BF16, BF16, BF16, BF16)
    q_heads = pl.BlockSpec((ATT_Q_HEADS, tm, ATT_HEAD_DIM), lambda i: (0, i, 0))
    return pl.pallas_call(
        _proj_kernel,
        grid=(t // tm,),
        in_specs=[row(D_MODEL)] + [_const_spec(c.shape) for c in consts_a] + [table] * 3,
        out_specs=[q_heads if w is None else row(w) for w in out_widths]
                  + [pl.BlockSpec((8, V7X_LANES), lambda i: (i, 0))],
        out_shape=[jax.ShapeDtypeStruct((ATT_Q_HEADS, t, ATT_HEAD_DIM) if w is None else (t, w), d)
                   for w, d in zip(out_widths, out_dtypes)]
                  + [jax.ShapeDtypeStruct((8 * (t // tm), V7X_LANES), F32)],
        compiler_params=_params(1),
        name="proj",
    )(h, *consts_a, cos, sin_up, sin_dn)


def _gla_constants():
    c, nl = GLA_C, GLA_LEVELS
    i = np.arange(c)[:, None]
    m = np.arange(c)[None, :]
    sum_f = [m <= i, m > i]
    sum_b = [m >= i, m < i]
    role_f, role_b, mask_f, mask_b = [], [], [], []
    for lvl in range(nl):
        s = c >> (lvl + 1)
        start = (i // (2 * s)) * 2 * s
        mid = start + s
        upper = (i % (2 * s)) >= s
        sum_f.append(np.where(upper, (m >= mid) & (m <= i), (m > i) & (m < mid)))
        sum_b.append(np.where(upper, (m >= mid) & (m < i), (m >= i) & (m < mid)))
        same = (i // (2 * s)) == (m // (2 * s))
        m_upper = (m % (2 * s)) >= s
        role_f.append(np.broadcast_to(upper, (c, V7X_LANES)))
        role_b.append(np.broadcast_to(~upper, (c, V7X_LANES)))
        mask_f.append(same & upper & ~m_upper)
        mask_b.append(same & ~upper & m_upper)
    sum_f = np.concatenate(sum_f, axis=0).astype(np.float32)
    sum_b = np.concatenate(sum_b, axis=0).astype(np.float32)
    sum_f = np.concatenate([sum_f, sum_f], axis=1)
    sum_b = np.concatenate([sum_b, sum_b], axis=1)
    roles = np.stack(role_f + role_b).astype(np.float32)
    masks = np.stack(mask_f + mask_b + [i == m]).astype(np.float32)
    return jnp.asarray(sum_f, BF16), jnp.asarray(sum_b, BF16), jnp.asarray(roles), jnp.asarray(masks)


def _gla_wide_prefix_matrix():
    wide = np.arange(GLA_WIDE_C)
    tri = (wide[None, :] <= wide[:, None]).astype(np.float32)
    return jnp.asarray(np.concatenate([tri, tri], axis=1), BF16)


def _decay_sums(g, sum_ref, n_rows):
    g_hi = g.astype(BF16)
    g_lo = (g - g_hi.astype(F32)).astype(BF16)
    return _dot(sum_ref[0:n_rows, :], jnp.concatenate([g_hi, g_lo], axis=0))


def _gla_scratch(n_chunks, rows):
    return [pltpu.VMEM((rows, GLA_DV), F32),
            pltpu.VMEM((rows, 2 * GLA_DK), BF16),
            pltpu.VMEM((n_chunks, GLA_DV, 2 * GLA_DK), F32),
            pltpu.VMEM((n_chunks, GLA_DV, 2 * GLA_DK), BF16),
            pltpu.VMEM((n_chunks * 8, 2 * GLA_DK), F32)]


def _gla_store_edges(scratch, i, rows, v, qe_f, qe_b, kd_f, kd_b, total_f, total_b):
    _, qe_ref, upd_ref, _, dec_ref = scratch
    qe_ref[rows, :] = jnp.concatenate([qe_f, qe_b], axis=1).astype(BF16)
    kd = jnp.concatenate([kd_f, kd_b], axis=1).astype(BF16)
    upd_ref[i] = _dot_tn(v, kd)
    total = jnp.concatenate([total_f, total_b], axis=1)
    dec_ref[pl.ds(pl.multiple_of(i * 8, 8), 8), :] = jnp.broadcast_to(total, (8, 2 * GLA_DK))


def _gla_scan_step(scratch, srows, i, j, sf, sb):
    _, _, upd_ref, snap_ref, dec_ref = scratch
    dk = GLA_DK
    snap_ref[i, srows, 0:dk] = sf.astype(BF16)
    snap_ref[j, srows, dk:2 * dk] = sb.astype(BF16)
    dec_f = dec_ref[pl.ds(pl.multiple_of(i * 8, 8), 8), 0:dk]
    dec_b = dec_ref[pl.ds(pl.multiple_of(j * 8, 8), 8), dk:2 * dk]
    sf = (sf.reshape(-1, 8, dk) * dec_f[None]).reshape(GLA_SLAB, dk) + upd_ref[i, srows, 0:dk]
    sb = (sb.reshape(-1, 8, dk) * dec_b[None]).reshape(GLA_SLAB, dk) + upd_ref[j, srows, dk:2 * dk]
    return sf, sb


def _gla_finish(scratch, rows, o_inter, gain_ref, sr_ref, o_ref):
    o = scratch[0][rows, :] + o_inter
    o_ref[rows, :] = (_rms(o, gain_ref[...]) * sr_ref[rows, :].astype(F32)).astype(BF16)


def _gla_any_decay(q_ref, k_ref, v_ref, sr_ref, gf_ref, gb_ref, gain_ref,
                   sumf_ref, sumb_ref, role_ref, mask_ref, o_ref, scratch):
    c, nl, dv = GLA_C, GLA_LEVELS, GLA_DV
    n_chunks = q_ref.shape[0] // c
    acc_ref, qe_ref, _, snap_ref, _ = scratch

    def chunk_any_decay(i, carry):
        rows = pl.ds(pl.multiple_of(i * c, c), c)
        q, k, v = q_ref[rows, :], k_ref[rows, :], v_ref[rows, :]
        qf, kf = q.astype(F32), k.astype(F32)
        ef = jnp.exp2(_decay_sums(gf_ref[rows, :], sumf_ref, (2 + nl) * c))
        eb = jnp.exp2(_decay_sums(gb_ref[rows, :], sumb_ref, (2 + nl) * c))
        a = _dot_nt(q, k) * mask_ref[2 * nl]
        for lvl in range(nl):
            blk = slice((2 + lvl) * c, (3 + lvl) * c)
            xf = (jnp.where(role_ref[lvl] > 0.5, qf, kf) * ef[blk]).astype(BF16)
            a = a + _dot_nt(xf, xf) * mask_ref[lvl]
            xb = (jnp.where(role_ref[nl + lvl] > 0.5, qf, kf) * eb[blk]).astype(BF16)
            a = a + _dot_nt(xb, xb) * mask_ref[nl + lvl]
        acc_ref[rows, :] = _dot(a.astype(BF16), v)
        _gla_store_edges(scratch, i, rows, v, qf * ef[0:c], qf * eb[0:c], kf * ef[c:2 * c], kf * eb[c:2 * c],
                         ef[c - 1:c], eb[0:1])
        return carry

    lax.fori_loop(0, n_chunks, chunk_any_decay, 0)

    for slab in range(dv // GLA_SLAB):
        srows = slice(slab * GLA_SLAB, (slab + 1) * GLA_SLAB)
        zero = jnp.zeros((GLA_SLAB, GLA_DK), F32)
        lax.fori_loop(0, n_chunks,
                      lambda step, carry: _gla_scan_step(scratch, srows, step, n_chunks - 1 - step, *carry),
                      (zero, zero), unroll=2)

    def finish(i, carry):
        rows = pl.ds(pl.multiple_of(i * c, c), c)
        _gla_finish(scratch, rows, _dot_nt(qe_ref[rows, :], snap_ref[i]), gain_ref, sr_ref, o_ref)
        return carry

    lax.fori_loop(0, n_chunks, finish, 0)


def _gla_head_spec(seq, width):
    return pl.BlockSpec((None, seq, width), lambda b, h: (b, 0, h))


def _score_bound(q_gain, k_gain):
    return (jnp.max(jnp.abs(q_gain)) * jnp.max(jnp.abs(k_gain))
            * (1.02 * ATT_HEAD_DIM ** 0.5 * LOG2_E))


class _GlaMild:
    def __init__(self, q_ref, k_ref, v_ref, sr_ref, gf_ref, gb_ref, gain_ref, tri_ref, o_ref, scratch):
        self.refs = (q_ref, k_ref, v_ref, sr_ref, gf_ref, gb_ref, gain_ref, tri_ref, o_ref)
        self.scratch = scratch
        w = GLA_WIDE_C
        self.n = q_ref.shape[0] // w
        self.rows = [slice(i * w, (i + 1) * w) for i in range(self.n)]

    def prefix_sums(self):
        _, _, _, _, gf_ref, gb_ref, _, tri_ref, _ = self.refs
        self.g_b = [gb_ref[r, :] for r in self.rows]
        self.prefix = [_decay_sums(jnp.concatenate([gf_ref[r, :], g], axis=1), tri_ref, GLA_WIDE_C)
                       for r, g in zip(self.rows, self.g_b)]

    def scores(self):
        q_ref, k_ref = self.refs[0:2]
        w, dk = GLA_WIDE_C, GLA_DK
        self.score, self.edge = [], []
        for r, g, p in zip(self.rows, self.g_b, self.prefix):
            qf, kf = q_ref[r, :].astype(F32), k_ref[r, :].astype(F32)
            bf, pb = p[:, 0:dk], p[:, dk:2 * dk]
            bb = (pb[w - 1:w] - pb) + g
            mid_f, mid_b = bf[w // 2 - 1:w // 2], bb[w // 2:w // 2 + 1]
            rf, rb = bf - mid_f, bb - mid_b
            q_f, k_f = qf * jnp.exp2(rf), kf * jnp.exp2(-rf)
            q_b, k_b = qf * jnp.exp2(rb), kf * jnp.exp2(-rb)
            self.score.append((_dot_nt(q_f.astype(BF16), k_f.astype(BF16)),
                               _dot_nt(q_b.astype(BF16), k_b.astype(BF16))))
            self.edge.append((q_f * jnp.exp2(mid_f), q_b * jnp.exp2(mid_b),
                              k_f * jnp.exp2(bf[w - 1:w] - mid_f), k_b * jnp.exp2(bb[0:1] - mid_b),
                              jnp.exp2(bf[w - 1:w]), jnp.exp2(bb[0:1])))

    def intra(self):
        v_ref, w = self.refs[2], GLA_WIDE_C
        lower = lax.broadcasted_iota(jnp.int32, (w, w), 0) >= lax.broadcasted_iota(jnp.int32, (w, w), 1)
        for r, (s_f, s_b) in zip(self.rows, self.score):
            self.scratch[0][r, :] = _dot(jnp.where(lower, s_f, s_b).astype(BF16), v_ref[r, :])

    def edges(self):
        v_ref = self.refs[2]
        for i, (r, parts) in enumerate(zip(self.rows, self.edge)):
            _gla_store_edges(self.scratch, i, r, v_ref[r, :], *parts)

    def scan(self):
        for slab in range(GLA_DV // GLA_SLAB):
            srows = slice(slab * GLA_SLAB, (slab + 1) * GLA_SLAB)
            sf = sb = jnp.zeros((GLA_SLAB, GLA_DK), F32)
            for step in range(self.n):
                sf, sb = _gla_scan_step(self.scratch, srows, step, self.n - 1 - step, sf, sb)

    def inter(self):
        qe_ref, snap_ref = self.scratch[1], self.scratch[3]
        self.o_inter = [_dot_nt(qe_ref[r, :], snap_ref[i]) for i, r in enumerate(self.rows)]

    def finish(self):
        sr_ref, gain_ref, o_ref = self.refs[3], self.refs[6], self.refs[8]
        for r, o_inter in zip(self.rows, self.o_inter):
            _gla_finish(self.scratch, r, o_inter, gain_ref, sr_ref, o_ref)


def _mixers_kernel(limits_ref, gq_ref, gk_ref, gv_ref, sr_ref, gf_ref, gb_ref, gain_ref,
                   sumf_ref, sumb_ref, role_ref, mask_ref, tri_ref, aq_ref, ak_ref, av_ref,
                   oa_ref, ob_ref, vext_ref, *scratch):
    hd, tq = ATT_HEAD_DIM, min(ATT_Q_TILE, aq_ref.shape[1])
    tiles = aq_ref.shape[1] // tq
    bound = limits_ref[0]
    vext_ref[:, :hd] = av_ref[...]
    vext_ref[:, hd:] = jnp.ones((av_ref.shape[0], hd), BF16)

    def weighted_values(p):
        o_ext = _dot(p, vext_ref[...])
        return (o_ext[:, :hd] / o_ext[:, hd:]).astype(BF16)

    mild = jnp.logical_and(bound <= ATT_SHIFT_LIMIT_LOG2, limits_ref[1] >= -GLA_MILD_LOG2)

    @pl.when(mild)
    def _():
        def attention_unit(tile, g):
            rows = slice(tile * tq, (tile + 1) * tq)
            p = jnp.exp2(_dot_nt(aq_ref[g, rows, :], ak_ref[...]) - bound).astype(BF16)
            ob_ref[rows, g * hd:(g + 1) * hd] = weighted_values(p)

        units = [(tile, g) for tile in range(tiles) for g in range(ATT_GROUP)]
        gla = _GlaMild(gq_ref, gk_ref, gv_ref, sr_ref, gf_ref, gb_ref, gain_ref, tri_ref, oa_ref, scratch)
        stages = [gla.prefix_sums, gla.scores, gla.intra, gla.edges, gla.scan, gla.inter, gla.finish]
        per_stage = -(-len(units) // (len(stages) + 1))
        for stage in stages:
            for unit in units[:per_stage]:
                attention_unit(*unit)
            units = units[per_stage:]
            stage()
        for unit in units:
            attention_unit(*unit)

    @pl.when(jnp.logical_not(mild))
    def _():
        _gla_any_decay(gq_ref, gk_ref, gv_ref, sr_ref, gf_ref, gb_ref, gain_ref,
                       sumf_ref, sumb_ref, role_ref, mask_ref, oa_ref, scratch)
        for tile in range(tiles):
            rows = slice(tile * tq, (tile + 1) * tq)
            q = aq_ref[:, rows, :].reshape(ATT_GROUP * tq, hd)
            s = _dot_nt(q, ak_ref[...])
            o = weighted_values(jnp.exp2(s - jnp.max(s, axis=-1, keepdims=True)).astype(BF16))
            for g in range(ATT_GROUP):
                ob_ref[rows, g * hd:(g + 1) * hd] = o[g * tq:(g + 1) * tq]


def _mixers(limits, gq, gk, gv, sr, lgf, lgb, gain, aq, ak, av, batch, seq):
    share = GLA_HEADS // ATT_KV_HEADS
    rows = seq // share
    head = functools.partial(_gla_head_spec, seq)
    qspec = pl.BlockSpec((ATT_GROUP, rows, ATT_HEAD_DIM), lambda b, h: (h // share, b * share + h % share, 0))
    kvspec = pl.BlockSpec((None, seq, ATT_HEAD_DIM), lambda b, h: (b, 0, h // share))
    obspec = pl.BlockSpec((None, rows, ATT_GROUP * ATT_HEAD_DIM), lambda b, h: (b, h % share, h // share))
    consts = (*_gla_constants(), _gla_wide_prefix_matrix())
    return pl.pallas_call(
        _mixers_kernel,
        grid=(batch, GLA_HEADS),
        in_specs=[pl.BlockSpec(memory_space=pltpu.SMEM),
                  head(GLA_DK), head(GLA_DK), head(GLA_DV), head(GLA_DV), head(GLA_DK), head(GLA_DK),
                  pl.BlockSpec((None, 1, GLA_DV), lambda b, h: (h, 0, 0))]
                 + [_const_spec(cst.shape) for cst in consts] + [qspec, kvspec, kvspec],
        out_specs=[head(GLA_DV), obspec],
        out_shape=[jax.ShapeDtypeStruct((batch, seq, GLA_VAL_W), BF16),
                   jax.ShapeDtypeStruct((batch, seq, ATT_Q_W), BF16)],
        scratch_shapes=[pltpu.VMEM((seq, 2 * ATT_HEAD_DIM), BF16)] + _gla_scratch(seq // GLA_C, seq),
        compiler_params=_params(2),
        name="mixers",
    )(limits, gq, gk, gv, sr, lgf, lgb, gain, *consts, aq, ak, av)


def _merge_kernel(h_ref, oa_ref, ob_ref, sga_ref, sgb_ref, wa_ref, wb_ref, wo_ref, gpost_ref, o_ref):
    for rows in _row_parts(h_ref, MERGE_ROWS):
        ya = _dot(oa_ref[rows, :], wa_ref[...])
        yb = _dot(ob_ref[rows, :], wb_ref[...])
        merged = sga_ref[rows, :].astype(F32) * ya + sgb_ref[rows, :].astype(F32) * yb
        m = _dot(merged.astype(BF16), wo_ref[...])
        o_ref[rows, :] = h_ref[rows, :] + _rms(m, gpost_ref[...])


def _merge(h, oa, ob, sga, sgb, w_a, w_b, w_o, g_post):
    t = h.shape[0]
    tm = min(MERGE_STEP_ROWS, t)
    row = pl.BlockSpec((tm, D_MODEL), lambda i: (i, 0))
    sq = _const_spec((D_MODEL, D_MODEL))
    return pl.pallas_call(
        _merge_kernel,
        grid=(t // tm,),
        in_specs=[row] * 5 + [sq, sq, sq, _const_spec((1, D_MODEL))],
        out_specs=row,
        out_shape=jax.ShapeDtypeStruct((t, D_MODEL), F32),
        compiler_params=_params(1),
        name="merge",
    )(h, oa, ob, sga, sgb, w_a, w_b, w_o, g_post)


def _split_w_in(w_in):
    gq, gk, gv, gr, za_f, za_b, aq, ak, av, ga, gb = jnp.split(w_in.astype(BF16), IN_OFFSETS, axis=-1)
    pad = jnp.zeros((D_MODEL, V7X_LANES - 2 * GLA_RANK), BF16)
    cat = lambda *parts: jnp.concatenate(parts, axis=-1)
    return cat(za_f, za_b, pad), cat(aq, ak), cat(ga, gb, gr), cat(gq, gk, gv, av)


def _pad_decay_up(up, first_row):
    out = jnp.zeros((V7X_LANES, GLA_KEY_W), BF16)
    return lax.dynamic_update_slice(out, up.astype(BF16), (first_row, 0))


def kernel(x, ffn1_pre_g, ffn1_w_in, ffn1_w_out, ffn1_post_g, mix_pre_g, w_in, gla_decay_up_f,
           gla_decay_bias_f, gla_decay_up_b, gla_decay_bias_b, gla_out_g, w_branch_a, att_q_norm_g,
           att_k_norm_g, w_branch_b, w_out, mix_post_g, ffn2_pre_g, ffn2_w_in, ffn2_w_out, ffn2_post_g):
    batch, seq, d = x.shape
    assert d == D_MODEL and seq % max(GLA_WIDE_C, GRID_W) == 0
    assert seq % ATT_Q_TILE == 0 or seq < ATT_Q_TILE
    assert seq % PROJ_ROW_TILE == 0 or seq < PROJ_ROW_TILE
    depth = w_in.shape[0]
    t = batch * seq
    h = x.reshape(t, d)
    vec = lambda g: g.reshape(1, -1).astype(F32)
    for l in range(depth):
        h = _ffn(h, vec(ffn1_pre_g[l]), ffn1_w_in[l].astype(BF16), ffn1_w_out[l].astype(BF16),
                 vec(ffn1_post_g[l]))
        gq, gk, gv, sr, lgf, lgb, aq, ak, av, sga, sgb, decay_min = _proj(
            h, seq, vec(mix_pre_g[l]), _split_w_in(w_in[l]),
            _pad_decay_up(gla_decay_up_f[l], 0), _pad_decay_up(gla_decay_up_b[l], GLA_RANK),
            vec(gla_decay_bias_f[l]), vec(gla_decay_bias_b[l]),
            vec(att_q_norm_g[l]), vec(att_k_norm_g[l]))
        per_seq = lambda a: a.reshape(batch, seq, a.shape[-1])
        limits = jnp.stack([_score_bound(att_q_norm_g[l], att_k_norm_g[l]), jnp.min(decay_min)]).astype(F32)
        oa, ob = _mixers(limits, per_seq(gq), per_seq(gk), per_seq(gv), per_seq(sr), per_seq(lgf),
                         per_seq(lgb), gla_out_g[l].reshape(GLA_HEADS, 1, GLA_DV).astype(F32),
                         aq, per_seq(ak), per_seq(av), batch, seq)
        h = _merge(h, oa.reshape(t, GLA_VAL_W), ob.reshape(t, ATT_Q_W), sga, sgb,
                   w_branch_a[l].astype(BF16), w_branch_b[l].astype(BF16), w_out[l].astype(BF16),
                   vec(mix_post_g[l]))
        h = _ffn(h, vec(ffn2_pre_g[l]), ffn2_w_in[l].astype(BF16), ffn2_w_out[l].astype(BF16),
                 vec(ffn2_post_g[l]))
    return h.reshape(batch, seq, d)
```

```python
import functools

import numpy as np
import jax
import jax.numpy as jnp
from jax import lax
from jax.experimental import pallas as pl
from jax.experimental.pallas import tpu as pltpu

F32 = jnp.float32
BF16 = jnp.bfloat16

D_MODEL = 1024
GRID_W = 64
D_FF = 2816
EPS = 1e-6
GLA_HEADS = 4
GLA_DK = 128
GLA_DV = 256
GLA_RANK = 16
GLA_TAU = 16.0
ATT_Q_HEADS = 8
ATT_KV_HEADS = 2
ATT_HEAD_DIM = 128
ATT_GROUP = ATT_Q_HEADS // ATT_KV_HEADS
ROPE_THETA = 10000.0
GLA_KEY_W = GLA_HEADS * GLA_DK
GLA_VAL_W = GLA_HEADS * GLA_DV
ATT_Q_W = ATT_Q_HEADS * ATT_HEAD_DIM
ATT_KV_W = ATT_KV_HEADS * ATT_HEAD_DIM
IN_SPLITS = (GLA_KEY_W, GLA_KEY_W, GLA_VAL_W, GLA_VAL_W, GLA_RANK, GLA_RANK,
             ATT_Q_W, ATT_KV_W, ATT_KV_W, D_MODEL, D_MODEL)
IN_OFFSETS = tuple(int(s) for s in np.cumsum(IN_SPLITS)[:-1])
LOG2_E = 1.4426950408889634

V7X_LANES = 128
V7X_VMEM_BYTES = 64 * 1024 * 1024
VMEM_LIMIT = V7X_VMEM_BYTES - 8 * 1024 * 1024

FFN_STEP_ROWS = 1024
MERGE_STEP_ROWS = 512
FFN_ROWS = 256
PROJ_ROW_TILE = 512
PROJ_ROWS = 256
ATT_Q_TILE = 512
ATT_SHIFT_LIMIT_LOG2 = 60.0
GLA_C = 64
GLA_LEVELS = GLA_C.bit_length() - 1
GLA_SLAB = 64
GLA_WIDE_C = 256
GLA_MILD_LOG2 = 48.0
FF_CHUNKS = ((0, 1024), (1024, 1024), (2048, 768))


def _const_spec(shape):
    zeros = (0,) * len(shape)
    return pl.BlockSpec(shape, lambda *_: zeros, pipeline_mode=pl.Buffered(1))


def _params(n_axes):
    return pltpu.CompilerParams(dimension_semantics=("arbitrary",) * n_axes,
                                vmem_limit_bytes=VMEM_LIMIT)


def _rms(x, g):
    ms = jnp.mean(x * x, axis=-1, keepdims=True)
    return x * lax.rsqrt(ms + EPS) * g


def _dot(a, b):
    return jnp.dot(a, b, preferred_element_type=F32)


def _dot_nt(a, b):
    return lax.dot_general(a, b, (((1,), (1,)), ((), ())), preferred_element_type=F32)


def _dot_tn(a, b):
    return lax.dot_general(a, b, (((0,), (0,)), ((), ())), preferred_element_type=F32)


def _row_parts(ref, rows):
    rows = min(rows, ref.shape[0])
    return [slice(r, r + rows) for r in range(0, ref.shape[0], rows)]


def _ffn_rows(x, gpre_ref, win_ref, wout_ref, gpost_ref):
    xn = _rms(x, gpre_ref[...]).astype(BF16)
    acc = None
    for c0, cw in FF_CHUNKS:
        gate = _dot(xn, win_ref[:, c0:c0 + cw])
        up = _dot(xn, win_ref[:, D_FF + c0:D_FF + c0 + cw])
        act = (gate * jax.nn.sigmoid(gate) * up).astype(BF16)
        part = _dot(act, wout_ref[c0:c0 + cw, :])
        acc = part if acc is None else acc + part
    return x + 0.5 * _rms(acc, gpost_ref[...])


def _ffn_kernel(x_ref, gpre_ref, win_ref, wout_ref, gpost_ref, o_ref):
    for rows in _row_parts(x_ref, FFN_ROWS):
        o_ref[rows, :] = _ffn_rows(x_ref[rows, :], gpre_ref, win_ref, wout_ref, gpost_ref)


def _ffn(x, g_pre, w_in, w_out, g_post):
    t = x.shape[0]
    tm = min(FFN_STEP_ROWS, t)
    row = pl.BlockSpec((tm, D_MODEL), lambda i: (i, 0))
    return pl.pallas_call(
        _ffn_kernel,
        grid=(t // tm,),
        in_specs=[row, _const_spec((1, D_MODEL)), _const_spec((D_MODEL, 2 * D_FF)),
                  _const_spec((D_FF, D_MODEL)), _const_spec((1, D_MODEL))],
        out_specs=row,
        out_shape=jax.ShapeDtypeStruct((t, D_MODEL), F32),
        compiler_params=_params(1),
        name="ffn",
    )(x, g_pre, w_in, w_out, g_post)


def _log_sigmoid(z):
    return jnp.minimum(z, 0.0) - jnp.log(1.0 + jnp.exp(-jnp.abs(z)))


def _rope(x, cos, sin_up, sin_dn):
    return (x * cos + pltpu.roll(x, V7X_LANES - 32, axis=1) * sin_up
            + pltpu.roll(x, 32, axis=1) * sin_dn)


def _proj_kernel(h_ref, gmix_ref, wz_ref, watt_ref, wgate_ref, wplain_ref, upf_ref, upb_ref,
                 bf_ref, bb_ref, qg_ref, kg_ref, cos_ref, sup_ref, sdn_ref,
                 gq_o, gk_o, gv_o, sr_o, lf_o, lb_o, aq_o, ak_o, av_o, sga_o, sgb_o, dmin_o):
    qg = qg_ref[...] * (ATT_HEAD_DIM ** -0.5 * LOG2_E)
    worst = None
    for rows in _row_parts(h_ref, PROJ_ROWS):
        u = _rms(h_ref[rows, :], gmix_ref[...]).astype(BF16)
        za = _dot(u, wz_ref[...]).astype(BF16)

        cos, sup, sdn = cos_ref[rows, :], sup_ref[rows, :], sdn_ref[rows, :]
        att = _dot(u, watt_ref[...])
        for hd in range(ATT_Q_HEADS):
            sl = slice(hd * ATT_HEAD_DIM, (hd + 1) * ATT_HEAD_DIM)
            aq_o[hd, rows, :] = _rope(_rms(att[:, sl], qg), cos, sup, sdn).astype(BF16)
        for hd in range(ATT_KV_HEADS):
            sl = slice(hd * ATT_HEAD_DIM, (hd + 1) * ATT_HEAD_DIM)
            ak_o[rows, sl] = _rope(_rms(att[:, ATT_Q_W + sl.start:ATT_Q_W + sl.stop], kg_ref[...]),
                                   cos, sup, sdn).astype(BF16)

        gate = _dot(u, wgate_ref[...])
        sga_o[rows, :] = jax.nn.sigmoid(gate[:, 0:D_MODEL]).astype(BF16)
        sgb_o[rows, :] = jax.nn.sigmoid(gate[:, D_MODEL:2 * D_MODEL]).astype(BF16)
        gr = gate[:, 2 * D_MODEL:]
        sr_o[rows, :] = (gr * jax.nn.sigmoid(gr)).astype(BF16)
        lf = _log_sigmoid(_dot(za, upf_ref[...]) + bf_ref[...]) * (LOG2_E / GLA_TAU)
        lb = _log_sigmoid(_dot(za, upb_ref[...]) + bb_ref[...]) * (LOG2_E / GLA_TAU)
        lf_o[rows, :] = lf
        lb_o[rows, :] = lb
        totals = jnp.minimum(lf.reshape(-1, GLA_WIDE_C, GLA_KEY_W).sum(axis=1),
                             lb.reshape(-1, GLA_WIDE_C, GLA_KEY_W).sum(axis=1))
        part = jnp.min(totals, axis=0, keepdims=True)
        worst = part if worst is None else jnp.minimum(worst, part)

        plain = _dot(u, wplain_ref[...])
        gq_o[rows, :] = (plain[:, 0:GLA_KEY_W] * (GLA_DK ** -0.5)).astype(BF16)
        gk_o[rows, :] = plain[:, GLA_KEY_W:2 * GLA_KEY_W].astype(BF16)
        gv_o[rows, :] = plain[:, 2 * GLA_KEY_W:2 * GLA_KEY_W + GLA_VAL_W].astype(BF16)
        av_o[rows, :] = plain[:, 2 * GLA_KEY_W + GLA_VAL_W:].astype(BF16)
    worst = functools.reduce(jnp.minimum, [worst[:, l:l + V7X_LANES] for l in range(0, GLA_KEY_W, V7X_LANES)])
    dmin_o[...] = jnp.broadcast_to(worst, dmin_o.shape)


def _rope_tables(seq):
    pos = np.arange(seq)
    half = ATT_HEAD_DIM // 2
    inv_freq = ROPE_THETA ** (-np.arange(0, half, 2, dtype=np.float64) / half)
    ang_r = (pos // GRID_W)[:, None] * inv_freq
    ang_c = (pos % GRID_W)[:, None] * inv_freq
    zero = np.zeros_like(ang_r)
    cos = np.concatenate([np.cos(ang_r)] * 2 + [np.cos(ang_c)] * 2, axis=-1)
    sin_up = np.concatenate([-np.sin(ang_r), zero, -np.sin(ang_c), zero], axis=-1)
    sin_dn = np.concatenate([zero, np.sin(ang_r), zero, np.sin(ang_c)], axis=-1)
    return tuple(jnp.asarray(t, F32) for t in (cos, sin_up, sin_dn))


def _proj(h, seq, g_mix, w_parts, up_f, up_b, bias_f, bias_b, q_gain, k_gain):
    t = h.shape[0]
    tm = min(PROJ_ROW_TILE, seq)
    tiles_per_seq = seq // tm
    cos, sin_up, sin_dn = _rope_tables(seq)

    def row(width):
        return pl.BlockSpec((tm, width), lambda i: (i, 0))

    table = pl.BlockSpec((tm, ATT_HEAD_DIM), lambda i: (i % tiles_per_seq, 0))
    consts_a = (g_mix, *w_parts, up_f, up_b, bias_f, bias_b, q_gain, k_gain)
    out_widths = (GLA_KEY_W, GLA_KEY_W, GLA_VAL_W, GLA_VAL_W, GLA_KEY_W, GLA_KEY_W,
                  None, ATT_KV_W, ATT_KV_W, D_MODEL, D_MODEL)
    out_dtypes = (BF16, BF16, BF16, BF16, F32, F32, BF16, BF16, BF16, BF16, BF16)
    q_heads = pl.BlockSpec((ATT_Q_HEADS, tm, ATT_HEAD_DIM), lambda i: (0, i, 0))
    return pl.pallas_call(
        _proj_kernel,
        grid=(t // tm,),
        in_specs=[row(D_MODEL)] + [_const_spec(c.shape) for c in consts_a] + [table] * 3,
        out_specs=[q_heads if w is None else row(w) for w in out_widths]
                  + [pl.BlockSpec((8, V7X_LANES), lambda i: (i, 0))],
        out_shape=[jax.ShapeDtypeStruct((ATT_Q_HEADS, t, ATT_HEAD_DIM) if w is None else (t, w), d)
                   for w, d in zip(out_widths, out_dtypes)]
                  + [jax.ShapeDtypeStruct((8 * (t // tm), V7X_LANES), F32)],
        compiler_params=_params(1),
        name="proj",
    )(h, *consts_a, cos, sin_up, sin_dn)


def _gla_constants():
    c, nl = GLA_C, GLA_LEVELS
    i = np.arange(c)[:, None]
    m = np.arange(c)[None, :]
    sum_f = [m <= i, m > i]
    sum_b = [m >= i, m < i]
    role_f, role_b, mask_f, mask_b = [], [], [], []
    for lvl in range(nl):
        s = c >> (lvl + 1)
        start = (i // (2 * s)) * 2 * s
        mid = start + s
        upper = (i % (2 * s)) >= s
        sum_f.append(np.where(upper, (m >= mid) & (m <= i), (m > i) & (m < mid)))
        sum_b.append(np.where(upper, (m >= mid) & (m < i), (m >= i) & (m < mid)))
        same = (i // (2 * s)) == (m // (2 * s))
        m_upper = (m % (2 * s)) >= s
        role_f.append(np.broadcast_to(upper, (c, V7X_LANES)))
        role_b.append(np.broadcast_to(~upper, (c, V7X_LANES)))
        mask_f.append(same & upper & ~m_upper)
        mask_b.append(same & ~upper & m_upper)
    sum_f = np.concatenate(sum_f, axis=0).astype(np.float32)
    sum_b = np.concatenate(sum_b, axis=0).astype(np.float32)
    sum_f = np.concatenate([sum_f, sum_f], axis=1)
    sum_b = np.concatenate([sum_b, sum_b], axis=1)
    roles = np.stack(role_f + role_b).astype(np.float32)
    masks = np.stack(mask_f + mask_b + [i == m]).astype(np.float32)
    return jnp.asarray(sum_f, BF16), jnp.asarray(sum_b, BF16), jnp.asarray(roles), jnp.asarray(masks)


def _gla_wide_prefix_matrix():
    wide = np.arange(GLA_WIDE_C)
    tri = (wide[None, :] <= wide[:, None]).astype(np.float32)
    return jnp.asarray(np.concatenate([tri, tri], axis=1), BF16)


def _decay_sums(g, sum_ref, n_rows):
    g_hi = g.astype(BF16)
    g_lo = (g - g_hi.astype(F32)).astype(BF16)
    return _dot(sum_ref[0:n_rows, :], jnp.concatenate([g_hi, g_lo], axis=0))


def _gla_scratch(n_chunks, rows):
    return [pltpu.VMEM((rows, GLA_DV), F32),
            pltpu.VMEM((rows, 2 * GLA_DK), BF16),
            pltpu.VMEM((n_chunks, GLA_DV, 2 * GLA_DK), F32),
            pltpu.VMEM((n_chunks, GLA_DV, 2 * GLA_DK), BF16),
            pltpu.VMEM((n_chunks * 8, 2 * GLA_DK), F32)]


def _gla_store_edges(scratch, i, rows, v, qe_f, qe_b, kd_f, kd_b, total_f, total_b):
    _, qe_ref, upd_ref, _, dec_ref = scratch
    qe_ref[rows, :] = jnp.concatenate([qe_f, qe_b], axis=1).astype(BF16)
    kd = jnp.concatenate([kd_f, kd_b], axis=1).astype(BF16)
    upd_ref[i] = _dot_tn(v, kd)
    total = jnp.concatenate([total_f, total_b], axis=1)
    dec_ref[pl.ds(pl.multiple_of(i * 8, 8), 8), :] = jnp.broadcast_to(total, (8, 2 * GLA_DK))


def _gla_scan_step(scratch, srows, i, j, sf, sb):
    _, _, upd_ref, snap_ref, dec_ref = scratch
    dk = GLA_DK
    snap_ref[i, srows, 0:dk] = sf.astype(BF16)
    snap_ref[j, srows, dk:2 * dk] = sb.astype(BF16)
    dec_f = dec_ref[pl.ds(pl.multiple_of(i * 8, 8), 8), 0:dk]
    dec_b = dec_ref[pl.ds(pl.multiple_of(j * 8, 8), 8), dk:2 * dk]
    sf = (sf.reshape(-1, 8, dk) * dec_f[None]).reshape(GLA_SLAB, dk) + upd_ref[i, srows, 0:dk]
    sb = (sb.reshape(-1, 8, dk) * dec_b[None]).reshape(GLA_SLAB, dk) + upd_ref[j, srows, dk:2 * dk]
    return sf, sb


def _gla_finish(scratch, rows, o_inter, gain_ref, sr_ref, o_ref):
    o = scratch[0][rows, :] + o_inter
    o_ref[rows, :] = (_rms(o, gain_ref[...]) * sr_ref[rows, :].astype(F32)).astype(BF16)


def _gla_any_decay(q_ref, k_ref, v_ref, sr_ref, gf_ref, gb_ref, gain_ref,
                   sumf_ref, sumb_ref, role_ref, mask_ref, o_ref, scratch):
    c, nl, dv = GLA_C, GLA_LEVELS, GLA_DV
    n_chunks = q_ref.shape[0] // c
    acc_ref, qe_ref, _, snap_ref, _ = scratch

    def chunk_any_decay(i, carry):
        rows = pl.ds(pl.multiple_of(i * c, c), c)
        q, k, v = q_ref[rows, :], k_ref[rows, :], v_ref[rows, :]
        qf, kf = q.astype(F32), k.astype(F32)
        ef = jnp.exp2(_decay_sums(gf_ref[rows, :], sumf_ref, (2 + nl) * c))
        eb = jnp.exp2(_decay_sums(gb_ref[rows, :], sumb_ref, (2 + nl) * c))
        a = _dot_nt(q, k) * mask_ref[2 * nl]
        for lvl in range(nl):
            blk = slice((2 + lvl) * c, (3 + lvl) * c)
            xf = (jnp.where(role_ref[lvl] > 0.5, qf, kf) * ef[blk]).astype(BF16)
            a = a + _dot_nt(xf, xf) * mask_ref[lvl]
            xb = (jnp.where(role_ref[nl + lvl] > 0.5, qf, kf) * eb[blk]).astype(BF16)
            a = a + _dot_nt(xb, xb) * mask_ref[nl + lvl]
        acc_ref[rows, :] = _dot(a.astype(BF16), v)
        _gla_store_edges(scratch, i, rows, v, qf * ef[0:c], qf * eb[0:c], kf * ef[c:2 * c], kf * eb[c:2 * c],
                         ef[c - 1:c], eb[0:1])
        return carry

    lax.fori_loop(0, n_chunks, chunk_any_decay, 0)

    for slab in range(dv // GLA_SLAB):
        srows = slice(slab * GLA_SLAB, (slab + 1) * GLA_SLAB)
        zero = jnp.zeros((GLA_SLAB, GLA_DK), F32)
        lax.fori_loop(0, n_chunks,
                      lambda step, carry: _gla_scan_step(scratch, srows, step, n_chunks - 1 - step, *carry),
                      (zero, zero), unroll=2)

    def finish(i, carry):
        rows = pl.ds(pl.multiple_of(i * c, c), c)
        _gla_finish(scratch, rows, _dot_nt(qe_ref[rows, :], snap_ref[i]), gain_ref, sr_ref, o_ref)
        return carry

    lax.fori_loop(0, n_chunks, finish, 0)


def _gla_head_spec(seq, width):
    return pl.BlockSpec((None, seq, width), lambda b, h: (b, 0, h))


def _score_bound(q_gain, k_gain):
    return (jnp.max(jnp.abs(q_gain)) * jnp.max(jnp.abs(k_gain))
            * (1.02 * ATT_HEAD_DIM ** 0.5 * LOG2_E))


class _GlaMild:
    def __init__(self, q_ref, k_ref, v_ref, sr_ref, gf_ref, gb_ref, gain_ref, tri_ref, o_ref, scratch):
        self.refs = (q_ref, k_ref, v_ref, sr_ref, gf_ref, gb_ref, gain_ref, tri_ref, o_ref)
        self.scratch = scratch
        w = GLA_WIDE_C
        self.n = q_ref.shape[0] // w
        self.rows = [slice(i * w, (i + 1) * w) for i in range(self.n)]

    def prefix_sums(self):
        _, _, _, _, gf_ref, gb_ref, _, tri_ref, _ = self.refs
        self.g_b = [gb_ref[r, :] for r in self.rows]
        self.prefix = [_decay_sums(jnp.concatenate([gf_ref[r, :], g], axis=1), tri_ref, GLA_WIDE_C)
                       for r, g in zip(self.rows, self.g_b)]

    def scores(self):
        q_ref, k_ref = self.refs[0:2]
        w, dk = GLA_WIDE_C, GLA_DK
        self.score, self.edge = [], []
        for r, g, p in zip(self.rows, self.g_b, self.prefix):
            qf, kf = q_ref[r, :].astype(F32), k_ref[r, :].astype(F32)
            bf, pb = p[:, 0:dk], p[:, dk:2 * dk]
            bb = (pb[w - 1:w] - pb) + g
            mid_f, mid_b = bf[w // 2 - 1:w // 2], bb[w // 2:w // 2 + 1]
            rf, rb = bf - mid_f, bb - mid_b
            q_f, k_f = qf * jnp.exp2(rf), kf * jnp.exp2(-rf)
            q_b, k_b = qf * jnp.exp2(rb), kf * jnp.exp2(-rb)
            self.score.append((_dot_nt(q_f.astype(BF16), k_f.astype(BF16)),
                               _dot_nt(q_b.astype(BF16), k_b.astype(BF16))))
            self.edge.append((q_f * jnp.exp2(mid_f), q_b * jnp.exp2(mid_b),
                              k_f * jnp.exp2(bf[w - 1:w] - mid_f), k_b * jnp.exp2(bb[0:1] - mid_b),
                              jnp.exp2(bf[w - 1:w]), jnp.exp2(bb[0:1])))

    def intra(self):
        v_ref, w = self.refs[2], GLA_WIDE_C
        lower = lax.broadcasted_iota(jnp.int32, (w, w), 0) >= lax.broadcasted_iota(jnp.int32, (w, w), 1)
        for r, (s_f, s_b) in zip(self.rows, self.score):
            self.scratch[0][r, :] = _dot(jnp.where(lower, s_f, s_b).astype(BF16), v_ref[r, :])

    def edges(self):
        v_ref = self.refs[2]
        for i, (r, parts) in enumerate(zip(self.rows, self.edge)):
            _gla_store_edges(self.scratch, i, r, v_ref[r, :], *parts)

    def scan(self):
        for slab in range(GLA_DV // GLA_SLAB):
            srows = slice(slab * GLA_SLAB, (slab + 1) * GLA_SLAB)
            sf = sb = jnp.zeros((GLA_SLAB, GLA_DK), F32)
            for step in range(self.n):
                sf, sb = _gla_scan_step(self.scratch, srows, step, self.n - 1 - step, sf, sb)

    def inter(self):
        qe_ref, snap_ref = self.scratch[1], self.scratch[3]
        self.o_inter = [_dot_nt(qe_ref[r, :], snap_ref[i]) for i, r in enumerate(self.rows)]

    def finish(self):
        sr_ref, gain_ref, o_ref = self.refs[3], self.refs[6], self.refs[8]
        for r, o_inter in zip(self.rows, self.o_inter):
            _gla_finish(self.scratch, r, o_inter, gain_ref, sr_ref, o_ref)


def _mixers_kernel(limits_ref, gq_ref, gk_ref, gv_ref, sr_ref, gf_ref, gb_ref, gain_ref,
                   sumf_ref, sumb_ref, role_ref, mask_ref, tri_ref, aq_ref, ak_ref, av_ref,
                   oa_ref, ob_ref, vext_ref, *scratch):
    hd, tq = ATT_HEAD_DIM, min(ATT_Q_TILE, aq_ref.shape[1])
    tiles = aq_ref.shape[1] // tq
    bound = limits_ref[0]
    vext_ref[:, :hd] = av_ref[...]
    vext_ref[:, hd:] = jnp.ones((av_ref.shape[0], hd), BF16)

    def weighted_values(p):
        o_ext = _dot(p, vext_ref[...])
        return (o_ext[:, :hd] / o_ext[:, hd:]).astype(BF16)

    mild = jnp.logical_and(bound <= ATT_SHIFT_LIMIT_LOG2, limits_ref[1] >= -GLA_MILD_LOG2)

    @pl.when(mild)
    def _():
        def attention_unit(tile, g):
            rows = slice(tile * tq, (tile + 1) * tq)
            p = jnp.exp2(_dot_nt(aq_ref[g, rows, :], ak_ref[...]) - bound).astype(BF16)
            ob_ref[rows, g * hd:(g + 1) * hd] = weighted_values(p)

        units = [(tile, g) for tile in range(tiles) for g in range(ATT_GROUP)]
        gla = _GlaMild(gq_ref, gk_ref, gv_ref, sr_ref, gf_ref, gb_ref, gain_ref, tri_ref, oa_ref, scratch)
        stages = [gla.prefix_sums, gla.scores, gla.intra, gla.edges, gla.scan, gla.inter, gla.finish]
        per_stage = -(-len(units) // (len(stages) + 1))
        for stage in stages:
            for unit in units[:per_stage]:
                attention_unit(*unit)
            units = units[per_stage:]
            stage()
        for unit in units:
            attention_unit(*unit)

    @pl.when(jnp.logical_not(mild))
    def _():
        _gla_any_decay(gq_ref, gk_ref, gv_ref, sr_ref, gf_ref, gb_ref, gain_ref,
                       sumf_ref, sumb_ref, role_ref, mask_ref, oa_ref, scratch)
        for tile in range(tiles):
            rows = slice(tile * tq, (tile + 1) * tq)
            q = aq_ref[:, rows, :].reshape(ATT_GROUP * tq, hd)
            s = _dot_nt(q, ak_ref[...])
            o = weighted_values(jnp.exp2(s - jnp.max(s, axis=-1, keepdims=True)).astype(BF16))
            for g in range(ATT_GROUP):
                ob_ref[rows, g * hd:(g + 1) * hd] = o[g * tq:(g + 1) * tq]


def _mixers(limits, gq, gk, gv, sr, lgf, lgb, gain, aq, ak, av, batch, seq):
    share = GLA_HEADS // ATT_KV_HEADS
    rows = seq // share
    head = functools.partial(_gla_head_spec, seq)
    qspec = pl.BlockSpec((ATT_GROUP, rows, ATT_HEAD_DIM), lambda b, h: (h // share, b * share + h % share, 0))
    kvspec = pl.BlockSpec((None, seq, ATT_HEAD_DIM), lambda b, h: (b, 0, h // share))
    obspec = pl.BlockSpec((None, rows, ATT_GROUP * ATT_HEAD_DIM), lambda b, h: (b, h % share, h // share))
    consts = (*_gla_constants(), _gla_wide_prefix_matrix())
    return pl.pallas_call(
        _mixers_kernel,
        grid=(batch, GLA_HEADS),
        in_specs=[pl.BlockSpec(memory_space=pltpu.SMEM),
                  head(GLA_DK), head(GLA_DK), head(GLA_DV), head(GLA_DV), head(GLA_DK), head(GLA_DK),
                  pl.BlockSpec((None, 1, GLA_DV), lambda b, h: (h, 0, 0))]
                 + [_const_spec(cst.shape) for cst in consts] + [qspec, kvspec, kvspec],
        out_specs=[head(GLA_DV), obspec],
        out_shape=[jax.ShapeDtypeStruct((batch, seq, GLA_VAL_W), BF16),
                   jax.ShapeDtypeStruct((batch, seq, ATT_Q_W), BF16)],
        scratch_shapes=[pltpu.VMEM((seq, 2 * ATT_HEAD_DIM), BF16)] + _gla_scratch(seq // GLA_C, seq),
        compiler_params=_params(2),
        name="mixers",
    )(limits, gq, gk, gv, sr, lgf, lgb, gain, *consts, aq, ak, av)


def _merge_ffn_kernel(h_ref, oa_ref, ob_ref, sga_ref, sgb_ref, wa_ref, wb_ref, wo_ref, gmix_ref,
                      gpre_ref, win_ref, wout_ref, gpost_ref, o_ref):
    def merged_rows(rows):
        ya = _dot(oa_ref[rows, :], wa_ref[...])
        yb = _dot(ob_ref[rows, :], wb_ref[...])
        merged = sga_ref[rows, :].astype(F32) * ya + sgb_ref[rows, :].astype(F32) * yb
        m = _dot(merged.astype(BF16), wo_ref[...])
        return h_ref[rows, :] + _rms(m, gmix_ref[...])

    parts = _row_parts(h_ref, FFN_ROWS)
    h2 = merged_rows(parts[0])
    for rows, nxt in zip(parts, parts[1:] + [None]):
        h2_next = merged_rows(nxt) if nxt is not None else None
        o_ref[rows, :] = _ffn_rows(h2, gpre_ref, win_ref, wout_ref, gpost_ref)
        h2 = h2_next


def _merge_ffn(h, oa, ob, sga, sgb, w_a, w_b, w_o, g_mix, g_pre, w_in, w_out, g_post):
    t = h.shape[0]
    tm = min(MERGE_STEP_ROWS, t)
    row = pl.BlockSpec((tm, D_MODEL), lambda i: (i, 0))
    sq = _const_spec((D_MODEL, D_MODEL))
    vec = _const_spec((1, D_MODEL))
    return pl.pallas_call(
        _merge_ffn_kernel,
        grid=(t // tm,),
        in_specs=[row] * 5 + [sq, sq, sq, vec, vec, _const_spec((D_MODEL, 2 * D_FF)),
                              _const_spec((D_FF, D_MODEL)), vec],
        out_specs=row,
        out_shape=jax.ShapeDtypeStruct((t, D_MODEL), F32),
        compiler_params=_params(1),
        name="merge_ffn",
    )(h, oa, ob, sga, sgb, w_a, w_b, w_o, g_mix, g_pre, w_in, w_out, g_post)


def _split_w_in(w_in):
    gq, gk, gv, gr, za_f, za_b, aq, ak, av, ga, gb = jnp.split(w_in.astype(BF16), IN_OFFSETS, axis=-1)
    pad = jnp.zeros((D_MODEL, V7X_LANES - 2 * GLA_RANK), BF16)
    cat = lambda *parts: jnp.concatenate(parts, axis=-1)
    return cat(za_f, za_b, pad), cat(aq, ak), cat(ga, gb, gr), cat(gq, gk, gv, av)


def _pad_decay_up(up, first_row):
    out = jnp.zeros((V7X_LANES, GLA_KEY_W), BF16)
    return lax.dynamic_update_slice(out, up.astype(BF16), (first_row, 0))


def kernel(x, ffn1_pre_g, ffn1_w_in, ffn1_w_out, ffn1_post_g, mix_pre_g, w_in, gla_decay_up_f,
           gla_decay_bias_f, gla_decay_up_b, gla_decay_bias_b, gla_out_g, w_branch_a, att_q_norm_g,
           att_k_norm_g, w_branch_b, w_out, mix_post_g, ffn2_pre_g, ffn2_w_in, ffn2_w_out, ffn2_post_g):
    batch, seq, d = x.shape
    assert d == D_MODEL and seq % max(GLA_WIDE_C, GRID_W) == 0
    assert seq % ATT_Q_TILE == 0 or seq < ATT_Q_TILE
    assert seq % PROJ_ROW_TILE == 0 or seq < PROJ_ROW_TILE
    depth = w_in.shape[0]
    t = batch * seq
    h = x.reshape(t, d)
    vec = lambda g: g.reshape(1, -1).astype(F32)
    for l in range(depth):
        h = _ffn(h, vec(ffn1_pre_g[l]), ffn1_w_in[l].astype(BF16), ffn1_w_out[l].astype(BF16),
                 vec(ffn1_post_g[l]))
        gq, gk, gv, sr, lgf, lgb, aq, ak, av, sga, sgb, decay_min = _proj(
            h, seq, vec(mix_pre_g[l]), _split_w_in(w_in[l]),
            _pad_decay_up(gla_decay_up_f[l], 0), _pad_decay_up(gla_decay_up_b[l], GLA_RANK),
            vec(gla_decay_bias_f[l]), vec(gla_decay_bias_b[l]),
            vec(att_q_norm_g[l]), vec(att_k_norm_g[l]))
        per_seq = lambda a: a.reshape(batch, seq, a.shape[-1])
        limits = jnp.stack([_score_bound(att_q_norm_g[l], att_k_norm_g[l]), jnp.min(decay_min)]).astype(F32)
        oa, ob = _mixers(limits, per_seq(gq), per_seq(gk), per_seq(gv), per_seq(sr), per_seq(lgf),
                         per_seq(lgb), gla_out_g[l].reshape(GLA_HEADS, 1, GLA_DV).astype(F32),
                         aq, per_seq(ak), per_seq(av), batch, seq)
        h = _merge_ffn(h, oa.reshape(t, GLA_VAL_W), ob.reshape(t, ATT_Q_W), sga, sgb,
                       w_branch_a[l].astype(BF16), w_branch_b[l].astype(BF16), w_out[l].astype(BF16),
                       vec(mix_post_g[l]), vec(ffn2_pre_g[l]), ffn2_w_in[l].astype(BF16),
                       ffn2_w_out[l].astype(BF16), vec(ffn2_post_g[l]))
    return h.reshape(batch, seq, d)
```

```python
import functools

import numpy as np
import jax
import jax.numpy as jnp
from jax import lax
from jax.experimental import pallas as pl
from jax.experimental.pallas import tpu as pltpu

F32 = jnp.float32
BF16 = jnp.bfloat16

D_MODEL = 1024
GRID_W = 64
D_FF = 2816
EPS = 1e-6
GLA_HEADS = 4
GLA_DK = 128
GLA_DV = 256
GLA_RANK = 16
GLA_TAU = 16.0
ATT_Q_HEADS = 8
ATT_KV_HEADS = 2
ATT_HEAD_DIM = 128
ATT_GROUP = ATT_Q_HEADS // ATT_KV_HEADS
ROPE_THETA = 10000.0
GLA_KEY_W = GLA_HEADS * GLA_DK
GLA_VAL_W = GLA_HEADS * GLA_DV
ATT_Q_W = ATT_Q_HEADS * ATT_HEAD_DIM
ATT_KV_W = ATT_KV_HEADS * ATT_HEAD_DIM
IN_SPLITS = (GLA_KEY_W, GLA_KEY_W, GLA_VAL_W, GLA_VAL_W, GLA_RANK, GLA_RANK,
             ATT_Q_W, ATT_KV_W, ATT_KV_W, D_MODEL, D_MODEL)
IN_OFFSETS = tuple(int(s) for s in np.cumsum(IN_SPLITS)[:-1])
LOG2_E = 1.4426950408889634

V7X_LANES = 128
V7X_VMEM_BYTES = 64 * 1024 * 1024
VMEM_LIMIT = V7X_VMEM_BYTES - 8 * 1024 * 1024

FFN_STEP_ROWS = 1024
MERGE_STEP_ROWS = 512
FFN_ROWS = 256
PROJ_ROW_TILE = 512
PROJ_ROWS = 256
ATT_Q_TILE = 512
ATT_SHIFT_LIMIT_LOG2 = 60.0
GLA_C = 64
GLA_LEVELS = GLA_C.bit_length() - 1
GLA_SLAB = 64
GLA_WIDE_C = 256
GLA_MILD_LOG2 = 48.0
FF_CHUNKS = ((0, 1024), (1024, 1024), (2048, 768))


def _const_spec(shape):
    zeros = (0,) * len(shape)
    return pl.BlockSpec(shape, lambda *_: zeros, pipeline_mode=pl.Buffered(1))


def _params(n_axes):
    return pltpu.CompilerParams(dimension_semantics=("arbitrary",) * n_axes,
                                vmem_limit_bytes=VMEM_LIMIT)


def _rms(x, g):
    ms = jnp.mean(x * x, axis=-1, keepdims=True)
    return x * lax.rsqrt(ms + EPS) * g


def _dot(a, b):
    return jnp.dot(a, b, preferred_element_type=F32)


def _dot_nt(a, b):
    return lax.dot_general(a, b, (((1,), (1,)), ((), ())), preferred_element_type=F32)


def _dot_tn(a, b):
    return lax.dot_general(a, b, (((0,), (0,)), ((), ())), preferred_element_type=F32)


def _row_parts(ref, rows):
    rows = min(rows, ref.shape[0])
    return [slice(r, r + rows) for r in range(0, ref.shape[0], rows)]


def _ffn_rows(x, gpre_ref, win_ref, wout_ref, gpost_ref):
    xn = _rms(x, gpre_ref[...]).astype(BF16)
    acc = None
    for c0, cw in FF_CHUNKS:
        gate = _dot(xn, win_ref[:, c0:c0 + cw])
        up = _dot(xn, win_ref[:, D_FF + c0:D_FF + c0 + cw])
        act = (gate * jax.nn.sigmoid(gate) * up).astype(BF16)
        part = _dot(act, wout_ref[c0:c0 + cw, :])
        acc = part if acc is None else acc + part
    return x + 0.5 * _rms(acc, gpost_ref[...])


def _ffn_kernel(x_ref, gpre_ref, win_ref, wout_ref, gpost_ref, o_ref):
    for rows in _row_parts(x_ref, FFN_ROWS):
        o_ref[rows, :] = _ffn_rows(x_ref[rows, :], gpre_ref, win_ref, wout_ref, gpost_ref)


def _ffn(x, g_pre, w_in, w_out, g_post):
    t = x.shape[0]
    tm = min(FFN_STEP_ROWS, t)
    row = pl.BlockSpec((tm, D_MODEL), lambda i: (i, 0))
    return pl.pallas_call(
        _ffn_kernel,
        grid=(t // tm,),
        in_specs=[row, _const_spec((1, D_MODEL)), _const_spec((D_MODEL, 2 * D_FF)),
                  _const_spec((D_FF, D_MODEL)), _const_spec((1, D_MODEL))],
        out_specs=row,
        out_shape=jax.ShapeDtypeStruct((t, D_MODEL), F32),
        compiler_params=_params(1),
        name="ffn",
    )(x, g_pre, w_in, w_out, g_post)


def _log_sigmoid(z):
    return jnp.minimum(z, 0.0) - jnp.log(1.0 + jnp.exp(-jnp.abs(z)))


def _rope(x, cos, sin_up, sin_dn):
    return (x * cos + pltpu.roll(x, V7X_LANES - 32, axis=1) * sin_up
            + pltpu.roll(x, 32, axis=1) * sin_dn)


def _store_heads(o_ref, rows, value):
    heads, _, width = o_ref.shape
    for hd in range(heads):
        o_ref[hd, rows, :] = value[:, hd * width:(hd + 1) * width]


def _proj_kernel(h_ref, gmix_ref, wz_ref, watt_ref, wgate_ref, wplain_ref, upf_ref, upb_ref,
                 bf_ref, bb_ref, qg_ref, kg_ref, cos_ref, sup_ref, sdn_ref,
                 gq_o, gk_o, gv_o, sr_o, lf_o, lb_o, aq_o, ak_o, av_o, sga_o, sgb_o, dmin_o):
    qg = qg_ref[...] * (ATT_HEAD_DIM ** -0.5 * LOG2_E)
    worst = None
    for rows in _row_parts(h_ref, PROJ_ROWS):
        u = _rms(h_ref[rows, :], gmix_ref[...]).astype(BF16)
        za = _dot(u, wz_ref[...]).astype(BF16)

        cos, sup, sdn = cos_ref[rows, :], sup_ref[rows, :], sdn_ref[rows, :]
        att = _dot(u, watt_ref[...])
        for hd in range(ATT_Q_HEADS):
            sl = slice(hd * ATT_HEAD_DIM, (hd + 1) * ATT_HEAD_DIM)
            aq_o[hd, rows, :] = _rope(_rms(att[:, sl], qg), cos, sup, sdn).astype(BF16)
        for hd in range(ATT_KV_HEADS):
            sl = slice(hd * ATT_HEAD_DIM, (hd + 1) * ATT_HEAD_DIM)
            ak_o[hd, rows, :] = _rope(_rms(att[:, ATT_Q_W + sl.start:ATT_Q_W + sl.stop], kg_ref[...]),
                                      cos, sup, sdn).astype(BF16)

        gate = _dot(u, wgate_ref[...])
        sga_o[rows, :] = jax.nn.sigmoid(gate[:, 0:D_MODEL]).astype(BF16)
        sgb_o[rows, :] = jax.nn.sigmoid(gate[:, D_MODEL:2 * D_MODEL]).astype(BF16)
        gr = gate[:, 2 * D_MODEL:]
        _store_heads(sr_o, rows, (gr * jax.nn.sigmoid(gr)).astype(BF16))
        lf = _log_sigmoid(_dot(za, upf_ref[...]) + bf_ref[...]) * (LOG2_E / GLA_TAU)
        lb = _log_sigmoid(_dot(za, upb_ref[...]) + bb_ref[...]) * (LOG2_E / GLA_TAU)
        _store_heads(lf_o, rows, lf)
        _store_heads(lb_o, rows, lb)
        totals = jnp.minimum(lf.reshape(-1, GLA_WIDE_C, GLA_KEY_W).sum(axis=1),
                             lb.reshape(-1, GLA_WIDE_C, GLA_KEY_W).sum(axis=1))
        part = jnp.min(totals, axis=0, keepdims=True)
        worst = part if worst is None else jnp.minimum(worst, part)

        plain = _dot(u, wplain_ref[...])
        _store_heads(gq_o, rows, (plain[:, 0:GLA_KEY_W] * (GLA_DK ** -0.5)).astype(BF16))
        _store_heads(gk_o, rows, plain[:, GLA_KEY_W:2 * GLA_KEY_W].astype(BF16))
        _store_heads(gv_o, rows, plain[:, 2 * GLA_KEY_W:2 * GLA_KEY_W + GLA_VAL_W].astype(BF16))
        _store_heads(av_o, rows, plain[:, 2 * GLA_KEY_W + GLA_VAL_W:].astype(BF16))
    worst = functools.reduce(jnp.minimum, [worst[:, l:l + V7X_LANES] for l in range(0, GLA_KEY_W, V7X_LANES)])
    dmin_o[...] = jnp.broadcast_to(worst, dmin_o.shape)


def _rope_tables(seq):
    pos = np.arange(seq)
    half = ATT_HEAD_DIM // 2
    inv_freq = ROPE_THETA ** (-np.arange(0, half, 2, dtype=np.float64) / half)
    ang_r = (pos // GRID_W)[:, None] * inv_freq
    ang_c = (pos % GRID_W)[:, None] * inv_freq
    zero = np.zeros_like(ang_r)
    cos = np.concatenate([np.cos(ang_r)] * 2 + [np.cos(ang_c)] * 2, axis=-1)
    sin_up = np.concatenate([-np.sin(ang_r), zero, -np.sin(ang_c), zero], axis=-1)
    sin_dn = np.concatenate([zero, np.sin(ang_r), zero, np.sin(ang_c)], axis=-1)
    return tuple(jnp.asarray(t, F32) for t in (cos, sin_up, sin_dn))


def _proj(h, seq, g_mix, w_parts, up_f, up_b, bias_f, bias_b, q_gain, k_gain):
    t = h.shape[0]
    tm = min(PROJ_ROW_TILE, seq)
    tiles_per_seq = seq // tm
    cos, sin_up, sin_dn = _rope_tables(seq)

    def row(width):
        return pl.BlockSpec((tm, width), lambda i: (i, 0))

    table = pl.BlockSpec((tm, ATT_HEAD_DIM), lambda i: (i % tiles_per_seq, 0))
    consts_a = (g_mix, *w_parts, up_f, up_b, bias_f, bias_b, q_gain, k_gain)
    outs = ((GLA_HEADS, GLA_DK, BF16), (GLA_HEADS, GLA_DK, BF16), (GLA_HEADS, GLA_DV, BF16),
            (GLA_HEADS, GLA_DV, BF16), (GLA_HEADS, GLA_DK, F32), (GLA_HEADS, GLA_DK, F32),
            (ATT_Q_HEADS, ATT_HEAD_DIM, BF16), (ATT_KV_HEADS, ATT_HEAD_DIM, BF16),
            (ATT_KV_HEADS, ATT_HEAD_DIM, BF16), (None, D_MODEL, BF16), (None, D_MODEL, BF16))
    return pl.pallas_call(
        _proj_kernel,
        grid=(t // tm,),
        in_specs=[row(D_MODEL)] + [_const_spec(c.shape) for c in consts_a] + [table] * 3,
        out_specs=[row(w) if hds is None else pl.BlockSpec((hds, tm, w), lambda i: (0, i, 0))
                   for hds, w, _ in outs] + [pl.BlockSpec((8, V7X_LANES), lambda i: (i, 0))],
        out_shape=[jax.ShapeDtypeStruct((t, w) if hds is None else (hds, t, w), d) for hds, w, d in outs]
                  + [jax.ShapeDtypeStruct((8 * (t // tm), V7X_LANES), F32)],
        compiler_params=_params(1),
        name="proj",
    )(h, *consts_a, cos, sin_up, sin_dn)


def _gla_constants():
    c, nl = GLA_C, GLA_LEVELS
    i = np.arange(c)[:, None]
    m = np.arange(c)[None, :]
    sum_f = [m <= i, m > i]
    sum_b = [m >= i, m < i]
    role_f, role_b, mask_f, mask_b = [], [], [], []
    for lvl in range(nl):
        s = c >> (lvl + 1)
        start = (i // (2 * s)) * 2 * s
        mid = start + s
        upper = (i % (2 * s)) >= s
        sum_f.append(np.where(upper, (m >= mid) & (m <= i), (m > i) & (m < mid)))
        sum_b.append(np.where(upper, (m >= mid) & (m < i), (m >= i) & (m < mid)))
        same = (i // (2 * s)) == (m // (2 * s))
        m_upper = (m % (2 * s)) >= s
        role_f.append(np.broadcast_to(upper, (c, V7X_LANES)))
        role_b.append(np.broadcast_to(~upper, (c, V7X_LANES)))
        mask_f.append(same & upper & ~m_upper)
        mask_b.append(same & ~upper & m_upper)
    sum_f = np.concatenate(sum_f, axis=0).astype(np.float32)
    sum_b = np.concatenate(sum_b, axis=0).astype(np.float32)
    sum_f = np.concatenate([sum_f, sum_f], axis=1)
    sum_b = np.concatenate([sum_b, sum_b], axis=1)
    roles = np.stack(role_f + role_b).astype(np.float32)
    masks = np.stack(mask_f + mask_b + [i == m]).astype(np.float32)
    return jnp.asarray(sum_f, BF16), jnp.asarray(sum_b, BF16), jnp.asarray(roles), jnp.asarray(masks)


def _gla_wide_prefix_matrix():
    wide = np.arange(GLA_WIDE_C)
    tri = (wide[None, :] <= wide[:, None]).astype(np.float32)
    return jnp.asarray(np.concatenate([tri, tri], axis=1), BF16)


def _decay_sums(g, sum_ref, n_rows):
    g_hi = g.astype(BF16)
    g_lo = (g - g_hi.astype(F32)).astype(BF16)
    return _dot(sum_ref[0:n_rows, :], jnp.concatenate([g_hi, g_lo], axis=0))


def _gla_scratch(n_chunks, rows):
    return [pltpu.VMEM((rows, GLA_DV), F32),
            pltpu.VMEM((rows, 2 * GLA_DK), BF16),
            pltpu.VMEM((n_chunks, GLA_DV, 2 * GLA_DK), F32),
            pltpu.VMEM((n_chunks, GLA_DV, 2 * GLA_DK), BF16),
            pltpu.VMEM((n_chunks * 8, 2 * GLA_DK), F32)]


def _gla_store_edges(scratch, i, rows, v, qe_f, qe_b, kd_f, kd_b, total_f, total_b):
    _, qe_ref, upd_ref, _, dec_ref = scratch
    qe_ref[rows, :] = jnp.concatenate([qe_f, qe_b], axis=1).astype(BF16)
    kd = jnp.concatenate([kd_f, kd_b], axis=1).astype(BF16)
    upd_ref[i] = _dot_tn(v, kd)
    total = jnp.concatenate([total_f, total_b], axis=1)
    dec_ref[pl.ds(pl.multiple_of(i * 8, 8), 8), :] = jnp.broadcast_to(total, (8, 2 * GLA_DK))


def _gla_scan_step(scratch, srows, i, j, sf, sb):
    _, _, upd_ref, snap_ref, dec_ref = scratch
    dk = GLA_DK
    snap_ref[i, srows, 0:dk] = sf.astype(BF16)
    snap_ref[j, srows, dk:2 * dk] = sb.astype(BF16)
    dec_f = dec_ref[pl.ds(pl.multiple_of(i * 8, 8), 8), 0:dk]
    dec_b = dec_ref[pl.ds(pl.multiple_of(j * 8, 8), 8), dk:2 * dk]
    sf = (sf.reshape(-1, 8, dk) * dec_f[None]).reshape(GLA_SLAB, dk) + upd_ref[i, srows, 0:dk]
    sb = (sb.reshape(-1, 8, dk) * dec_b[None]).reshape(GLA_SLAB, dk) + upd_ref[j, srows, dk:2 * dk]
    return sf, sb


def _gla_finish(scratch, rows, o_inter, gain_ref, sr_ref, o_ref):
    o = scratch[0][rows, :] + o_inter
    o_ref[rows, :] = (_rms(o, gain_ref[...]) * sr_ref[rows, :].astype(F32)).astype(BF16)


def _gla_any_decay(q_ref, k_ref, v_ref, sr_ref, gf_ref, gb_ref, gain_ref,
                   sumf_ref, sumb_ref, role_ref, mask_ref, o_ref, scratch):
    c, nl, dv = GLA_C, GLA_LEVELS, GLA_DV
    n_chunks = q_ref.shape[0] // c
    acc_ref, qe_ref, _, snap_ref, _ = scratch

    def chunk_any_decay(i, carry):
        rows = pl.ds(pl.multiple_of(i * c, c), c)
        q, k, v = q_ref[rows, :], k_ref[rows, :], v_ref[rows, :]
        qf, kf = q.astype(F32), k.astype(F32)
        ef = jnp.exp2(_decay_sums(gf_ref[rows, :], sumf_ref, (2 + nl) * c))
        eb = jnp.exp2(_decay_sums(gb_ref[rows, :], sumb_ref, (2 + nl) * c))
        a = _dot_nt(q, k) * mask_ref[2 * nl]
        for lvl in range(nl):
            blk = slice((2 + lvl) * c, (3 + lvl) * c)
            xf = (jnp.where(role_ref[lvl] > 0.5, qf, kf) * ef[blk]).astype(BF16)
            a = a + _dot_nt(xf, xf) * mask_ref[lvl]
            xb = (jnp.where(role_ref[nl + lvl] > 0.5, qf, kf) * eb[blk]).astype(BF16)
            a = a + _dot_nt(xb, xb) * mask_ref[nl + lvl]
        acc_ref[rows, :] = _dot(a.astype(BF16), v)
        _gla_store_edges(scratch, i, rows, v, qf * ef[0:c], qf * eb[0:c], kf * ef[c:2 * c], kf * eb[c:2 * c],
                         ef[c - 1:c], eb[0:1])
        return carry

    lax.fori_loop(0, n_chunks, chunk_any_decay, 0)

    for slab in range(dv // GLA_SLAB):
        srows = slice(slab * GLA_SLAB, (slab + 1) * GLA_SLAB)
        zero = jnp.zeros((GLA_SLAB, GLA_DK), F32)
        lax.fori_loop(0, n_chunks,
                      lambda step, carry: _gla_scan_step(scratch, srows, step, n_chunks - 1 - step, *carry),
                      (zero, zero), unroll=2)

    def finish(i, carry):
        rows = pl.ds(pl.multiple_of(i * c, c), c)
        _gla_finish(scratch, rows, _dot_nt(qe_ref[rows, :], snap_ref[i]), gain_ref, sr_ref, o_ref)
        return carry

    lax.fori_loop(0, n_chunks, finish, 0)


def _gla_head_spec(seq, width):
    return pl.BlockSpec((None, seq, width), lambda b, h: (h, b, 0))


def _score_bound(q_gain, k_gain):
    return (jnp.max(jnp.abs(q_gain)) * jnp.max(jnp.abs(k_gain))
            * (1.02 * ATT_HEAD_DIM ** 0.5 * LOG2_E))


class _GlaMild:
    def __init__(self, q_ref, k_ref, v_ref, sr_ref, gf_ref, gb_ref, gain_ref, tri_ref, o_ref, scratch):
        self.refs = (q_ref, k_ref, v_ref, sr_ref, gf_ref, gb_ref, gain_ref, tri_ref, o_ref)
        self.scratch = scratch
        w = GLA_WIDE_C
        self.n = q_ref.shape[0] // w
        self.rows = [slice(i * w, (i + 1) * w) for i in range(self.n)]

    def prefix_sums(self):
        _, _, _, _, gf_ref, gb_ref, _, tri_ref, _ = self.refs
        self.g_b = [gb_ref[r, :] for r in self.rows]
        self.prefix = [_decay_sums(jnp.concatenate([gf_ref[r, :], g], axis=1), tri_ref, GLA_WIDE_C)
                       for r, g in zip(self.rows, self.g_b)]

    def scores(self):
        q_ref, k_ref = self.refs[0:2]
        w, dk = GLA_WIDE_C, GLA_DK
        self.score, self.edge = [], []
        for r, g, p in zip(self.rows, self.g_b, self.prefix):
            qf, kf = q_ref[r, :].astype(F32), k_ref[r, :].astype(F32)
            bf, pb = p[:, 0:dk], p[:, dk:2 * dk]
            bb = (pb[w - 1:w] - pb) + g
            mid_f, mid_b = bf[w // 2 - 1:w // 2], bb[w // 2:w // 2 + 1]
            rf, rb = bf - mid_f, bb - mid_b
            q_f, k_f = qf * jnp.exp2(rf), kf * jnp.exp2(-rf)
            q_b, k_b = qf * jnp.exp2(rb), kf * jnp.exp2(-rb)
            self.score.append((_dot_nt(q_f.astype(BF16), k_f.astype(BF16)),
                               _dot_nt(q_b.astype(BF16), k_b.astype(BF16))))
            self.edge.append((q_f * jnp.exp2(mid_f), q_b * jnp.exp2(mid_b),
                              k_f * jnp.exp2(bf[w - 1:w] - mid_f), k_b * jnp.exp2(bb[0:1] - mid_b),
                              jnp.exp2(bf[w - 1:w]), jnp.exp2(bb[0:1])))

    def intra(self):
        v_ref, w = self.refs[2], GLA_WIDE_C
        lower = lax.broadcasted_iota(jnp.int32, (w, w), 0) >= lax.broadcasted_iota(jnp.int32, (w, w), 1)
        for r, (s_f, s_b) in zip(self.rows, self.score):
            self.scratch[0][r, :] = _dot(jnp.where(lower, s_f, s_b).astype(BF16), v_ref[r, :])

    def edges(self):
        v_ref = self.refs[2]
        for i, (r, parts) in enumerate(zip(self.rows, self.edge)):
            _gla_store_edges(self.scratch, i, r, v_ref[r, :], *parts)

    def scan(self):
        for slab in range(GLA_DV // GLA_SLAB):
            srows = slice(slab * GLA_SLAB, (slab + 1) * GLA_SLAB)
            sf = sb = jnp.zeros((GLA_SLAB, GLA_DK), F32)
            for step in range(self.n):
                sf, sb = _gla_scan_step(self.scratch, srows, step, self.n - 1 - step, sf, sb)

    def inter(self):
        qe_ref, snap_ref = self.scratch[1], self.scratch[3]
        self.o_inter = [_dot_nt(qe_ref[r, :], snap_ref[i]) for i, r in enumerate(self.rows)]

    def finish(self):
        sr_ref, gain_ref, o_ref = self.refs[3], self.refs[6], self.refs[8]
        for r, o_inter in zip(self.rows, self.o_inter):
            _gla_finish(self.scratch, r, o_inter, gain_ref, sr_ref, o_ref)


def _mixers_kernel(limits_ref, gq_ref, gk_ref, gv_ref, sr_ref, gf_ref, gb_ref, gain_ref,
                   sumf_ref, sumb_ref, role_ref, mask_ref, tri_ref, aq_ref, ak_ref, av_ref,
                   oa_ref, ob_ref, vext_ref, *scratch):
    hd, tq = ATT_HEAD_DIM, min(ATT_Q_TILE, aq_ref.shape[1])
    tiles = aq_ref.shape[1] // tq
    bound = limits_ref[0]
    vext_ref[:, :hd] = av_ref[...]
    vext_ref[:, hd:] = jnp.ones((av_ref.shape[0], hd), BF16)

    def weighted_values(p):
        o_ext = _dot(p, vext_ref[...])
        return (o_ext[:, :hd] / o_ext[:, hd:]).astype(BF16)

    mild = jnp.logical_and(bound <= ATT_SHIFT_LIMIT_LOG2, limits_ref[1] >= -GLA_MILD_LOG2)

    @pl.when(mild)
    def _():
        def attention_unit(tile, g):
            rows = slice(tile * tq, (tile + 1) * tq)
            p = jnp.exp2(_dot_nt(aq_ref[g, rows, :], ak_ref[...]) - bound).astype(BF16)
            ob_ref[rows, g * hd:(g + 1) * hd] = weighted_values(p)

        units = [(tile, g) for tile in range(tiles) for g in range(ATT_GROUP)]
        gla = _GlaMild(gq_ref, gk_ref, gv_ref, sr_ref, gf_ref, gb_ref, gain_ref, tri_ref, oa_ref, scratch)
        stages = [gla.prefix_sums, gla.scores, gla.intra, gla.edges, gla.scan, gla.inter, gla.finish]
        per_stage = -(-len(units) // (len(stages) + 1))
        for stage in stages:
            for unit in units[:per_stage]:
                attention_unit(*unit)
            units = units[per_stage:]
            stage()
        for unit in units:
            attention_unit(*unit)

    @pl.when(jnp.logical_not(mild))
    def _():
        _gla_any_decay(gq_ref, gk_ref, gv_ref, sr_ref, gf_ref, gb_ref, gain_ref,
                       sumf_ref, sumb_ref, role_ref, mask_ref, oa_ref, scratch)
        for tile in range(tiles):
            rows = slice(tile * tq, (tile + 1) * tq)
            q = aq_ref[:, rows, :].reshape(ATT_GROUP * tq, hd)
            s = _dot_nt(q, ak_ref[...])
            o = weighted_values(jnp.exp2(s - jnp.max(s, axis=-1, keepdims=True)).astype(BF16))
            for g in range(ATT_GROUP):
                ob_ref[rows, g * hd:(g + 1) * hd] = o[g * tq:(g + 1) * tq]


def _mixers(limits, gq, gk, gv, sr, lgf, lgb, gain, aq, ak, av, batch, seq):
    share = GLA_HEADS // ATT_KV_HEADS
    rows = seq // share
    head = functools.partial(_gla_head_spec, seq)
    qspec = pl.BlockSpec((ATT_GROUP, rows, ATT_HEAD_DIM), lambda b, h: (h // share, b * share + h % share, 0))
    kvspec = pl.BlockSpec((None, seq, ATT_HEAD_DIM), lambda b, h: (h // share, b, 0))
    obspec = pl.BlockSpec((None, rows, ATT_GROUP * ATT_HEAD_DIM), lambda b, h: (b, h % share, h // share))
    consts = (*_gla_constants(), _gla_wide_prefix_matrix())
    return pl.pallas_call(
        _mixers_kernel,
        grid=(batch, GLA_HEADS),
        in_specs=[pl.BlockSpec(memory_space=pltpu.SMEM),
                  head(GLA_DK), head(GLA_DK), head(GLA_DV), head(GLA_DV), head(GLA_DK), head(GLA_DK),
                  pl.BlockSpec((None, 1, GLA_DV), lambda b, h: (h, 0, 0))]
                 + [_const_spec(cst.shape) for cst in consts] + [qspec, kvspec, kvspec],
        out_specs=[pl.BlockSpec((None, seq, GLA_DV), lambda b, h: (b, 0, h)), obspec],
        out_shape=[jax.ShapeDtypeStruct((batch, seq, GLA_VAL_W), BF16),
                   jax.ShapeDtypeStruct((batch, seq, ATT_Q_W), BF16)],
        scratch_shapes=[pltpu.VMEM((seq, 2 * ATT_HEAD_DIM), BF16)] + _gla_scratch(seq // GLA_C, seq),
        compiler_params=_params(2),
        name="mixers",
    )(limits, gq, gk, gv, sr, lgf, lgb, gain, *consts, aq, ak, av)


def _merge_ffn_kernel(h_ref, oa_ref, ob_ref, sga_ref, sgb_ref, wa_ref, wb_ref, wo_ref, gmix_ref,
                      gpre_ref, win_ref, wout_ref, gpost_ref, o_ref):
    def merged_rows(rows):
        ya = _dot(oa_ref[rows, :], wa_ref[...])
        yb = _dot(ob_ref[rows, :], wb_ref[...])
        merged = sga_ref[rows, :].astype(F32) * ya + sgb_ref[rows, :].astype(F32) * yb
        m = _dot(merged.astype(BF16), wo_ref[...])
        return h_ref[rows, :] + _rms(m, gmix_ref[...])

    parts = _row_parts(h_ref, FFN_ROWS)
    h2 = merged_rows(parts[0])
    for rows, nxt in zip(parts, parts[1:] + [None]):
        h2_next = merged_rows(nxt) if nxt is not None else None
        o_ref[rows, :] = _ffn_rows(h2, gpre_ref, win_ref, wout_ref, gpost_ref)
        h2 = h2_next


def _merge_ffn(h, oa, ob, sga, sgb, w_a, w_b, w_o, g_mix, g_pre, w_in, w_out, g_post):
    t = h.shape[0]
    tm = min(MERGE_STEP_ROWS, t)
    row = pl.BlockSpec((tm, D_MODEL), lambda i: (i, 0))
    sq = _const_spec((D_MODEL, D_MODEL))
    vec = _const_spec((1, D_MODEL))
    return pl.pallas_call(
        _merge_ffn_kernel,
        grid=(t // tm,),
        in_specs=[row] * 5 + [sq, sq, sq, vec, vec, _const_spec((D_MODEL, 2 * D_FF)),
                              _const_spec((D_FF, D_MODEL)), vec],
        out_specs=row,
        out_shape=jax.ShapeDtypeStruct((t, D_MODEL), F32),
        compiler_params=_params(1),
        name="merge_ffn",
    )(h, oa, ob, sga, sgb, w_a, w_b, w_o, g_mix, g_pre, w_in, w_out, g_post)


def _split_w_in(w_in):
    gq, gk, gv, gr, za_f, za_b, aq, ak, av, ga, gb = jnp.split(w_in.astype(BF16), IN_OFFSETS, axis=-1)
    pad = jnp.zeros((D_MODEL, V7X_LANES - 2 * GLA_RANK), BF16)
    cat = lambda *parts: jnp.concatenate(parts, axis=-1)
    return cat(za_f, za_b, pad), cat(aq, ak), cat(ga, gb, gr), cat(gq, gk, gv, av)


def _pad_decay_up(up, first_row):
    out = jnp.zeros((V7X_LANES, GLA_KEY_W), BF16)
    return lax.dynamic_update_slice(out, up.astype(BF16), (first_row, 0))


def kernel(x, ffn1_pre_g, ffn1_w_in, ffn1_w_out, ffn1_post_g, mix_pre_g, w_in, gla_decay_up_f,
           gla_decay_bias_f, gla_decay_up_b, gla_decay_bias_b, gla_out_g, w_branch_a, att_q_norm_g,
           att_k_norm_g, w_branch_b, w_out, mix_post_g, ffn2_pre_g, ffn2_w_in, ffn2_w_out, ffn2_post_g):
    batch, seq, d = x.shape
    assert d == D_MODEL and seq % max(GLA_WIDE_C, GRID_W) == 0
    assert seq % ATT_Q_TILE == 0 or seq < ATT_Q_TILE
    assert seq % PROJ_ROW_TILE == 0 or seq < PROJ_ROW_TILE
    depth = w_in.shape[0]
    t = batch * seq
    h = x.reshape(t, d)
    vec = lambda g: g.reshape(1, -1).astype(F32)
    for l in range(depth):
        h = _ffn(h, vec(ffn1_pre_g[l]), ffn1_w_in[l].astype(BF16), ffn1_w_out[l].astype(BF16),
                 vec(ffn1_post_g[l]))
        gq, gk, gv, sr, lgf, lgb, aq, ak, av, sga, sgb, decay_min = _proj(
            h, seq, vec(mix_pre_g[l]), _split_w_in(w_in[l]),
            _pad_decay_up(gla_decay_up_f[l], 0), _pad_decay_up(gla_decay_up_b[l], GLA_RANK),
            vec(gla_decay_bias_f[l]), vec(gla_decay_bias_b[l]),
            vec(att_q_norm_g[l]), vec(att_k_norm_g[l]))
        limits = jnp.stack([_score_bound(att_q_norm_g[l], att_k_norm_g[l]), jnp.min(decay_min)]).astype(F32)
        oa, ob = _mixers(limits, gq, gk, gv, sr, lgf, lgb,
                         gla_out_g[l].reshape(GLA_HEADS, 1, GLA_DV).astype(F32), aq, ak, av, batch, seq)
        h = _merge_ffn(h, oa.reshape(t, GLA_VAL_W), ob.reshape(t, ATT_Q_W), sga, sgb,
                       w_branch_a[l].astype(BF16), w_branch_b[l].astype(BF16), w_out[l].astype(BF16),
                       vec(mix_post_g[l]), vec(ffn2_pre_g[l]), ffn2_w_in[l].astype(BF16),
                       ffn2_w_out[l].astype(BF16), vec(ffn2_post_g[l]))
    return h.reshape(batch, seq, d)
```

```python
import functools

import numpy as np
import jax
import jax.numpy as jnp
from jax import lax
from jax.experimental import pallas as pl
from jax.experimental.pallas import tpu as pltpu

F32 = jnp.float32
BF16 = jnp.bfloat16

D_MODEL = 1024
GRID_W = 64
D_FF = 2816
EPS = 1e-6
GLA_HEADS = 4
GLA_DK = 128
GLA_DV = 256
GLA_RANK = 16
GLA_TAU = 16.0
ATT_Q_HEADS = 8
ATT_KV_HEADS = 2
ATT_HEAD_DIM = 128
ATT_GROUP = ATT_Q_HEADS // ATT_KV_HEADS
ROPE_THETA = 10000.0
GLA_KEY_W = GLA_HEADS * GLA_DK
GLA_VAL_W = GLA_HEADS * GLA_DV
ATT_Q_W = ATT_Q_HEADS * ATT_HEAD_DIM
ATT_KV_W = ATT_KV_HEADS * ATT_HEAD_DIM
IN_SPLITS = (GLA_KEY_W, GLA_KEY_W, GLA_VAL_W, GLA_VAL_W, GLA_RANK, GLA_RANK,
             ATT_Q_W, ATT_KV_W, ATT_KV_W, D_MODEL, D_MODEL)
IN_OFFSETS = tuple(int(s) for s in np.cumsum(IN_SPLITS)[:-1])
LOG2_E = 1.4426950408889634

V7X_LANES = 128
V7X_VMEM_BYTES = 64 * 1024 * 1024
VMEM_LIMIT = V7X_VMEM_BYTES - 8 * 1024 * 1024

FFN_STEP_ROWS = 1024
MERGE_STEP_ROWS = 512
FFN_ROWS = 256
FFN_WEIGHT_STEPS = 8
MERGE_WEIGHT_STEPS = 16
PROJ_ROW_TILE = 512
PROJ_ROWS = 256
ATT_Q_TILE = 256
ATT_SHIFT_LIMIT_LOG2 = 60.0
GLA_C = 64
GLA_LEVELS = GLA_C.bit_length() - 1
GLA_SLAB = 64
GLA_WIDE_C = 256
GLA_MILD_LOG2 = 48.0
FF_CHUNKS = ((0, 1024), (1024, 1024), (2048, 768))


def _const_spec(shape):
    zeros = (0,) * len(shape)
    return pl.BlockSpec(shape, lambda *_: zeros, pipeline_mode=pl.Buffered(1))


def _params(n_axes):
    return pltpu.CompilerParams(dimension_semantics=("arbitrary",) * n_axes,
                                vmem_limit_bytes=VMEM_LIMIT)


def _rms(x, g):
    ms = jnp.mean(x * x, axis=-1, keepdims=True)
    return x * lax.rsqrt(ms + EPS) * g


def _dot(a, b):
    return jnp.dot(a, b, preferred_element_type=F32)


def _dot_nt(a, b):
    return lax.dot_general(a, b, (((1,), (1,)), ((), ())), preferred_element_type=F32)


def _dot_tn(a, b):
    return lax.dot_general(a, b, (((0,), (0,)), ((), ())), preferred_element_type=F32)


def _row_parts(ref, rows):
    rows = min(rows, ref.shape[0])
    return [slice(r, r + rows) for r in range(0, ref.shape[0], rows)]


def _ffn_rows(x, gpre_ref, win_ref, wout_ref, gpost_ref):
    xn = _rms(x, gpre_ref[...]).astype(BF16)
    acc = None
    for c0, cw in FF_CHUNKS:
        gate = _dot(xn, win_ref[:, c0:c0 + cw])
        up = _dot(xn, win_ref[:, D_FF + c0:D_FF + c0 + cw])
        act = (gate * jax.nn.sigmoid(gate) * up).astype(BF16)
        part = _dot(act, wout_ref[c0:c0 + cw, :])
        acc = part if acc is None else acc + part
    return x + 0.5 * _rms(acc, gpost_ref[...])


def _weight_chunk_spec(shape, steps):
    return pl.BlockSpec((shape[0] // steps, shape[1]), lambda i: (jnp.minimum(i, steps - 1), 0))


def _row_tile_spec(rows, steps):
    return pl.BlockSpec((rows, D_MODEL), lambda i: (jnp.maximum(i - steps, 0), 0))


def _cast_weight_chunk(step, src_ref, dst_ref):
    rows = src_ref.shape[0]
    dst_ref[pl.ds(pl.multiple_of(step * rows, rows), rows), :] = src_ref[...].astype(BF16)


def _ffn_kernel(x_ref, gpre_ref, win32_ref, wout32_ref, gpost_ref, o_ref, win_ref, wout_ref):
    step = pl.program_id(0)

    @pl.when(step < FFN_WEIGHT_STEPS)
    def _():
        _cast_weight_chunk(step, win32_ref, win_ref)
        _cast_weight_chunk(step, wout32_ref, wout_ref)

    @pl.when(step >= FFN_WEIGHT_STEPS)
    def _():
        for rows in _row_parts(x_ref, FFN_ROWS):
            o_ref[rows, :] = _ffn_rows(x_ref[rows, :], gpre_ref, win_ref, wout_ref, gpost_ref)


def _ffn(x, g_pre, w_in, w_out, g_post):
    t = x.shape[0]
    tm = min(FFN_STEP_ROWS, t)
    steps = FFN_WEIGHT_STEPS
    row = _row_tile_spec(tm, steps)
    return pl.pallas_call(
        _ffn_kernel,
        grid=(steps + t // tm,),
        in_specs=[row, _const_spec((1, D_MODEL)), _weight_chunk_spec(w_in.shape, steps),
                  _weight_chunk_spec(w_out.shape, steps), _const_spec((1, D_MODEL))],
        out_specs=row,
        out_shape=jax.ShapeDtypeStruct((t, D_MODEL), F32),
        scratch_shapes=[pltpu.VMEM(w_in.shape, BF16), pltpu.VMEM(w_out.shape, BF16)],
        compiler_params=_params(1),
        name="ffn",
    )(x, g_pre, w_in, w_out, g_post)


def _log_sigmoid(z):
    return jnp.minimum(z, 0.0) - jnp.log(1.0 + jnp.exp(-jnp.abs(z)))


def _rope(x, cos, sin_up, sin_dn):
    return (x * cos + pltpu.roll(x, V7X_LANES - 32, axis=1) * sin_up
            + pltpu.roll(x, 32, axis=1) * sin_dn)


def _store_heads(o_ref, rows, value):
    heads, _, width = o_ref.shape
    for hd in range(heads):
        o_ref[hd, rows, :] = value[:, hd * width:(hd + 1) * width]


def _proj_kernel(h_ref, gmix_ref, wz_ref, watt_ref, wgate_ref, wplain_ref, upf_ref, upb_ref,
                 bf_ref, bb_ref, qg_ref, kg_ref, cos_ref, sup_ref, sdn_ref,
                 gq_o, gk_o, gv_o, sr_o, lf_o, lb_o, aq_o, ak_o, av_o, sga_o, sgb_o, dmin_o):
    qg = qg_ref[...] * (ATT_HEAD_DIM ** -0.5 * LOG2_E)
    worst = None
    for rows in _row_parts(h_ref, PROJ_ROWS):
        u = _rms(h_ref[rows, :], gmix_ref[...]).astype(BF16)
        za = _dot(u, wz_ref[...]).astype(BF16)

        cos, sup, sdn = cos_ref[rows, :], sup_ref[rows, :], sdn_ref[rows, :]
        att = _dot(u, watt_ref[...])
        for hd in range(ATT_Q_HEADS):
            sl = slice(hd * ATT_HEAD_DIM, (hd + 1) * ATT_HEAD_DIM)
            aq_o[hd, rows, :] = _rope(_rms(att[:, sl], qg), cos, sup, sdn).astype(BF16)
        for hd in range(ATT_KV_HEADS):
            sl = slice(hd * ATT_HEAD_DIM, (hd + 1) * ATT_HEAD_DIM)
            ak_o[hd, rows, :] = _rope(_rms(att[:, ATT_Q_W + sl.start:ATT_Q_W + sl.stop], kg_ref[...]),
                                      cos, sup, sdn).astype(BF16)

        gate = _dot(u, wgate_ref[...])
        sga_o[rows, :] = jax.nn.sigmoid(gate[:, 0:D_MODEL]).astype(BF16)
        sgb_o[rows, :] = jax.nn.sigmoid(gate[:, D_MODEL:2 * D_MODEL]).astype(BF16)
        gr = gate[:, 2 * D_MODEL:]
        _store_heads(sr_o, rows, (gr * jax.nn.sigmoid(gr)).astype(BF16))
        lf = _log_sigmoid(_dot(za, upf_ref[...]) + bf_ref[...]) * (LOG2_E / GLA_TAU)
        lb = _log_sigmoid(_dot(za, upb_ref[...]) + bb_ref[...]) * (LOG2_E / GLA_TAU)
        _store_heads(lf_o, rows, lf)
        _store_heads(lb_o, rows, lb)
        totals = jnp.minimum(lf.reshape(-1, GLA_WIDE_C, GLA_KEY_W).sum(axis=1),
                             lb.reshape(-1, GLA_WIDE_C, GLA_KEY_W).sum(axis=1))
        part = jnp.min(totals, axis=0, keepdims=True)
        worst = part if worst is None else jnp.minimum(worst, part)

        plain = _dot(u, wplain_ref[...])
        _store_heads(gq_o, rows, (plain[:, 0:GLA_KEY_W] * (GLA_DK ** -0.5)).astype(BF16))
        _store_heads(gk_o, rows, plain[:, GLA_KEY_W:2 * GLA_KEY_W].astype(BF16))
        _store_heads(gv_o, rows, plain[:, 2 * GLA_KEY_W:2 * GLA_KEY_W + GLA_VAL_W].astype(BF16))
        _store_heads(av_o, rows, plain[:, 2 * GLA_KEY_W + GLA_VAL_W:].astype(BF16))
    worst = functools.reduce(jnp.minimum, [worst[:, l:l + V7X_LANES] for l in range(0, GLA_KEY_W, V7X_LANES)])
    dmin_o[...] = jnp.broadcast_to(worst, dmin_o.shape)


def _rope_tables(seq):
    pos = np.arange(seq)
    half = ATT_HEAD_DIM // 2
    inv_freq = ROPE_THETA ** (-np.arange(0, half, 2, dtype=np.float64) / half)
    ang_r = (pos // GRID_W)[:, None] * inv_freq
    ang_c = (pos % GRID_W)[:, None] * inv_freq
    zero = np.zeros_like(ang_r)
    cos = np.concatenate([np.cos(ang_r)] * 2 + [np.cos(ang_c)] * 2, axis=-1)
    sin_up = np.concatenate([-np.sin(ang_r), zero, -np.sin(ang_c), zero], axis=-1)
    sin_dn = np.concatenate([zero, np.sin(ang_r), zero, np.sin(ang_c)], axis=-1)
    return tuple(jnp.asarray(t, F32) for t in (cos, sin_up, sin_dn))


def _proj(h, seq, g_mix, w_parts, up_f, up_b, bias_f, bias_b, q_gain, k_gain):
    t = h.shape[0]
    tm = min(PROJ_ROW_TILE, seq)
    tiles_per_seq = seq // tm
    cos, sin_up, sin_dn = _rope_tables(seq)

    def row(width):
        return pl.BlockSpec((tm, width), lambda i: (i, 0))

    table = pl.BlockSpec((tm, ATT_HEAD_DIM), lambda i: (i % tiles_per_seq, 0))
    consts_a = (g_mix, *w_parts, up_f, up_b, bias_f, bias_b, q_gain, k_gain)
    outs = ((GLA_HEADS, GLA_DK, BF16), (GLA_HEADS, GLA_DK, BF16), (GLA_HEADS, GLA_DV, BF16),
            (GLA_HEADS, GLA_DV, BF16), (GLA_HEADS, GLA_DK, F32), (GLA_HEADS, GLA_DK, F32),
            (ATT_Q_HEADS, ATT_HEAD_DIM, BF16), (ATT_KV_HEADS, ATT_HEAD_DIM, BF16),
            (ATT_KV_HEADS, ATT_HEAD_DIM, BF16), (None, D_MODEL, BF16), (None, D_MODEL, BF16))
    return pl.pallas_call(
        _proj_kernel,
        grid=(t // tm,),
        in_specs=[row(D_MODEL)] + [_const_spec(c.shape) for c in consts_a] + [table] * 3,
        out_specs=[row(w) if hds is None else pl.BlockSpec((hds, tm, w), lambda i: (0, i, 0))
                   for hds, w, _ in outs] + [pl.BlockSpec((8, V7X_LANES), lambda i: (i, 0))],
        out_shape=[jax.ShapeDtypeStruct((t, w) if hds is None else (hds, t, w), d) for hds, w, d in outs]
                  + [jax.ShapeDtypeStruct((8 * (t // tm), V7X_LANES), F32)],
        compiler_params=_params(1),
        name="proj",
    )(h, *consts_a, cos, sin_up, sin_dn)


def _gla_constants():
    c, nl = GLA_C, GLA_LEVELS
    i = np.arange(c)[:, None]
    m = np.arange(c)[None, :]
    sum_f = [m <= i, m > i]
    sum_b = [m >= i, m < i]
    role_f, role_b, mask_f, mask_b = [], [], [], []
    for lvl in range(nl):
        s = c >> (lvl + 1)
        start = (i // (2 * s)) * 2 * s
        mid = start + s
        upper = (i % (2 * s)) >= s
        sum_f.append(np.where(upper, (m >= mid) & (m <= i), (m > i) & (m < mid)))
        sum_b.append(np.where(upper, (m >= mid) & (m < i), (m >= i) & (m < mid)))
        same = (i // (2 * s)) == (m // (2 * s))
        m_upper = (m % (2 * s)) >= s
        role_f.append(np.broadcast_to(upper, (c, V7X_LANES)))
        role_b.append(np.broadcast_to(~upper, (c, V7X_LANES)))
        mask_f.append(same & upper & ~m_upper)
        mask_b.append(same & ~upper & m_upper)
    sum_f = np.concatenate(sum_f, axis=0).astype(np.float32)
    sum_b = np.concatenate(sum_b, axis=0).astype(np.float32)
    sum_f = np.concatenate([sum_f, sum_f], axis=1)
    sum_b = np.concatenate([sum_b, sum_b], axis=1)
    roles = np.stack(role_f + role_b).astype(np.float32)
    masks = np.stack(mask_f + mask_b + [i == m]).astype(np.float32)
    return jnp.asarray(sum_f, BF16), jnp.asarray(sum_b, BF16), jnp.asarray(roles), jnp.asarray(masks)


def _gla_wide_prefix_matrix():
    wide = np.arange(GLA_WIDE_C)
    tri = (wide[None, :] <= wide[:, None]).astype(np.float32)
    return jnp.asarray(np.concatenate([tri, tri], axis=1), BF16)


def _decay_sums(g, sum_ref, n_rows):
    g_hi = g.astype(BF16)
    g_lo = (g - g_hi.astype(F32)).astype(BF16)
    return _dot(sum_ref[0:n_rows, :], jnp.concatenate([g_hi, g_lo], axis=0))


def _gla_scratch(n_chunks, rows):
    return [pltpu.VMEM((rows, GLA_DV), F32),
            pltpu.VMEM((rows, 2 * GLA_DK), BF16),
            pltpu.VMEM((n_chunks, GLA_DV, 2 * GLA_DK), F32),
            pltpu.VMEM((n_chunks, GLA_DV, 2 * GLA_DK), BF16),
            pltpu.VMEM((n_chunks * 8, 2 * GLA_DK), F32)]


def _gla_store_edges(scratch, i, rows, v, qe_f, qe_b, kd_f, kd_b, total_f, total_b):
    _, qe_ref, upd_ref, _, dec_ref = scratch
    qe_ref[rows, :] = jnp.concatenate([qe_f, qe_b], axis=1).astype(BF16)
    kd = jnp.concatenate([kd_f, kd_b], axis=1).astype(BF16)
    upd_ref[i] = _dot_tn(v, kd)
    total = jnp.concatenate([total_f, total_b], axis=1)
    dec_ref[pl.ds(pl.multiple_of(i * 8, 8), 8), :] = jnp.broadcast_to(total, (8, 2 * GLA_DK))


def _gla_scan_step(scratch, srows, i, j, sf, sb):
    _, _, upd_ref, snap_ref, dec_ref = scratch
    dk = GLA_DK
    snap_ref[i, srows, 0:dk] = sf.astype(BF16)
    snap_ref[j, srows, dk:2 * dk] = sb.astype(BF16)
    dec_f = dec_ref[pl.ds(pl.multiple_of(i * 8, 8), 8), 0:dk]
    dec_b = dec_ref[pl.ds(pl.multiple_of(j * 8, 8), 8), dk:2 * dk]
    sf = (sf.reshape(-1, 8, dk) * dec_f[None]).reshape(GLA_SLAB, dk) + upd_ref[i, srows, 0:dk]
    sb = (sb.reshape(-1, 8, dk) * dec_b[None]).reshape(GLA_SLAB, dk) + upd_ref[j, srows, dk:2 * dk]
    return sf, sb


def _gla_finish(scratch, rows, o_inter, gain_ref, sr_ref, o_ref):
    o = scratch[0][rows, :] + o_inter
    o_ref[rows, :] = (_rms(o, gain_ref[...]) * sr_ref[rows, :].astype(F32)).astype(BF16)


def _gla_any_decay(q_ref, k_ref, v_ref, sr_ref, gf_ref, gb_ref, gain_ref,
                   sumf_ref, sumb_ref, role_ref, mask_ref, o_ref, scratch):
    c, nl, dv = GLA_C, GLA_LEVELS, GLA_DV
    n_chunks = q_ref.shape[0] // c
    acc_ref, qe_ref, _, snap_ref, _ = scratch

    def chunk_any_decay(i, carry):
        rows = pl.ds(pl.multiple_of(i * c, c), c)
        q, k, v = q_ref[rows, :], k_ref[rows, :], v_ref[rows, :]
        qf, kf = q.astype(F32), k.astype(F32)
        ef = jnp.exp2(_decay_sums(gf_ref[rows, :], sumf_ref, (2 + nl) * c))
        eb = jnp.exp2(_decay_sums(gb_ref[rows, :], sumb_ref, (2 + nl) * c))
        a = _dot_nt(q, k) * mask_ref[2 * nl]
        for lvl in range(nl):
            blk = slice((2 + lvl) * c, (3 + lvl) * c)
            xf = (jnp.where(role_ref[lvl] > 0.5, qf, kf) * ef[blk]).astype(BF16)
            a = a + _dot_nt(xf, xf) * mask_ref[lvl]
            xb = (jnp.where(role_ref[nl + lvl] > 0.5, qf, kf) * eb[blk]).astype(BF16)
            a = a + _dot_nt(xb, xb) * mask_ref[nl + lvl]
        acc_ref[rows, :] = _dot(a.astype(BF16), v)
        _gla_store_edges(scratch, i, rows, v, qf * ef[0:c], qf * eb[0:c], kf * ef[c:2 * c], kf * eb[c:2 * c],
                         ef[c - 1:c], eb[0:1])
        return carry

    lax.fori_loop(0, n_chunks, chunk_any_decay, 0)

    for slab in range(dv // GLA_SLAB):
        srows = slice(slab * GLA_SLAB, (slab + 1) * GLA_SLAB)
        zero = jnp.zeros((GLA_SLAB, GLA_DK), F32)
        lax.fori_loop(0, n_chunks,
                      lambda step, carry: _gla_scan_step(scratch, srows, step, n_chunks - 1 - step, *carry),
                      (zero, zero), unroll=2)

    def finish(i, carry):
        rows = pl.ds(pl.multiple_of(i * c, c), c)
        _gla_finish(scratch, rows, _dot_nt(qe_ref[rows, :], snap_ref[i]), gain_ref, sr_ref, o_ref)
        return carry

    lax.fori_loop(0, n_chunks, finish, 0)


def _gla_head_spec(seq, width):
    return pl.BlockSpec((None, seq, width), lambda b, h: (h, b, 0))


def _score_bound(q_gain, k_gain):
    return (jnp.max(jnp.abs(q_gain)) * jnp.max(jnp.abs(k_gain))
            * (1.02 * ATT_HEAD_DIM ** 0.5 * LOG2_E))


class _GlaMild:
    def __init__(self, q_ref, k_ref, v_ref, sr_ref, gf_ref, gb_ref, gain_ref, tri_ref, o_ref, scratch):
        self.refs = (q_ref, k_ref, v_ref, sr_ref, gf_ref, gb_ref, gain_ref, tri_ref, o_ref)
        self.scratch = scratch
        w = GLA_WIDE_C
        self.n = q_ref.shape[0] // w
        self.rows = [slice(i * w, (i + 1) * w) for i in range(self.n)]

    def prefix_sums(self):
        _, _, _, _, gf_ref, gb_ref, _, tri_ref, _ = self.refs
        self.g_b = [gb_ref[r, :] for r in self.rows]
        self.prefix = [_decay_sums(jnp.concatenate([gf_ref[r, :], g], axis=1), tri_ref, GLA_WIDE_C)
                       for r, g in zip(self.rows, self.g_b)]

    def scores(self):
        q_ref, k_ref = self.refs[0:2]
        w, dk = GLA_WIDE_C, GLA_DK
        self.score, self.edge = [], []
        for r, g, p in zip(self.rows, self.g_b, self.prefix):
            qf, kf = q_ref[r, :].astype(F32), k_ref[r, :].astype(F32)
            bf, pb = p[:, 0:dk], p[:, dk:2 * dk]
            bb = (pb[w - 1:w] - pb) + g
            mid_f, mid_b = bf[w // 2 - 1:w // 2], bb[w // 2:w // 2 + 1]
            rf, rb = bf - mid_f, bb - mid_b
            q_f, k_f = qf * jnp.exp2(rf), kf * jnp.exp2(-rf)
            q_b, k_b = qf * jnp.exp2(rb), kf * jnp.exp2(-rb)
            self.score.append((_dot_nt(q_f.astype(BF16), k_f.astype(BF16)),
                               _dot_nt(q_b.astype(BF16), k_b.astype(BF16))))
            self.edge.append((q_f * jnp.exp2(mid_f), q_b * jnp.exp2(mid_b),
                              k_f * jnp.exp2(bf[w - 1:w] - mid_f), k_b * jnp.exp2(bb[0:1] - mid_b),
                              jnp.exp2(bf[w - 1:w]), jnp.exp2(bb[0:1])))

    def intra(self):
        v_ref, w = self.refs[2], GLA_WIDE_C
        lower = lax.broadcasted_iota(jnp.int32, (w, w), 0) >= lax.broadcasted_iota(jnp.int32, (w, w), 1)
        for r, (s_f, s_b) in zip(self.rows, self.score):
            self.scratch[0][r, :] = _dot(jnp.where(lower, s_f, s_b).astype(BF16), v_ref[r, :])

    def edges(self):
        v_ref = self.refs[2]
        for i, (r, parts) in enumerate(zip(self.rows, self.edge)):
            _gla_store_edges(self.scratch, i, r, v_ref[r, :], *parts)

    def scan(self):
        for slab in range(GLA_DV // GLA_SLAB):
            srows = slice(slab * GLA_SLAB, (slab + 1) * GLA_SLAB)
            sf = sb = jnp.zeros((GLA_SLAB, GLA_DK), F32)
            for step in range(self.n):
                sf, sb = _gla_scan_step(self.scratch, srows, step, self.n - 1 - step, sf, sb)

    def inter(self):
        qe_ref, snap_ref = self.scratch[1], self.scratch[3]
        self.o_inter = [_dot_nt(qe_ref[r, :], snap_ref[i]) for i, r in enumerate(self.rows)]

    def finish(self):
        sr_ref, gain_ref, o_ref = self.refs[3], self.refs[6], self.refs[8]
        for r, o_inter in zip(self.rows, self.o_inter):
            _gla_finish(self.scratch, r, o_inter, gain_ref, sr_ref, o_ref)


def _mixers_kernel(limits_ref, gq_ref, gk_ref, gv_ref, sr_ref, gf_ref, gb_ref, gain_ref,
                   sumf_ref, sumb_ref, role_ref, mask_ref, tri_ref, aq_ref, ak_ref, av_ref,
                   oa_ref, ob_ref, vext_ref, *scratch):
    hd, tq = ATT_HEAD_DIM, min(ATT_Q_TILE, aq_ref.shape[1])
    tiles = aq_ref.shape[1] // tq
    bound = limits_ref[0]
    vext_ref[:, :hd] = av_ref[...]
    vext_ref[:, hd:] = jnp.ones((av_ref.shape[0], hd), BF16)

    def weighted_values(p):
        o_ext = _dot(p, vext_ref[...])
        return (o_ext[:, :hd] / o_ext[:, hd:]).astype(BF16)

    mild = jnp.logical_and(bound <= ATT_SHIFT_LIMIT_LOG2, limits_ref[1] >= -GLA_MILD_LOG2)

    @pl.when(mild)
    def _():
        def attention_unit(tile, g):
            rows = slice(tile * tq, (tile + 1) * tq)
            p = jnp.exp2(_dot_nt(aq_ref[g, rows, :], ak_ref[...]) - bound).astype(BF16)
            ob_ref[rows, g * hd:(g + 1) * hd] = weighted_values(p)

        units = [(tile, g) for tile in range(tiles) for g in range(ATT_GROUP)]
        gla = _GlaMild(gq_ref, gk_ref, gv_ref, sr_ref, gf_ref, gb_ref, gain_ref, tri_ref, oa_ref, scratch)
        stages = [gla.prefix_sums, gla.scores, gla.intra, gla.edges, gla.scan, gla.inter, gla.finish]
        per_stage = -(-len(units) // (len(stages) + 1))
        for stage in stages:
            for unit in units[:per_stage]:
                attention_unit(*unit)
            units = units[per_stage:]
            stage()
        for unit in units:
            attention_unit(*unit)

    @pl.when(jnp.logical_not(mild))
    def _():
        _gla_any_decay(gq_ref, gk_ref, gv_ref, sr_ref, gf_ref, gb_ref, gain_ref,
                       sumf_ref, sumb_ref, role_ref, mask_ref, oa_ref, scratch)
        for tile in range(tiles):
            rows = slice(tile * tq, (tile + 1) * tq)
            q = aq_ref[:, rows, :].reshape(ATT_GROUP * tq, hd)
            s = _dot_nt(q, ak_ref[...])
            o = weighted_values(jnp.exp2(s - jnp.max(s, axis=-1, keepdims=True)).astype(BF16))
            for g in range(ATT_GROUP):
                ob_ref[rows, g * hd:(g + 1) * hd] = o[g * tq:(g + 1) * tq]


def _mixers(limits, gq, gk, gv, sr, lgf, lgb, gain, aq, ak, av, batch, seq):
    share = GLA_HEADS // ATT_KV_HEADS
    rows = seq // share
    head = functools.partial(_gla_head_spec, seq)
    qspec = pl.BlockSpec((ATT_GROUP, rows, ATT_HEAD_DIM), lambda b, h: (h // share, b * share + h % share, 0))
    kvspec = pl.BlockSpec((None, seq, ATT_HEAD_DIM), lambda b, h: (h // share, b, 0))
    obspec = pl.BlockSpec((None, rows, ATT_GROUP * ATT_HEAD_DIM), lambda b, h: (b, h % share, h // share))
    consts = (*_gla_constants(), _gla_wide_prefix_matrix())
    return pl.pallas_call(
        _mixers_kernel,
        grid=(batch, GLA_HEADS),
        in_specs=[pl.BlockSpec(memory_space=pltpu.SMEM),
                  head(GLA_DK), head(GLA_DK), head(GLA_DV), head(GLA_DV), head(GLA_DK), head(GLA_DK),
                  pl.BlockSpec((None, 1, GLA_DV), lambda b, h: (h, 0, 0))]
                 + [_const_spec(cst.shape) for cst in consts] + [qspec, kvspec, kvspec],
        out_specs=[pl.BlockSpec((None, seq, GLA_DV), lambda b, h: (b, 0, h)), obspec],
        out_shape=[jax.ShapeDtypeStruct((batch, seq, GLA_VAL_W), BF16),
                   jax.ShapeDtypeStruct((batch, seq, ATT_Q_W), BF16)],
        scratch_shapes=[pltpu.VMEM((seq, 2 * ATT_HEAD_DIM), BF16)] + _gla_scratch(seq // GLA_C, seq),
        compiler_params=_params(2),
        name="mixers",
    )(limits, gq, gk, gv, sr, lgf, lgb, gain, *consts, aq, ak, av)


def _merge_ffn_kernel(h_ref, oa_ref, ob_ref, sga_ref, sgb_ref, wa32_ref, wb32_ref, wo32_ref, gmix_ref,
                      gpre_ref, win32_ref, wout32_ref, gpost_ref, o_ref,
                      wa_ref, wb_ref, wo_ref, win_ref, wout_ref):
    step = pl.program_id(0)

    @pl.when(step < MERGE_WEIGHT_STEPS)
    def _():
        for src, dst in ((wa32_ref, wa_ref), (wb32_ref, wb_ref), (wo32_ref, wo_ref),
                         (win32_ref, win_ref), (wout32_ref, wout_ref)):
            _cast_weight_chunk(step, src, dst)

    def merged_rows(rows):
        ya = _dot(oa_ref[rows, :], wa_ref[...])
        yb = _dot(ob_ref[rows, :], wb_ref[...])
        merged = sga_ref[rows, :].astype(F32) * ya + sgb_ref[rows, :].astype(F32) * yb
        m = _dot(merged.astype(BF16), wo_ref[...])
        return h_ref[rows, :] + _rms(m, gmix_ref[...])

    @pl.when(step >= MERGE_WEIGHT_STEPS)
    def _():
        parts = _row_parts(h_ref, FFN_ROWS)
        h2 = merged_rows(parts[0])
        for rows, nxt in zip(parts, parts[1:] + [None]):
            h2_next = merged_rows(nxt) if nxt is not None else None
            o_ref[rows, :] = _ffn_rows(h2, gpre_ref, win_ref, wout_ref, gpost_ref)
            h2 = h2_next


def _merge_ffn(h, oa, ob, sga, sgb, w_a, w_b, w_o, g_mix, g_pre, w_in, w_out, g_post):
    t = h.shape[0]
    tm = min(MERGE_STEP_ROWS, t)
    steps = MERGE_WEIGHT_STEPS
    row = _row_tile_spec(tm, steps)
    vec = _const_spec((1, D_MODEL))
    weights = (w_a, w_b, w_o, w_in, w_out)
    chunk = [_weight_chunk_spec(w.shape, steps) for w in weights]
    return pl.pallas_call(
        _merge_ffn_kernel,
        grid=(steps + t // tm,),
        in_specs=[row] * 5 + chunk[0:3] + [vec, vec] + chunk[3:5] + [vec],
        out_specs=row,
        out_shape=jax.ShapeDtypeStruct((t, D_MODEL), F32),
        scratch_shapes=[pltpu.VMEM(w.shape, BF16) for w in weights],
        compiler_params=_params(1),
        name="merge_ffn",
    )(h, oa, ob, sga, sgb, w_a, w_b, w_o, g_mix, g_pre, w_in, w_out, g_post)


def _split_w_in(w_in):
    gq, gk, gv, gr, za_f, za_b, aq, ak, av, ga, gb = jnp.split(w_in.astype(BF16), IN_OFFSETS, axis=-1)
    pad = jnp.zeros((D_MODEL, V7X_LANES - 2 * GLA_RANK), BF16)
    cat = lambda *parts: jnp.concatenate(parts, axis=-1)
    return cat(za_f, za_b, pad), cat(aq, ak), cat(ga, gb, gr), cat(gq, gk, gv, av)


def _pad_decay_up(up, first_row):
    out = jnp.zeros((V7X_LANES, GLA_KEY_W), BF16)
    return lax.dynamic_update_slice(out, up.astype(BF16), (first_row, 0))


def kernel(x, ffn1_pre_g, ffn1_w_in, ffn1_w_out, ffn1_post_g, mix_pre_g, w_in, gla_decay_up_f,
           gla_decay_bias_f, gla_decay_up_b, gla_decay_bias_b, gla_out_g, w_branch_a, att_q_norm_g,
           att_k_norm_g, w_branch_b, w_out, mix_post_g, ffn2_pre_g, ffn2_w_in, ffn2_w_out, ffn2_post_g):
    batch, seq, d = x.shape
    assert d == D_MODEL and seq % max(GLA_WIDE_C, GRID_W) == 0
    assert seq % ATT_Q_TILE == 0 or seq < ATT_Q_TILE
    assert seq % PROJ_ROW_TILE == 0 or seq < PROJ_ROW_TILE
    depth = w_in.shape[0]
    t = batch * seq
    h = x.reshape(t, d)
    vec = lambda g: g.reshape(1, -1).astype(F32)
    for l in range(depth):
        h = _ffn(h, vec(ffn1_pre_g[l]), ffn1_w_in[l].astype(F32), ffn1_w_out[l].astype(F32),
                 vec(ffn1_post_g[l]))
        gq, gk, gv, sr, lgf, lgb, aq, ak, av, sga, sgb, decay_min = _proj(
            h, seq, vec(mix_pre_g[l]), _split_w_in(w_in[l]),
            _pad_decay_up(gla_decay_up_f[l], 0), _pad_decay_up(gla_decay_up_b[l], GLA_RANK),
            vec(gla_decay_bias_f[l]), vec(gla_decay_bias_b[l]),
            vec(att_q_norm_g[l]), vec(att_k_norm_g[l]))
        limits = jnp.stack([_score_bound(att_q_norm_g[l], att_k_norm_g[l]), jnp.min(decay_min)]).astype(F32)
        oa, ob = _mixers(limits, gq, gk, gv, sr, lgf, lgb,
                         gla_out_g[l].reshape(GLA_HEADS, 1, GLA_DV).astype(F32), aq, ak, av, batch, seq)
        h = _merge_ffn(h, oa.reshape(t, GLA_VAL_W), ob.reshape(t, ATT_Q_W), sga, sgb,
                       w_branch_a[l].astype(F32), w_branch_b[l].astype(F32), w_out[l].astype(F32),
                       vec(mix_post_g[l]), vec(ffn2_pre_g[l]), ffn2_w_in[l].astype(F32),
                       ffn2_w_out[l].astype(F32), vec(ffn2_post_g[l]))
    return h.reshape(batch, seq, d)
```

```python
import functools

import numpy as np
import jax
import jax.numpy as jnp
from jax import lax
from jax.experimental import pallas as pl
from jax.experimental.pallas import tpu as pltpu

F32 = jnp.float32
BF16 = jnp.bfloat16

D_MODEL = 1024
GRID_W = 64
D_FF = 2816
EPS = 1e-6
GLA_HEADS = 4
GLA_DK = 128
GLA_DV = 256
GLA_RANK = 16
GLA_TAU = 16.0
ATT_Q_HEADS = 8
ATT_KV_HEADS = 2
ATT_HEAD_DIM = 128
ATT_GROUP = ATT_Q_HEADS // ATT_KV_HEADS
ROPE_THETA = 10000.0
GLA_KEY_W = GLA_HEADS * GLA_DK
GLA_VAL_W = GLA_HEADS * GLA_DV
ATT_Q_W = ATT_Q_HEADS * ATT_HEAD_DIM
ATT_KV_W = ATT_KV_HEADS * ATT_HEAD_DIM
IN_SPLITS = (GLA_KEY_W, GLA_KEY_W, GLA_VAL_W, GLA_VAL_W, GLA_RANK, GLA_RANK,
             ATT_Q_W, ATT_KV_W, ATT_KV_W, D_MODEL, D_MODEL)
IN_OFFSETS = tuple(int(s) for s in np.cumsum(IN_SPLITS)[:-1])
LOG2_E = 1.4426950408889634

V7X_LANES = 128
V7X_VMEM_BYTES = 64 * 1024 * 1024
VMEM_LIMIT = V7X_VMEM_BYTES - 8 * 1024 * 1024

_GROUP_WIDTHS = (V7X_LANES, ATT_Q_W + ATT_KV_W, 3 * D_MODEL, 2 * GLA_KEY_W + GLA_VAL_W + ATT_KV_W)
PROJ_GROUPS = tuple(zip(np.cumsum((0,) + _GROUP_WIDTHS[:-1]).tolist(), np.cumsum(_GROUP_WIDTHS).tolist()))

FFN_STEP_ROWS = 1024
MERGE_STEP_ROWS = 512
FFN_ROWS = 256
FFN_WEIGHT_STEPS = 8
MERGE_WEIGHT_STEPS = 16
PROJ_ROW_TILE = 512
PROJ_ROWS = 256
ATT_Q_TILE = 256
ATT_SHIFT_LIMIT_LOG2 = 60.0
GLA_C = 64
GLA_LEVELS = GLA_C.bit_length() - 1
GLA_SLAB = 64
GLA_WIDE_C = 256
GLA_MILD_LOG2 = 48.0
FF_CHUNKS = ((0, 1024), (1024, 1024), (2048, 768))


def _const_spec(shape):
    zeros = (0,) * len(shape)
    return pl.BlockSpec(shape, lambda *_: zeros, pipeline_mode=pl.Buffered(1))


def _params(n_axes):
    return pltpu.CompilerParams(dimension_semantics=("arbitrary",) * n_axes,
                                vmem_limit_bytes=VMEM_LIMIT)


def _rms(x, g):
    ms = jnp.mean(x * x, axis=-1, keepdims=True)
    return x * lax.rsqrt(ms + EPS) * g


def _dot(a, b):
    return jnp.dot(a, b, preferred_element_type=F32)


def _dot_nt(a, b):
    return lax.dot_general(a, b, (((1,), (1,)), ((), ())), preferred_element_type=F32)


def _dot_tn(a, b):
    return lax.dot_general(a, b, (((0,), (0,)), ((), ())), preferred_element_type=F32)


def _row_parts(ref, rows):
    rows = min(rows, ref.shape[0])
    return [slice(r, r + rows) for r in range(0, ref.shape[0], rows)]


def _ffn_rows(x, gpre_ref, win_ref, wout_ref, gpost_ref):
    xn = _rms(x, gpre_ref[...]).astype(BF16)
    acc = None
    for c0, cw in FF_CHUNKS:
        gate = _dot(xn, win_ref[:, c0:c0 + cw])
        up = _dot(xn, win_ref[:, D_FF + c0:D_FF + c0 + cw])
        act = (gate * jax.nn.sigmoid(gate) * up).astype(BF16)
        part = _dot(act, wout_ref[c0:c0 + cw, :])
        acc = part if acc is None else acc + part
    return x + 0.5 * _rms(acc, gpost_ref[...])


def _weight_chunk_spec(shape, steps):
    return pl.BlockSpec((shape[0] // steps, shape[1]), lambda i: (jnp.minimum(i, steps - 1), 0))


def _row_tile_spec(rows, steps):
    return pl.BlockSpec((rows, D_MODEL), lambda i: (jnp.maximum(i - steps, 0), 0))


def _cast_weight_chunk(step, src_ref, dst_ref):
    rows = src_ref.shape[0]
    dst_ref[pl.ds(pl.multiple_of(step * rows, rows), rows), :] = src_ref[...].astype(BF16)


def _ffn_kernel(x_ref, gpre_ref, win32_ref, wout32_ref, gpost_ref, o_ref, win_ref, wout_ref):
    step = pl.program_id(0)

    @pl.when(step < FFN_WEIGHT_STEPS)
    def _():
        _cast_weight_chunk(step, win32_ref, win_ref)
        _cast_weight_chunk(step, wout32_ref, wout_ref)

    @pl.when(step >= FFN_WEIGHT_STEPS)
    def _():
        for rows in _row_parts(x_ref, FFN_ROWS):
            o_ref[rows, :] = _ffn_rows(x_ref[rows, :], gpre_ref, win_ref, wout_ref, gpost_ref)


def _ffn(x, g_pre, w_in, w_out, g_post):
    t = x.shape[0]
    tm = min(FFN_STEP_ROWS, t)
    steps = FFN_WEIGHT_STEPS
    row = _row_tile_spec(tm, steps)
    return pl.pallas_call(
        _ffn_kernel,
        grid=(steps + t // tm,),
        in_specs=[row, _const_spec((1, D_MODEL)), _weight_chunk_spec(w_in.shape, steps),
                  _weight_chunk_spec(w_out.shape, steps), _const_spec((1, D_MODEL))],
        out_specs=row,
        out_shape=jax.ShapeDtypeStruct((t, D_MODEL), F32),
        scratch_shapes=[pltpu.VMEM(w_in.shape, BF16), pltpu.VMEM(w_out.shape, BF16)],
        compiler_params=_params(1),
        name="ffn",
    )(x, g_pre, w_in, w_out, g_post)


def _log_sigmoid(z):
    return jnp.minimum(z, 0.0) - jnp.log(1.0 + jnp.exp(-jnp.abs(z)))


def _rope(x, cos, sin_up, sin_dn):
    return (x * cos + pltpu.roll(x, V7X_LANES - 32, axis=1) * sin_up
            + pltpu.roll(x, 32, axis=1) * sin_dn)


def _store_heads(o_ref, rows, value):
    heads, _, width = o_ref.shape
    for hd in range(heads):
        o_ref[hd, rows, :] = value[:, hd * width:(hd + 1) * width]


def _proj_kernel(h_ref, gmix_ref, w_ref, upf_ref, upb_ref,
                 bf_ref, bb_ref, qg_ref, kg_ref, cos_ref, sup_ref, sdn_ref,
                 gq_o, gk_o, gv_o, sr_o, lf_o, lb_o, aq_o, ak_o, av_o, sga_o, sgb_o, dmin_o):
    qg = qg_ref[...] * (ATT_HEAD_DIM ** -0.5 * LOG2_E)
    worst = None
    for rows in _row_parts(h_ref, PROJ_ROWS):
        u = _rms(h_ref[rows, :], gmix_ref[...]).astype(BF16)
        wz_ref, watt_ref, wgate_ref, wplain_ref = [w_ref.at[:, a:b] for a, b in PROJ_GROUPS]
        za = _dot(u, wz_ref[...]).astype(BF16)

        cos, sup, sdn = cos_ref[rows, :], sup_ref[rows, :], sdn_ref[rows, :]
        att = _dot(u, watt_ref[...])
        for hd in range(ATT_Q_HEADS):
            sl = slice(hd * ATT_HEAD_DIM, (hd + 1) * ATT_HEAD_DIM)
            aq_o[hd, rows, :] = _rope(_rms(att[:, sl], qg), cos, sup, sdn).astype(BF16)
        for hd in range(ATT_KV_HEADS):
            sl = slice(hd * ATT_HEAD_DIM, (hd + 1) * ATT_HEAD_DIM)
            ak_o[hd, rows, :] = _rope(_rms(att[:, ATT_Q_W + sl.start:ATT_Q_W + sl.stop], kg_ref[...]),
                                      cos, sup, sdn).astype(BF16)

        gate = _dot(u, wgate_ref[...])
        sga_o[rows, :] = jax.nn.sigmoid(gate[:, 0:D_MODEL]).astype(BF16)
        sgb_o[rows, :] = jax.nn.sigmoid(gate[:, D_MODEL:2 * D_MODEL]).astype(BF16)
        gr = gate[:, 2 * D_MODEL:]
        _store_heads(sr_o, rows, (gr * jax.nn.sigmoid(gr)).astype(BF16))
        lf = _log_sigmoid(_dot(za, upf_ref[...]) + bf_ref[...]) * (LOG2_E / GLA_TAU)
        lb = _log_sigmoid(_dot(za, upb_ref[...]) + bb_ref[...]) * (LOG2_E / GLA_TAU)
        _store_heads(lf_o, rows, lf)
        _store_heads(lb_o, rows, lb)
        totals = jnp.minimum(lf.reshape(-1, GLA_WIDE_C, GLA_KEY_W).sum(axis=1),
                             lb.reshape(-1, GLA_WIDE_C, GLA_KEY_W).sum(axis=1))
        part = jnp.min(totals, axis=0, keepdims=True)
        worst = part if worst is None else jnp.minimum(worst, part)

        plain = _dot(u, wplain_ref[...])
        _store_heads(gq_o, rows, (plain[:, 0:GLA_KEY_W] * (GLA_DK ** -0.5)).astype(BF16))
        _store_heads(gk_o, rows, plain[:, GLA_KEY_W:2 * GLA_KEY_W].astype(BF16))
        _store_heads(gv_o, rows, plain[:, 2 * GLA_KEY_W:2 * GLA_KEY_W + GLA_VAL_W].astype(BF16))
        _store_heads(av_o, rows, plain[:, 2 * GLA_KEY_W + GLA_VAL_W:].astype(BF16))
    worst = functools.reduce(jnp.minimum, [worst[:, l:l + V7X_LANES] for l in range(0, GLA_KEY_W, V7X_LANES)])
    dmin_o[...] = jnp.broadcast_to(worst, dmin_o.shape)


def _rope_tables(seq):
    pos = np.arange(seq)
    half = ATT_HEAD_DIM // 2
    inv_freq = ROPE_THETA ** (-np.arange(0, half, 2, dtype=np.float64) / half)
    ang_r = (pos // GRID_W)[:, None] * inv_freq
    ang_c = (pos % GRID_W)[:, None] * inv_freq
    zero = np.zeros_like(ang_r)
    cos = np.concatenate([np.cos(ang_r)] * 2 + [np.cos(ang_c)] * 2, axis=-1)
    sin_up = np.concatenate([-np.sin(ang_r), zero, -np.sin(ang_c), zero], axis=-1)
    sin_dn = np.concatenate([zero, np.sin(ang_r), zero, np.sin(ang_c)], axis=-1)
    return tuple(jnp.asarray(t, F32) for t in (cos, sin_up, sin_dn))


def _proj(h, seq, g_mix, w_grouped, up_f, up_b, bias_f, bias_b, q_gain, k_gain):
    t = h.shape[0]
    tm = min(PROJ_ROW_TILE, seq)
    tiles_per_seq = seq // tm
    cos, sin_up, sin_dn = _rope_tables(seq)

    def row(width):
        return pl.BlockSpec((tm, width), lambda i: (i, 0))

    table = pl.BlockSpec((tm, ATT_HEAD_DIM), lambda i: (i % tiles_per_seq, 0))
    consts_a = (g_mix, w_grouped, up_f, up_b, bias_f, bias_b, q_gain, k_gain)
    outs = ((GLA_HEADS, GLA_DK, BF16), (GLA_HEADS, GLA_DK, BF16), (GLA_HEADS, GLA_DV, BF16),
            (GLA_HEADS, GLA_DV, BF16), (GLA_HEADS, GLA_DK, F32), (GLA_HEADS, GLA_DK, F32),
            (ATT_Q_HEADS, ATT_HEAD_DIM, BF16), (ATT_KV_HEADS, ATT_HEAD_DIM, BF16),
            (ATT_KV_HEADS, ATT_HEAD_DIM, BF16), (None, D_MODEL, BF16), (None, D_MODEL, BF16))
    return pl.pallas_call(
        _proj_kernel,
        grid=(t // tm,),
        in_specs=[row(D_MODEL)] + [_const_spec(c.shape) for c in consts_a] + [table] * 3,
        out_specs=[row(w) if hds is None else pl.BlockSpec((hds, tm, w), lambda i: (0, i, 0))
                   for hds, w, _ in outs] + [pl.BlockSpec((8, V7X_LANES), lambda i: (i, 0))],
        out_shape=[jax.ShapeDtypeStruct((t, w) if hds is None else (hds, t, w), d) for hds, w, d in outs]
                  + [jax.ShapeDtypeStruct((8 * (t // tm), V7X_LANES), F32)],
        compiler_params=_params(1),
        name="proj",
    )(h, *consts_a, cos, sin_up, sin_dn)


def _gla_constants():
    c, nl = GLA_C, GLA_LEVELS
    i = np.arange(c)[:, None]
    m = np.arange(c)[None, :]
    sum_f = [m <= i, m > i]
    sum_b = [m >= i, m < i]
    role_f, role_b, mask_f, mask_b = [], [], [], []
    for lvl in range(nl):
        s = c >> (lvl + 1)
        start = (i // (2 * s)) * 2 * s
        mid = start + s
        upper = (i % (2 * s)) >= s
        sum_f.append(np.where(upper, (m >= mid) & (m <= i), (m > i) & (m < mid)))
        sum_b.append(np.where(upper, (m >= mid) & (m < i), (m >= i) & (m < mid)))
        same = (i // (2 * s)) == (m // (2 * s))
        m_upper = (m % (2 * s)) >= s
        role_f.append(np.broadcast_to(upper, (c, V7X_LANES)))
        role_b.append(np.broadcast_to(~upper, (c, V7X_LANES)))
        mask_f.append(same & upper & ~m_upper)
        mask_b.append(same & ~upper & m_upper)
    sum_f = np.concatenate(sum_f, axis=0).astype(np.float32)
    sum_b = np.concatenate(sum_b, axis=0).astype(np.float32)
    sum_f = np.concatenate([sum_f, sum_f], axis=1)
    sum_b = np.concatenate([sum_b, sum_b], axis=1)
    roles = np.stack(role_f + role_b).astype(np.float32)
    masks = np.stack(mask_f + mask_b + [i == m]).astype(np.float32)
    return jnp.asarray(sum_f, BF16), jnp.asarray(sum_b, BF16), jnp.asarray(roles), jnp.asarray(masks)


def _gla_wide_prefix_matrix():
    wide = np.arange(GLA_WIDE_C)
    tri = (wide[None, :] <= wide[:, None]).astype(np.float32)
    return jnp.asarray(np.concatenate([tri, tri], axis=1), BF16)


def _decay_sums(g, sum_ref, n_rows):
    g_hi = g.astype(BF16)
    g_lo = (g - g_hi.astype(F32)).astype(BF16)
    return _dot(sum_ref[0:n_rows, :], jnp.concatenate([g_hi, g_lo], axis=0))


def _gla_scratch(n_chunks, rows):
    return [pltpu.VMEM((rows, GLA_DV), F32),
            pltpu.VMEM((rows, 2 * GLA_DK), BF16),
            pltpu.VMEM((n_chunks, GLA_DV, 2 * GLA_DK), F32),
            pltpu.VMEM((n_chunks, GLA_DV, 2 * GLA_DK), BF16),
            pltpu.VMEM((n_chunks * 8, 2 * GLA_DK), F32)]


def _gla_store_edges(scratch, i, rows, v, qe_f, qe_b, kd_f, kd_b, total_f, total_b):
    _, qe_ref, upd_ref, _, dec_ref = scratch
    qe_ref[rows, :] = jnp.concatenate([qe_f, qe_b], axis=1).astype(BF16)
    kd = jnp.concatenate([kd_f, kd_b], axis=1).astype(BF16)
    upd_ref[i] = _dot_tn(v, kd)
    total = jnp.concatenate([total_f, total_b], axis=1)
    dec_ref[pl.ds(pl.multiple_of(i * 8, 8), 8), :] = jnp.broadcast_to(total, (8, 2 * GLA_DK))


def _gla_scan_step(scratch, srows, i, j, sf, sb):
    _, _, upd_ref, snap_ref, dec_ref = scratch
    dk = GLA_DK
    snap_ref[i, srows, 0:dk] = sf.astype(BF16)
    snap_ref[j, srows, dk:2 * dk] = sb.astype(BF16)
    dec_f = dec_ref[pl.ds(pl.multiple_of(i * 8, 8), 8), 0:dk]
    dec_b = dec_ref[pl.ds(pl.multiple_of(j * 8, 8), 8), dk:2 * dk]
    sf = (sf.reshape(-1, 8, dk) * dec_f[None]).reshape(GLA_SLAB, dk) + upd_ref[i, srows, 0:dk]
    sb = (sb.reshape(-1, 8, dk) * dec_b[None]).reshape(GLA_SLAB, dk) + upd_ref[j, srows, dk:2 * dk]
    return sf, sb


def _gla_finish(scratch, rows, o_inter, gain_ref, sr_ref, o_ref):
    o = scratch[0][rows, :] + o_inter
    o_ref[rows, :] = (_rms(o, gain_ref[...]) * sr_ref[rows, :].astype(F32)).astype(BF16)


def _gla_any_decay(q_ref, k_ref, v_ref, sr_ref, gf_ref, gb_ref, gain_ref,
                   sumf_ref, sumb_ref, role_ref, mask_ref, o_ref, scratch):
    c, nl, dv = GLA_C, GLA_LEVELS, GLA_DV
    n_chunks = q_ref.shape[0] // c
    acc_ref, qe_ref, _, snap_ref, _ = scratch

    def chunk_any_decay(i, carry):
        rows = pl.ds(pl.multiple_of(i * c, c), c)
        q, k, v = q_ref[rows, :], k_ref[rows, :], v_ref[rows, :]
        qf, kf = q.astype(F32), k.astype(F32)
        ef = jnp.exp2(_decay_sums(gf_ref[rows, :], sumf_ref, (2 + nl) * c))
        eb = jnp.exp2(_decay_sums(gb_ref[rows, :], sumb_ref, (2 + nl) * c))
        a = _dot_nt(q, k) * mask_ref[2 * nl]
        for lvl in range(nl):
            blk = slice((2 + lvl) * c, (3 + lvl) * c)
            xf = (jnp.where(role_ref[lvl] > 0.5, qf, kf) * ef[blk]).astype(BF16)
            a = a + _dot_nt(xf, xf) * mask_ref[lvl]
            xb = (jnp.where(role_ref[nl + lvl] > 0.5, qf, kf) * eb[blk]).astype(BF16)
            a = a + _dot_nt(xb, xb) * mask_ref[nl + lvl]
        acc_ref[rows, :] = _dot(a.astype(BF16), v)
        _gla_store_edges(scratch, i, rows, v, qf * ef[0:c], qf * eb[0:c], kf * ef[c:2 * c], kf * eb[c:2 * c],
                         ef[c - 1:c], eb[0:1])
        return carry

    lax.fori_loop(0, n_chunks, chunk_any_decay, 0)

    for slab in range(dv // GLA_SLAB):
        srows = slice(slab * GLA_SLAB, (slab + 1) * GLA_SLAB)
        zero = jnp.zeros((GLA_SLAB, GLA_DK), F32)
        lax.fori_loop(0, n_chunks,
                      lambda step, carry: _gla_scan_step(scratch, srows, step, n_chunks - 1 - step, *carry),
                      (zero, zero), unroll=2)

    def finish(i, carry):
        rows = pl.ds(pl.multiple_of(i * c, c), c)
        _gla_finish(scratch, rows, _dot_nt(qe_ref[rows, :], snap_ref[i]), gain_ref, sr_ref, o_ref)
        return carry

    lax.fori_loop(0, n_chunks, finish, 0)


def _gla_head_spec(seq, width):
    return pl.BlockSpec((None, seq, width), lambda b, h: (h, b, 0))


def _score_bound(q_gain, k_gain):
    return (jnp.max(jnp.abs(q_gain)) * jnp.max(jnp.abs(k_gain))
            * (1.02 * ATT_HEAD_DIM ** 0.5 * LOG2_E))


class _GlaMild:
    def __init__(self, q_ref, k_ref, v_ref, sr_ref, gf_ref, gb_ref, gain_ref, tri_ref, o_ref, scratch):
        self.refs = (q_ref, k_ref, v_ref, sr_ref, gf_ref, gb_ref, gain_ref, tri_ref, o_ref)
        self.scratch = scratch
        w = GLA_WIDE_C
        self.n = q_ref.shape[0] // w
        self.rows = [slice(i * w, (i + 1) * w) for i in range(self.n)]

    def prefix_sums(self):
        _, _, _, _, gf_ref, gb_ref, _, tri_ref, _ = self.refs
        self.g_b = [gb_ref[r, :] for r in self.rows]
        self.prefix = [_decay_sums(jnp.concatenate([gf_ref[r, :], g], axis=1), tri_ref, GLA_WIDE_C)
                       for r, g in zip(self.rows, self.g_b)]

    def scores(self):
        q_ref, k_ref = self.refs[0:2]
        w, dk = GLA_WIDE_C, GLA_DK
        self.score, self.edge = [], []
        for r, g, p in zip(self.rows, self.g_b, self.prefix):
            qf, kf = q_ref[r, :].astype(F32), k_ref[r, :].astype(F32)
            bf, pb = p[:, 0:dk], p[:, dk:2 * dk]
            bb = (pb[w - 1:w] - pb) + g
            mid_f, mid_b = bf[w // 2 - 1:w // 2], bb[w // 2:w // 2 + 1]
            rf, rb = bf - mid_f, bb - mid_b
            q_f, k_f = qf * jnp.exp2(rf), kf * jnp.exp2(-rf)
            q_b, k_b = qf * jnp.exp2(rb), kf * jnp.exp2(-rb)
            self.score.append((_dot_nt(q_f.astype(BF16), k_f.astype(BF16)),
                               _dot_nt(q_b.astype(BF16), k_b.astype(BF16))))
            self.edge.append((q_f * jnp.exp2(mid_f), q_b * jnp.exp2(mid_b),
                              k_f * jnp.exp2(bf[w - 1:w] - mid_f), k_b * jnp.exp2(bb[0:1] - mid_b),
                              jnp.exp2(bf[w - 1:w]), jnp.exp2(bb[0:1])))

    def intra(self):
        v_ref, w = self.refs[2], GLA_WIDE_C
        lower = lax.broadcasted_iota(jnp.int32, (w, w), 0) >= lax.broadcasted_iota(jnp.int32, (w, w), 1)
        for r, (s_f, s_b) in zip(self.rows, self.score):
            self.scratch[0][r, :] = _dot(jnp.where(lower, s_f, s_b).astype(BF16), v_ref[r, :])

    def edges(self):
        v_ref = self.refs[2]
        for i, (r, parts) in enumerate(zip(self.rows, self.edge)):
            _gla_store_edges(self.scratch, i, r, v_ref[r, :], *parts)

    def scan(self):
        for slab in range(GLA_DV // GLA_SLAB):
            srows = slice(slab * GLA_SLAB, (slab + 1) * GLA_SLAB)
            sf = sb = jnp.zeros((GLA_SLAB, GLA_DK), F32)
            for step in range(self.n):
                sf, sb = _gla_scan_step(self.scratch, srows, step, self.n - 1 - step, sf, sb)

    def inter(self):
        qe_ref, snap_ref = self.scratch[1], self.scratch[3]
        self.o_inter = [_dot_nt(qe_ref[r, :], snap_ref[i]) for i, r in enumerate(self.rows)]

    def finish(self):
        sr_ref, gain_ref, o_ref = self.refs[3], self.refs[6], self.refs[8]
        for r, o_inter in zip(self.rows, self.o_inter):
            _gla_finish(self.scratch, r, o_inter, gain_ref, sr_ref, o_ref)


def _mixers_kernel(limits_ref, gq_ref, gk_ref, gv_ref, sr_ref, gf_ref, gb_ref, gain_ref,
                   sumf_ref, sumb_ref, role_ref, mask_ref, tri_ref, aq_ref, ak_ref, av_ref,
                   oa_ref, ob_ref, vext_ref, *scratch):
    hd, tq = ATT_HEAD_DIM, min(ATT_Q_TILE, aq_ref.shape[1])
    tiles = aq_ref.shape[1] // tq
    bound = limits_ref[0]
    vext_ref[:, :hd] = av_ref[...]
    vext_ref[:, hd:] = jnp.ones((av_ref.shape[0], hd), BF16)

    def weighted_values(p):
        o_ext = _dot(p, vext_ref[...])
        return (o_ext[:, :hd] / o_ext[:, hd:]).astype(BF16)

    mild = jnp.logical_and(bound <= ATT_SHIFT_LIMIT_LOG2, limits_ref[1] >= -GLA_MILD_LOG2)

    @pl.when(mild)
    def _():
        def attention_unit(tile, g):
            rows = slice(tile * tq, (tile + 1) * tq)
            p = jnp.exp2(_dot_nt(aq_ref[g, rows, :], ak_ref[...]) - bound).astype(BF16)
            ob_ref[rows, g * hd:(g + 1) * hd] = weighted_values(p)

        units = [(tile, g) for tile in range(tiles) for g in range(ATT_GROUP)]
        gla = _GlaMild(gq_ref, gk_ref, gv_ref, sr_ref, gf_ref, gb_ref, gain_ref, tri_ref, oa_ref, scratch)
        stages = [gla.prefix_sums, gla.scores, gla.intra, gla.edges, gla.scan, gla.inter, gla.finish]
        per_stage = -(-len(units) // (len(stages) + 1))
        for stage in stages:
            for unit in units[:per_stage]:
                attention_unit(*unit)
            units = units[per_stage:]
            stage()
        for unit in units:
            attention_unit(*unit)

    @pl.when(jnp.logical_not(mild))
    def _():
        _gla_any_decay(gq_ref, gk_ref, gv_ref, sr_ref, gf_ref, gb_ref, gain_ref,
                       sumf_ref, sumb_ref, role_ref, mask_ref, oa_ref, scratch)
        for tile in range(tiles):
            rows = slice(tile * tq, (tile + 1) * tq)
            q = aq_ref[:, rows, :].reshape(ATT_GROUP * tq, hd)
            s = _dot_nt(q, ak_ref[...])
            o = weighted_values(jnp.exp2(s - jnp.max(s, axis=-1, keepdims=True)).astype(BF16))
            for g in range(ATT_GROUP):
                ob_ref[rows, g * hd:(g + 1) * hd] = o[g * tq:(g + 1) * tq]


def _mixers(limits, gq, gk, gv, sr, lgf, lgb, gain, aq, ak, av, batch, seq):
    share = GLA_HEADS // ATT_KV_HEADS
    rows = seq // share
    head = functools.partial(_gla_head_spec, seq)
    qspec = pl.BlockSpec((ATT_GROUP, rows, ATT_HEAD_DIM), lambda b, h: (h // share, b * share + h % share, 0))
    kvspec = pl.BlockSpec((None, seq, ATT_HEAD_DIM), lambda b, h: (h // share, b, 0))
    obspec = pl.BlockSpec((None, rows, ATT_GROUP * ATT_HEAD_DIM), lambda b, h: (b, h % share, h // share))
    consts = (*_gla_constants(), _gla_wide_prefix_matrix())
    return pl.pallas_call(
        _mixers_kernel,
        grid=(batch, GLA_HEADS),
        in_specs=[pl.BlockSpec(memory_space=pltpu.SMEM),
                  head(GLA_DK), head(GLA_DK), head(GLA_DV), head(GLA_DV), head(GLA_DK), head(GLA_DK),
                  pl.BlockSpec((None, 1, GLA_DV), lambda b, h: (h, 0, 0))]
                 + [_const_spec(cst.shape) for cst in consts] + [qspec, kvspec, kvspec],
        out_specs=[pl.BlockSpec((None, seq, GLA_DV), lambda b, h: (b, 0, h)), obspec],
        out_shape=[jax.ShapeDtypeStruct((batch, seq, GLA_VAL_W), BF16),
                   jax.ShapeDtypeStruct((batch, seq, ATT_Q_W), BF16)],
        scratch_shapes=[pltpu.VMEM((seq, 2 * ATT_HEAD_DIM), BF16)] + _gla_scratch(seq // GLA_C, seq),
        compiler_params=_params(2),
        name="mixers",
    )(limits, gq, gk, gv, sr, lgf, lgb, gain, *consts, aq, ak, av)


def _merge_ffn_kernel(h_ref, oa_ref, ob_ref, sga_ref, sgb_ref, wa32_ref, wb32_ref, wo32_ref, gmix_ref,
                      gpre_ref, win32_ref, wout32_ref, gpost_ref, o_ref,
                      wa_ref, wb_ref, wo_ref, win_ref, wout_ref):
    step = pl.program_id(0)

    @pl.when(step < MERGE_WEIGHT_STEPS)
    def _():
        for src, dst in ((wa32_ref, wa_ref), (wb32_ref, wb_ref), (wo32_ref, wo_ref),
                         (win32_ref, win_ref), (wout32_ref, wout_ref)):
            _cast_weight_chunk(step, src, dst)

    def merged_rows(rows):
        ya = _dot(oa_ref[rows, :], wa_ref[...])
        yb = _dot(ob_ref[rows, :], wb_ref[...])
        merged = sga_ref[rows, :].astype(F32) * ya + sgb_ref[rows, :].astype(F32) * yb
        m = _dot(merged.astype(BF16), wo_ref[...])
        return h_ref[rows, :] + _rms(m, gmix_ref[...])

    @pl.when(step >= MERGE_WEIGHT_STEPS)
    def _():
        parts = _row_parts(h_ref, FFN_ROWS)
        h2 = merged_rows(parts[0])
        for rows, nxt in zip(parts, parts[1:] + [None]):
            h2_next = merged_rows(nxt) if nxt is not None else None
            o_ref[rows, :] = _ffn_rows(h2, gpre_ref, win_ref, wout_ref, gpost_ref)
            h2 = h2_next


def _merge_ffn(h, oa, ob, sga, sgb, w_a, w_b, w_o, g_mix, g_pre, w_in, w_out, g_post):
    t = h.shape[0]
    tm = min(MERGE_STEP_ROWS, t)
    steps = MERGE_WEIGHT_STEPS
    row = _row_tile_spec(tm, steps)
    vec = _const_spec((1, D_MODEL))
    weights = (w_a, w_b, w_o, w_in, w_out)
    chunk = [_weight_chunk_spec(w.shape, steps) for w in weights]
    return pl.pallas_call(
        _merge_ffn_kernel,
        grid=(steps + t // tm,),
        in_specs=[row] * 5 + chunk[0:3] + [vec, vec] + chunk[3:5] + [vec],
        out_specs=row,
        out_shape=jax.ShapeDtypeStruct((t, D_MODEL), F32),
        scratch_shapes=[pltpu.VMEM(w.shape, BF16) for w in weights],
        compiler_params=_params(1),
        name="merge_ffn",
    )(h, oa, ob, sga, sgb, w_a, w_b, w_o, g_mix, g_pre, w_in, w_out, g_post)


def _regroup_w_in(w_in):
    gq, gk, gv, gr, za_f, za_b, aq, ak, av, ga, gb = jnp.split(w_in.T, IN_OFFSETS, axis=0)
    pad = jnp.zeros((V7X_LANES - 2 * GLA_RANK, D_MODEL), w_in.dtype)
    rows = jnp.concatenate([za_f, za_b, pad, aq, ak, ga, gb, gr, gq, gk, gv, av], axis=0)
    return rows.astype(BF16).T


def _pad_decay_up(up, first_row):
    out = jnp.zeros((V7X_LANES, GLA_KEY_W), BF16)
    return lax.dynamic_update_slice(out, up.astype(BF16), (first_row, 0))


def kernel(x, ffn1_pre_g, ffn1_w_in, ffn1_w_out, ffn1_post_g, mix_pre_g, w_in, gla_decay_up_f,
           gla_decay_bias_f, gla_decay_up_b, gla_decay_bias_b, gla_out_g, w_branch_a, att_q_norm_g,
           att_k_norm_g, w_branch_b, w_out, mix_post_g, ffn2_pre_g, ffn2_w_in, ffn2_w_out, ffn2_post_g):
    batch, seq, d = x.shape
    assert d == D_MODEL and seq % max(GLA_WIDE_C, GRID_W) == 0
    assert seq % ATT_Q_TILE == 0 or seq < ATT_Q_TILE
    assert seq % PROJ_ROW_TILE == 0 or seq < PROJ_ROW_TILE
    depth = w_in.shape[0]
    t = batch * seq
    h = x.reshape(t, d)
    vec = lambda g: g.reshape(1, -1).astype(F32)
    for l in range(depth):
        h = _ffn(h, vec(ffn1_pre_g[l]), ffn1_w_in[l].astype(F32), ffn1_w_out[l].astype(F32),
                 vec(ffn1_post_g[l]))
        gq, gk, gv, sr, lgf, lgb, aq, ak, av, sga, sgb, decay_min = _proj(
            h, seq, vec(mix_pre_g[l]), _regroup_w_in(w_in[l]),
            _pad_decay_up(gla_decay_up_f[l], 0), _pad_decay_up(gla_decay_up_b[l], GLA_RANK),
            vec(gla_decay_bias_f[l]), vec(gla_decay_bias_b[l]),
            vec(att_q_norm_g[l]), vec(att_k_norm_g[l]))
        limits = jnp.stack([_score_bound(att_q_norm_g[l], att_k_norm_g[l]), jnp.min(decay_min)]).astype(F32)
        oa, ob = _mixers(limits, gq, gk, gv, sr, lgf, lgb,
                         gla_out_g[l].reshape(GLA_HEADS, 1, GLA_DV).astype(F32), aq, ak, av, batch, seq)
        h = _merge_ffn(h, oa.reshape(t, GLA_VAL_W), ob.reshape(t, ATT_Q_W), sga, sgb,
                       w_branch_a[l].astype(F32), w_branch_b[l].astype(F32), w_out[l].astype(F32),
                       vec(mix_post_g[l]), vec(ffn2_pre_g[l]), ffn2_w_in[l].astype(F32),
                       ffn2_w_out[l].astype(F32), vec(ffn2_post_g[l]))
    return h.reshape(batch, seq, d)
```

```python
import functools

import numpy as np
import jax
import jax.numpy as jnp
from jax import lax
from jax.experimental import pallas as pl
from jax.experimental.pallas import tpu as pltpu

F32 = jnp.float32
BF16 = jnp.bfloat16

D_MODEL = 1024
GRID_W = 64
D_FF = 2816
EPS = 1e-6
GLA_HEADS = 4
GLA_DK = 128
GLA_DV = 256
GLA_RANK = 16
GLA_TAU = 16.0
ATT_Q_HEADS = 8
ATT_KV_HEADS = 2
ATT_HEAD_DIM = 128
ATT_GROUP = ATT_Q_HEADS // ATT_KV_HEADS
ROPE_THETA = 10000.0
GLA_KEY_W = GLA_HEADS * GLA_DK
GLA_VAL_W = GLA_HEADS * GLA_DV
ATT_Q_W = ATT_Q_HEADS * ATT_HEAD_DIM
ATT_KV_W = ATT_KV_HEADS * ATT_HEAD_DIM
IN_SPLITS = (GLA_KEY_W, GLA_KEY_W, GLA_VAL_W, GLA_VAL_W, GLA_RANK, GLA_RANK,
             ATT_Q_W, ATT_KV_W, ATT_KV_W, D_MODEL, D_MODEL)
IN_OFFSETS = tuple(int(s) for s in np.cumsum(IN_SPLITS)[:-1])
LOG2_E = 1.4426950408889634

V7X_LANES = 128
V7X_SUBLANES = 8
V7X_VMEM_BYTES = 64 * 1024 * 1024
VMEM_LIMIT = V7X_VMEM_BYTES - 8 * 1024 * 1024

FFN_STEP_ROWS = 1024
MERGE_STEP_ROWS = 512
FFN_ROWS = 256
FFN_WEIGHT_STEPS = 8
MERGE_WEIGHT_STEPS = 32
MERGE_WOUT_STEPS = 16
PROJ_ROW_TILE = 512
PROJ_ROWS = 256
ATT_Q_TILE = 256
ATT_SHIFT_LIMIT_LOG2 = 60.0
GLA_C = 64
GLA_LEVELS = GLA_C.bit_length() - 1
GLA_SLAB = 64
GLA_WIDE_C = 256
GLA_MILD_LOG2 = 48.0
FF_CHUNKS = ((0, 1024), (1024, 1024), (2048, 768))


def _const_spec(shape):
    zeros = (0,) * len(shape)
    return pl.BlockSpec(shape, lambda *_: zeros, pipeline_mode=pl.Buffered(1))


def _params(n_axes):
    return pltpu.CompilerParams(dimension_semantics=("arbitrary",) * n_axes,
                                vmem_limit_bytes=VMEM_LIMIT)


def _rms(x, g):
    ms = jnp.mean(x * x, axis=-1, keepdims=True)
    return x * lax.rsqrt(ms + EPS) * g


def _dot(a, b):
    return jnp.dot(a, b, preferred_element_type=F32)


def _dot_nt(a, b):
    return lax.dot_general(a, b, (((1,), (1,)), ((), ())), preferred_element_type=F32)


def _dot_tn(a, b):
    return lax.dot_general(a, b, (((0,), (0,)), ((), ())), preferred_element_type=F32)


def _row_parts(ref, rows):
    rows = min(rows, ref.shape[0])
    return [slice(r, r + rows) for r in range(0, ref.shape[0], rows)]


def _ffn_rows(x, gpre_ref, win_ref, wout_ref, gpost_ref):
    xn = _rms(x, gpre_ref[...]).astype(BF16)
    acc = None
    for c0, cw in FF_CHUNKS:
        gate = _dot(xn, win_ref[:, c0:c0 + cw])
        up = _dot(xn, win_ref[:, D_FF + c0:D_FF + c0 + cw])
        act = (gate * jax.nn.sigmoid(gate) * up).astype(BF16)
        part = _dot(act, wout_ref[c0:c0 + cw, :])
        acc = part if acc is None else acc + part
    return x + 0.5 * _rms(acc, gpost_ref[...])


def _weight_chunk_spec(shape, steps):
    return pl.BlockSpec((shape[0] // steps, shape[1]), lambda i: (jnp.minimum(i, steps - 1), 0))


def _row_tile_spec(rows, steps):
    return pl.BlockSpec((rows, D_MODEL), lambda i: (jnp.maximum(i - steps, 0), 0))


def _cast_weight_chunk(step, src_ref, dst_ref):
    rows = src_ref.shape[0]
    dst_ref[pl.ds(pl.multiple_of(step * rows, rows), rows), :] = src_ref[...].astype(BF16)


def _ffn_kernel(x_ref, gpre_ref, win32_ref, wout32_ref, gpost_ref, o_ref, win_ref, wout_ref):
    step = pl.program_id(0)

    @pl.when(step < FFN_WEIGHT_STEPS)
    def _():
        _cast_weight_chunk(step, win32_ref, win_ref)
        _cast_weight_chunk(step, wout32_ref, wout_ref)

    @pl.when(step >= FFN_WEIGHT_STEPS)
    def _():
        for rows in _row_parts(x_ref, FFN_ROWS):
            o_ref[rows, :] = _ffn_rows(x_ref[rows, :], gpre_ref, win_ref, wout_ref, gpost_ref)


def _ffn(x, g_pre, w_in, w_out, g_post):
    t = x.shape[0]
    tm = min(FFN_STEP_ROWS, t)
    steps = FFN_WEIGHT_STEPS
    row = _row_tile_spec(tm, steps)
    return pl.pallas_call(
        _ffn_kernel,
        grid=(steps + t // tm,),
        in_specs=[row, _const_spec((1, D_MODEL)), _weight_chunk_spec(w_in.shape, steps),
                  _weight_chunk_spec(w_out.shape, steps), _const_spec((1, D_MODEL))],
        out_specs=row,
        out_shape=jax.ShapeDtypeStruct((t, D_MODEL), F32),
        scratch_shapes=[pltpu.VMEM(w_in.shape, BF16), pltpu.VMEM(w_out.shape, BF16)],
        compiler_params=_params(1),
        name="ffn",
    )(x, g_pre, w_in, w_out, g_post)


def _log_sigmoid(z):
    return jnp.minimum(z, 0.0) - jnp.log(1.0 + jnp.exp(-jnp.abs(z)))


def _rope(x, cos, sin_up, sin_dn):
    return (x * cos + pltpu.roll(x, V7X_LANES - 32, axis=1) * sin_up
            + pltpu.roll(x, 32, axis=1) * sin_dn)


def _store_heads(o_ref, rows, value):
    heads, _, width = o_ref.shape
    for hd in range(heads):
        o_ref[hd, rows, :] = value[:, hd * width:(hd + 1) * width]


def _proj_kernel(h_ref, gmix_ref, wz_ref, watt_ref, wr_ref, wplain_ref, upf_ref, upb_ref,
                 bf_ref, bb_ref, qg_ref, kg_ref, cos_ref, sup_ref, sdn_ref,
                 gq_o, gk_o, gv_o, sr_o, lf_o, lb_o, aq_o, ak_o, av_o, dmin_o):
    qg = qg_ref[...] * (ATT_HEAD_DIM ** -0.5 * LOG2_E)
    worst = None
    for rows in _row_parts(h_ref, PROJ_ROWS):
        u = _rms(h_ref[rows, :], gmix_ref[...]).astype(BF16)
        za = _dot(u, wz_ref[...]).astype(BF16)

        cos, sup, sdn = cos_ref[rows, :], sup_ref[rows, :], sdn_ref[rows, :]
        att = _dot(u, watt_ref[...])
        for hd in range(ATT_Q_HEADS):
            sl = slice(hd * ATT_HEAD_DIM, (hd + 1) * ATT_HEAD_DIM)
            aq_o[hd, rows, :] = _rope(_rms(att[:, sl], qg), cos, sup, sdn).astype(BF16)
        for hd in range(ATT_KV_HEADS):
            sl = slice(hd * ATT_HEAD_DIM, (hd + 1) * ATT_HEAD_DIM)
            ak_o[hd, rows, :] = _rope(_rms(att[:, ATT_Q_W + sl.start:ATT_Q_W + sl.stop], kg_ref[...]),
                                      cos, sup, sdn).astype(BF16)

        gr = _dot(u, wr_ref[...])
        _store_heads(sr_o, rows, (gr * jax.nn.sigmoid(gr)).astype(BF16))
        lf = _log_sigmoid(_dot(za, upf_ref[...]) + bf_ref[...]) * (LOG2_E / GLA_TAU)
        lb = _log_sigmoid(_dot(za, upb_ref[...]) + bb_ref[...]) * (LOG2_E / GLA_TAU)
        _store_heads(lf_o, rows, lf)
        _store_heads(lb_o, rows, lb)
        totals = jnp.minimum(lf.reshape(-1, GLA_WIDE_C, GLA_KEY_W).sum(axis=1),
                             lb.reshape(-1, GLA_WIDE_C, GLA_KEY_W).sum(axis=1))
        part = jnp.min(totals, axis=0, keepdims=True)
        worst = part if worst is None else jnp.minimum(worst, part)

        plain = _dot(u, wplain_ref[...])
        _store_heads(gq_o, rows, (plain[:, 0:GLA_KEY_W] * (GLA_DK ** -0.5)).astype(BF16))
        _store_heads(gk_o, rows, plain[:, GLA_KEY_W:2 * GLA_KEY_W].astype(BF16))
        _store_heads(gv_o, rows, plain[:, 2 * GLA_KEY_W:2 * GLA_KEY_W + GLA_VAL_W].astype(BF16))
        _store_heads(av_o, rows, plain[:, 2 * GLA_KEY_W + GLA_VAL_W:].astype(BF16))
    worst = functools.reduce(jnp.minimum, [worst[:, l:l + V7X_LANES] for l in range(0, GLA_KEY_W, V7X_LANES)])
    dmin_o[...] = jnp.broadcast_to(worst, dmin_o.shape)


def _rope_tables(seq):
    pos = np.arange(seq)
    half = ATT_HEAD_DIM // 2
    inv_freq = ROPE_THETA ** (-np.arange(0, half, 2, dtype=np.float64) / half)
    ang_r = (pos // GRID_W)[:, None] * inv_freq
    ang_c = (pos % GRID_W)[:, None] * inv_freq
    zero = np.zeros_like(ang_r)
    cos = np.concatenate([np.cos(ang_r)] * 2 + [np.cos(ang_c)] * 2, axis=-1)
    sin_up = np.concatenate([-np.sin(ang_r), zero, -np.sin(ang_c), zero], axis=-1)
    sin_dn = np.concatenate([zero, np.sin(ang_r), zero, np.sin(ang_c)], axis=-1)
    return tuple(jnp.asarray(t, F32) for t in (cos, sin_up, sin_dn))


def _proj(h, seq, g_mix, w_parts, up_f, up_b, bias_f, bias_b, q_gain, k_gain):
    t = h.shape[0]
    tm = min(PROJ_ROW_TILE, seq)
    tiles_per_seq = seq // tm
    cos, sin_up, sin_dn = _rope_tables(seq)

    def row(width):
        return pl.BlockSpec((tm, width), lambda i: (i, 0))

    table = pl.BlockSpec((tm, ATT_HEAD_DIM), lambda i: (i % tiles_per_seq, 0))
    consts_a = (g_mix, *w_parts, up_f, up_b, bias_f, bias_b, q_gain, k_gain)
    outs = ((GLA_HEADS, GLA_DK, BF16), (GLA_HEADS, GLA_DK, BF16), (GLA_HEADS, GLA_DV, BF16),
            (GLA_HEADS, GLA_DV, BF16), (GLA_HEADS, GLA_DK, F32), (GLA_HEADS, GLA_DK, F32),
            (ATT_Q_HEADS, ATT_HEAD_DIM, BF16), (ATT_KV_HEADS, ATT_HEAD_DIM, BF16),
            (ATT_KV_HEADS, ATT_HEAD_DIM, BF16))
    return pl.pallas_call(
        _proj_kernel,
        grid=(t // tm,),
        in_specs=[row(D_MODEL)] + [_const_spec(c.shape) for c in consts_a] + [table] * 3,
        out_specs=[pl.BlockSpec((hds, tm, w), lambda i: (0, i, 0)) for hds, w, _ in outs]
                  + [pl.BlockSpec((V7X_SUBLANES, V7X_LANES), lambda i: (i, 0))],
        out_shape=[jax.ShapeDtypeStruct((hds, t, w), d) for hds, w, d in outs]
                  + [jax.ShapeDtypeStruct((V7X_SUBLANES * (t // tm), V7X_LANES), F32)],
        compiler_params=_params(1),
        name="proj",
    )(h, *consts_a, cos, sin_up, sin_dn)


def _gla_constants():
    c, nl = GLA_C, GLA_LEVELS
    i = np.arange(c)[:, None]
    m = np.arange(c)[None, :]
    sum_f = [m <= i, m > i]
    sum_b = [m >= i, m < i]
    role_f, role_b, mask_f, mask_b = [], [], [], []
    for lvl in range(nl):
        s = c >> (lvl + 1)
        start = (i // (2 * s)) * 2 * s
        mid = start + s
        upper = (i % (2 * s)) >= s
        sum_f.append(np.where(upper, (m >= mid) & (m <= i), (m > i) & (m < mid)))
        sum_b.append(np.where(upper, (m >= mid) & (m < i), (m >= i) & (m < mid)))
        same = (i // (2 * s)) == (m // (2 * s))
        m_upper = (m % (2 * s)) >= s
        role_f.append(np.broadcast_to(upper, (c, V7X_LANES)))
        role_b.append(np.broadcast_to(~upper, (c, V7X_LANES)))
        mask_f.append(same & upper & ~m_upper)
        mask_b.append(same & ~upper & m_upper)
    sum_f = np.concatenate(sum_f, axis=0).astype(np.float32)
    sum_b = np.concatenate(sum_b, axis=0).astype(np.float32)
    sum_f = np.concatenate([sum_f, sum_f], axis=1)
    sum_b = np.concatenate([sum_b, sum_b], axis=1)
    roles = np.stack(role_f + role_b).astype(np.float32)
    masks = np.stack(mask_f + mask_b + [i == m]).astype(np.float32)
    return jnp.asarray(sum_f, BF16), jnp.asarray(sum_b, BF16), jnp.asarray(roles), jnp.asarray(masks)


def _gla_wide_prefix_matrix():
    wide = np.arange(GLA_WIDE_C)
    tri = (wide[None, :] <= wide[:, None]).astype(np.float32)
    return jnp.asarray(np.concatenate([tri, tri], axis=1), BF16)


def _decay_sums(g, sum_ref, n_rows):
    g_hi = g.astype(BF16)
    g_lo = (g - g_hi.astype(F32)).astype(BF16)
    return _dot(sum_ref[0:n_rows, :], jnp.concatenate([g_hi, g_lo], axis=0))


def _gla_scratch(n_chunks, rows):
    return [pltpu.VMEM((rows, GLA_DV), F32),
            pltpu.VMEM((rows, 2 * GLA_DK), BF16),
            pltpu.VMEM((n_chunks, GLA_DV, 2 * GLA_DK), F32),
            pltpu.VMEM((n_chunks, GLA_DV, 2 * GLA_DK), BF16),
            pltpu.VMEM((n_chunks * V7X_SUBLANES, 2 * GLA_DK), F32)]


def _gla_store_edges(scratch, i, rows, v, qe_f, qe_b, kd_f, kd_b, total_f, total_b):
    _, qe_ref, upd_ref, _, dec_ref = scratch
    qe_ref[rows, :] = jnp.concatenate([qe_f, qe_b], axis=1).astype(BF16)
    kd = jnp.concatenate([kd_f, kd_b], axis=1).astype(BF16)
    upd_ref[i] = _dot_tn(v, kd)
    total = jnp.concatenate([total_f, total_b], axis=1)
    dec_ref[_sublane_tile(i), :] = jnp.broadcast_to(total, (V7X_SUBLANES, 2 * GLA_DK))


def _gla_scan_step(scratch, srows, i, j, sf, sb):
    _, _, upd_ref, snap_ref, dec_ref = scratch
    dk = GLA_DK
    snap_ref[i, srows, 0:dk] = sf.astype(BF16)
    snap_ref[j, srows, dk:2 * dk] = sb.astype(BF16)
    dec_f = dec_ref[_sublane_tile(i), 0:dk]
    dec_b = dec_ref[_sublane_tile(j), dk:2 * dk]
    sf = (sf.reshape(-1, V7X_SUBLANES, dk) * dec_f[None]).reshape(GLA_SLAB, dk) + upd_ref[i, srows, 0:dk]
    sb = (sb.reshape(-1, V7X_SUBLANES, dk) * dec_b[None]).reshape(GLA_SLAB, dk) + upd_ref[j, srows, dk:2 * dk]
    return sf, sb


def _sublane_tile(i):
    return pl.ds(pl.multiple_of(i * V7X_SUBLANES, V7X_SUBLANES), V7X_SUBLANES)


def _gla_finish(scratch, rows, o_inter, gain_ref, sr_ref, o_ref):
    o = scratch[0][rows, :] + o_inter
    o_ref[rows, :] = (_rms(o, gain_ref[...]) * sr_ref[rows, :].astype(F32)).astype(BF16)


def _gla_any_decay(q_ref, k_ref, v_ref, sr_ref, gf_ref, gb_ref, gain_ref,
                   sumf_ref, sumb_ref, role_ref, mask_ref, o_ref, scratch):
    c, nl, dv = GLA_C, GLA_LEVELS, GLA_DV
    n_chunks = q_ref.shape[0] // c
    acc_ref, qe_ref, _, snap_ref, _ = scratch

    def chunk_any_decay(i, carry):
        rows = pl.ds(pl.multiple_of(i * c, c), c)
        q, k, v = q_ref[rows, :], k_ref[rows, :], v_ref[rows, :]
        qf, kf = q.astype(F32), k.astype(F32)
        ef = jnp.exp2(_decay_sums(gf_ref[rows, :], sumf_ref, (2 + nl) * c))
        eb = jnp.exp2(_decay_sums(gb_ref[rows, :], sumb_ref, (2 + nl) * c))
        a = _dot_nt(q, k) * mask_ref[2 * nl]
        for lvl in range(nl):
            blk = slice((2 + lvl) * c, (3 + lvl) * c)
            xf = (jnp.where(role_ref[lvl] > 0.5, qf, kf) * ef[blk]).astype(BF16)
            a = a + _dot_nt(xf, xf) * mask_ref[lvl]
            xb = (jnp.where(role_ref[nl + lvl] > 0.5, qf, kf) * eb[blk]).astype(BF16)
            a = a + _dot_nt(xb, xb) * mask_ref[nl + lvl]
        acc_ref[rows, :] = _dot(a.astype(BF16), v)
        _gla_store_edges(scratch, i, rows, v, qf * ef[0:c], qf * eb[0:c], kf * ef[c:2 * c], kf * eb[c:2 * c],
                         ef[c - 1:c], eb[0:1])
        return carry

    lax.fori_loop(0, n_chunks, chunk_any_decay, 0)

    for slab in range(dv // GLA_SLAB):
        srows = slice(slab * GLA_SLAB, (slab + 1) * GLA_SLAB)
        zero = jnp.zeros((GLA_SLAB, GLA_DK), F32)
        lax.fori_loop(0, n_chunks,
                      lambda step, carry: _gla_scan_step(scratch, srows, step, n_chunks - 1 - step, *carry),
                      (zero, zero), unroll=2)

    def finish(i, carry):
        rows = pl.ds(pl.multiple_of(i * c, c), c)
        _gla_finish(scratch, rows, _dot_nt(qe_ref[rows, :], snap_ref[i]), gain_ref, sr_ref, o_ref)
        return carry

    lax.fori_loop(0, n_chunks, finish, 0)


def _gla_head_spec(seq, width):
    return pl.BlockSpec((None, seq, width), lambda b, h: (h, b, 0))


def _score_bound(q_gain, k_gain):
    return (jnp.max(jnp.abs(q_gain)) * jnp.max(jnp.abs(k_gain))
            * (1.02 * ATT_HEAD_DIM ** 0.5 * LOG2_E))


class _GlaMild:
    def __init__(self, q_ref, k_ref, v_ref, sr_ref, gf_ref, gb_ref, gain_ref, tri_ref, o_ref, scratch):
        self.refs = (q_ref, k_ref, v_ref, sr_ref, gf_ref, gb_ref, gain_ref, tri_ref, o_ref)
        self.scratch = scratch
        w = GLA_WIDE_C
        self.n = q_ref.shape[0] // w
        self.rows = [slice(i * w, (i + 1) * w) for i in range(self.n)]

    def prefix_sums(self):
        _, _, _, _, gf_ref, gb_ref, _, tri_ref, _ = self.refs
        self.g_b = [gb_ref[r, :] for r in self.rows]
        self.prefix = [_decay_sums(jnp.concatenate([gf_ref[r, :], g], axis=1), tri_ref, GLA_WIDE_C)
                       for r, g in zip(self.rows, self.g_b)]

    def scores(self):
        q_ref, k_ref = self.refs[0:2]
        w, dk = GLA_WIDE_C, GLA_DK
        self.score, self.edge = [], []
        for r, g, p in zip(self.rows, self.g_b, self.prefix):
            qf, kf = q_ref[r, :].astype(F32), k_ref[r, :].astype(F32)
            bf, pb = p[:, 0:dk], p[:, dk:2 * dk]
            bb = (pb[w - 1:w] - pb) + g
            mid_f, mid_b = bf[w // 2 - 1:w // 2], bb[w // 2:w // 2 + 1]
            rf, rb = bf - mid_f, bb - mid_b
            q_f, k_f = qf * jnp.exp2(rf), kf * jnp.exp2(-rf)
            q_b, k_b = qf * jnp.exp2(rb), kf * jnp.exp2(-rb)
            self.score.append((_dot_nt(q_f.astype(BF16), k_f.astype(BF16)),
                               _dot_nt(q_b.astype(BF16), k_b.astype(BF16))))
            self.edge.append((q_f * jnp.exp2(mid_f), q_b * jnp.exp2(mid_b),
                              k_f * jnp.exp2(bf[w - 1:w] - mid_f), k_b * jnp.exp2(bb[0:1] - mid_b),
                              jnp.exp2(bf[w - 1:w]), jnp.exp2(bb[0:1])))

    def intra(self):
        v_ref, w = self.refs[2], GLA_WIDE_C
        lower = lax.broadcasted_iota(jnp.int32, (w, w), 0) >= lax.broadcasted_iota(jnp.int32, (w, w), 1)
        for r, (s_f, s_b) in zip(self.rows, self.score):
            self.scratch[0][r, :] = _dot(jnp.where(lower, s_f, s_b).astype(BF16), v_ref[r, :])

    def edges(self):
        v_ref = self.refs[2]
        for i, (r, parts) in enumerate(zip(self.rows, self.edge)):
            _gla_store_edges(self.scratch, i, r, v_ref[r, :], *parts)

    def scan(self):
        for slab in range(GLA_DV // GLA_SLAB):
            srows = slice(slab * GLA_SLAB, (slab + 1) * GLA_SLAB)
            sf = sb = jnp.zeros((GLA_SLAB, GLA_DK), F32)
            for step in range(self.n):
                sf, sb = _gla_scan_step(self.scratch, srows, step, self.n - 1 - step, sf, sb)

    def inter(self):
        qe_ref, snap_ref = self.scratch[1], self.scratch[3]
        self.o_inter = [_dot_nt(qe_ref[r, :], snap_ref[i]) for i, r in enumerate(self.rows)]

    def finish(self):
        sr_ref, gain_ref, o_ref = self.refs[3], self.refs[6], self.refs[8]
        for r, o_inter in zip(self.rows, self.o_inter):
            _gla_finish(self.scratch, r, o_inter, gain_ref, sr_ref, o_ref)


def _mixers_kernel(limits_ref, gq_ref, gk_ref, gv_ref, sr_ref, gf_ref, gb_ref, gain_ref,
                   sumf_ref, sumb_ref, role_ref, mask_ref, tri_ref, aq_ref, ak_ref, av_ref,
                   oa_ref, ob_ref, vext_ref, kt_ref, *scratch):
    hd, tq = ATT_HEAD_DIM, min(ATT_Q_TILE, aq_ref.shape[1])
    tiles = aq_ref.shape[1] // tq
    bound = limits_ref[0]
    vext_ref[:, :hd] = av_ref[...]
    vext_ref[:, hd:] = jnp.ones((av_ref.shape[0], hd), BF16)

    def weighted_values(p):
        o_ext = _dot(p, vext_ref[...])
        return (o_ext[:, :hd] / o_ext[:, hd:]).astype(BF16)

    mild = jnp.logical_and(bound <= ATT_SHIFT_LIMIT_LOG2, limits_ref[1] >= -GLA_MILD_LOG2)

    @pl.when(mild)
    def _():
        kt_ref[...] = ak_ref[...].T
        def attention_unit(tile, g):
            rows = slice(tile * tq, (tile + 1) * tq)
            p = jnp.exp2(_dot(aq_ref[g, rows, :], kt_ref[...]) - bound).astype(BF16)
            ob_ref[rows, g * hd:(g + 1) * hd] = weighted_values(p)

        units = [(tile, g) for tile in range(tiles) for g in range(ATT_GROUP)]
        gla = _GlaMild(gq_ref, gk_ref, gv_ref, sr_ref, gf_ref, gb_ref, gain_ref, tri_ref, oa_ref, scratch)
        stages = [gla.prefix_sums, gla.scores, gla.intra, gla.edges, gla.scan, gla.inter, gla.finish]
        per_stage = -(-len(units) // (len(stages) + 1))
        for stage in stages:
            for unit in units[:per_stage]:
                attention_unit(*unit)
            units = units[per_stage:]
            stage()
        for unit in units:
            attention_unit(*unit)

    @pl.when(jnp.logical_not(mild))
    def _():
        _gla_any_decay(gq_ref, gk_ref, gv_ref, sr_ref, gf_ref, gb_ref, gain_ref,
                       sumf_ref, sumb_ref, role_ref, mask_ref, oa_ref, scratch)
        for tile in range(tiles):
            rows = slice(tile * tq, (tile + 1) * tq)
            q = aq_ref[:, rows, :].reshape(ATT_GROUP * tq, hd)
            s = _dot_nt(q, ak_ref[...])
            o = weighted_values(jnp.exp2(s - jnp.max(s, axis=-1, keepdims=True)).astype(BF16))
            for g in range(ATT_GROUP):
                ob_ref[rows, g * hd:(g + 1) * hd] = o[g * tq:(g + 1) * tq]


def _mixers(limits, gq, gk, gv, sr, lgf, lgb, gain, aq, ak, av, batch, seq):
    share = GLA_HEADS // ATT_KV_HEADS
    rows = seq // share
    head = functools.partial(_gla_head_spec, seq)
    qspec = pl.BlockSpec((ATT_GROUP, rows, ATT_HEAD_DIM), lambda b, h: (h // share, b * share + h % share, 0))
    kvspec = pl.BlockSpec((None, seq, ATT_HEAD_DIM), lambda b, h: (h // share, b, 0))
    obspec = pl.BlockSpec((None, rows, ATT_GROUP * ATT_HEAD_DIM), lambda b, h: (b, h % share, h // share))
    consts = (*_gla_constants(), _gla_wide_prefix_matrix())
    return pl.pallas_call(
        _mixers_kernel,
        grid=(batch, GLA_HEADS),
        in_specs=[pl.BlockSpec(memory_space=pltpu.SMEM),
                  head(GLA_DK), head(GLA_DK), head(GLA_DV), head(GLA_DV), head(GLA_DK), head(GLA_DK),
                  pl.BlockSpec((None, 1, GLA_DV), lambda b, h: (h, 0, 0))]
                 + [_const_spec(cst.shape) for cst in consts] + [qspec, kvspec, kvspec],
        out_specs=[pl.BlockSpec((None, seq, GLA_DV), lambda b, h: (b, 0, h)), obspec],
        out_shape=[jax.ShapeDtypeStruct((batch, seq, GLA_VAL_W), BF16),
                   jax.ShapeDtypeStruct((batch, seq, ATT_Q_W), BF16)],
        scratch_shapes=[pltpu.VMEM((seq, 2 * ATT_HEAD_DIM), BF16), pltpu.VMEM((ATT_HEAD_DIM, seq), BF16)]
                       + _gla_scratch(seq // GLA_C, seq),
        compiler_params=_params(2),
        name="mixers",
    )(limits, gq, gk, gv, sr, lgf, lgb, gain, *consts, aq, ak, av)


def _merge_ffn_kernel(h_ref, oa_ref, ob_ref, gmixpre_ref, wg32_ref, wa32_ref, wb32_ref, wo32_ref, gmix_ref,
                      gpre_ref, win32_ref, wout32_ref, gpost_ref, o_ref,
                      wg_ref, wa_ref, wb_ref, wo_ref, win_ref, wout_ref):
    step = pl.program_id(0)

    @pl.when(step < MERGE_WEIGHT_STEPS)
    def _():
        for src, dst in ((wg32_ref, wg_ref), (wa32_ref, wa_ref), (wb32_ref, wb_ref), (wo32_ref, wo_ref),
                         (win32_ref, win_ref)):
            _cast_weight_chunk(step, src, dst)

    @pl.when(step < MERGE_WOUT_STEPS)
    def _():
        _cast_weight_chunk(step, wout32_ref, wout_ref)

    def merged_rows(rows):
        h = h_ref[rows, :]
        gates = jax.nn.sigmoid(_dot(_rms(h, gmixpre_ref[...]).astype(BF16), wg_ref[...]))
        ya = _dot(oa_ref[rows, :], wa_ref[...])
        yb = _dot(ob_ref[rows, :], wb_ref[...])
        merged = gates[:, 0:D_MODEL] * ya + gates[:, D_MODEL:] * yb
        m = _dot(merged.astype(BF16), wo_ref[...])
        return h + _rms(m, gmix_ref[...])

    @pl.when(step >= MERGE_WEIGHT_STEPS)
    def _():
        parts = _row_parts(h_ref, FFN_ROWS)
        h2 = merged_rows(parts[0])
        for rows, nxt in zip(parts, parts[1:] + [None]):
            h2_next = merged_rows(nxt) if nxt is not None else None
            o_ref[rows, :] = _ffn_rows(h2, gpre_ref, win_ref, wout_ref, gpost_ref)
            h2 = h2_next


def _merge_ffn(h, oa, ob, g_mix_pre, w_gates, w_a, w_b, w_o, g_mix, g_pre, w_in, w_out, g_post):
    t = h.shape[0]
    tm = min(MERGE_STEP_ROWS, t)
    steps = MERGE_WEIGHT_STEPS
    row = _row_tile_spec(tm, steps)
    vec = _const_spec((1, D_MODEL))
    weights = (w_gates, w_a, w_b, w_o, w_in, w_out)
    chunk = [_weight_chunk_spec(w.shape, steps) for w in weights[:-1]]
    chunk.append(_weight_chunk_spec(w_out.shape, MERGE_WOUT_STEPS))
    return pl.pallas_call(
        _merge_ffn_kernel,
        grid=(steps + t // tm,),
        in_specs=[row] * 3 + [vec] + chunk[0:4] + [vec, vec] + chunk[4:6] + [vec],
        out_specs=row,
        out_shape=jax.ShapeDtypeStruct((t, D_MODEL), F32),
        scratch_shapes=[pltpu.VMEM(w.shape, BF16) for w in weights],
        compiler_params=_params(1),
        name="merge_ffn",
    )(h, oa, ob, g_mix_pre, w_gates, w_a, w_b, w_o, g_mix, g_pre, w_in, w_out, g_post)


def _split_w_in(w_in):
    gq, gk, gv, gr, za_f, za_b, aq, ak, av, _, _ = jnp.split(w_in.astype(BF16), IN_OFFSETS, axis=-1)
    pad = jnp.zeros((D_MODEL, V7X_LANES - 2 * GLA_RANK), BF16)
    cat = lambda *parts: jnp.concatenate(parts, axis=-1)
    return cat(za_f, za_b, pad), cat(aq, ak), gr, cat(gq, gk, gv, av)


def _pad_decay_up(up, first_row):
    out = jnp.zeros((V7X_LANES, GLA_KEY_W), BF16)
    return lax.dynamic_update_slice(out, up.astype(BF16), (first_row, 0))


def kernel(x, ffn1_pre_g, ffn1_w_in, ffn1_w_out, ffn1_post_g, mix_pre_g, w_in, gla_decay_up_f,
           gla_decay_bias_f, gla_decay_up_b, gla_decay_bias_b, gla_out_g, w_branch_a, att_q_norm_g,
           att_k_norm_g, w_branch_b, w_out, mix_post_g, ffn2_pre_g, ffn2_w_in, ffn2_w_out, ffn2_post_g):
    batch, seq, d = x.shape
    assert d == D_MODEL and seq % max(GLA_WIDE_C, GRID_W) == 0
    assert seq % ATT_Q_TILE == 0 or seq < ATT_Q_TILE
    assert seq % PROJ_ROW_TILE == 0 or seq < PROJ_ROW_TILE
    depth = w_in.shape[0]
    t = batch * seq
    h = x.reshape(t, d)
    vec = lambda g: g.reshape(1, -1).astype(F32)
    for l in range(depth):
        h = _ffn(h, vec(ffn1_pre_g[l]), ffn1_w_in[l].astype(F32), ffn1_w_out[l].astype(F32),
                 vec(ffn1_post_g[l]))
        gq, gk, gv, sr, lgf, lgb, aq, ak, av, decay_min = _proj(
            h, seq, vec(mix_pre_g[l]), _split_w_in(w_in[l]),
            _pad_decay_up(gla_decay_up_f[l], 0), _pad_decay_up(gla_decay_up_b[l], GLA_RANK),
            vec(gla_decay_bias_f[l]), vec(gla_decay_bias_b[l]),
            vec(att_q_norm_g[l]), vec(att_k_norm_g[l]))
        limits = jnp.stack([_score_bound(att_q_norm_g[l], att_k_norm_g[l]), jnp.min(decay_min)]).astype(F32)
        oa, ob = _mixers(limits, gq, gk, gv, sr, lgf, lgb,
                         gla_out_g[l].reshape(GLA_HEADS, 1, GLA_DV).astype(F32), aq, ak, av, batch, seq)
        h = _merge_ffn(h, oa.reshape(t, GLA_VAL_W), ob.reshape(t, ATT_Q_W), vec(mix_pre_g[l]),
                       w_in[l][:, IN_OFFSETS[-2]:].astype(F32), w_branch_a[l].astype(F32), w_branch_b[l].astype(F32), w_out[l].astype(F32),
                       vec(mix_post_g[l]), vec(ffn2_pre_g[l]), ffn2_w_in[l].astype(F32),
                       ffn2_w_out[l].astype(F32), vec(ffn2_post_g[l]))
    return h.reshape(batch, seq, d)
```
